```python
import math
import jax, jax.numpy as jnp
from jax import lax
import numpy as np

D_MODEL = 1024
BATCH = 2
SEQ = 8192
DEPTH = 4
DEC_BATCH = 32
DEC_SEQ = 1
PAST_LEN = 8192
PAGE_SIZE = 128

HEAD_DIM = 64
S5_WIDTH = D_MODEL // 4
S5_GROUP = 16
S5_GROUPS = S5_WIDTH // S5_GROUP
S5_STATE = 64
RW_WIDTH = 3 * D_MODEL // 8
RW_HEADS = RW_WIDTH // HEAD_DIM
ATT_WIDTH = D_MODEL - S5_WIDTH - RW_WIDTH
ATT_HEADS = ATT_WIDTH // HEAD_DIM
MIX_WIDTH = S5_WIDTH + RW_WIDTH + ATT_WIDTH
N_IN = S5_WIDTH + 4 * RW_WIDTH + 3 * ATT_WIDTH
RW_DECAY_LORA = 64
RW_A_LORA = 64
RW_GATE_LORA = 128
RW_GN_EPS = 64e-5
DILATED_PATTERNS = ((128, 1), (512, 4), (2048, 16))
WIN_MAX = max(w for w, _ in DILATED_PATTERNS)
WIN_BUF = min(WIN_MAX, PAST_LEN)
ATT_Q_BLOCK = 128
ATT_SCALE = HEAD_DIM ** -0.5
ROPE_THETA = 500000.0
ROPE_DIM = HEAD_DIM // 4
NEG_INF = -1e30
N_EXPERTS = 64
N_EXPERT_GROUPS = 8
TOPK_GROUPS = 4
TOP_K = 8
EXPERT_FF = D_MODEL // 4
SHARED_FF = EXPERT_FF
ROUTED_SCALE = 2.5
MOE_BLOCK = 128
LN_EPS = 1e-5
DN_ALPHA = (2 * DEPTH) ** 0.25
DN_BETA = (8 * DEPTH) ** -0.25

kernel_name = 'hymba_s5_rwkv7_dilated_moe_step'


def _f32(t):
    return t.astype(jnp.float32)


def layer_norm(x, g, b):
    xf = _f32(x)
    mu = xf.mean(-1, keepdims=True)
    var = jnp.mean(jnp.square(xf - mu), -1, keepdims=True)
    return ((xf - mu) * lax.rsqrt(var + LN_EPS) * _f32(g) + _f32(b)).astype(x.dtype)


def s5_mixer(u, h0_re, h0_im, lam_re, lam_im, b_re, b_im, c_re, c_im, d_skip, log_step, w_glu, b_glu):
    bsz, L, _ = u.shape
    ug = _f32(u).reshape(bsz, L, S5_GROUPS, S5_GROUP)
    lr, li = _f32(lam_re), _f32(lam_im)
    dt = jnp.exp(_f32(log_step))[:, None]
    mag = jnp.exp(lr * dt)
    ab_re, ab_im = mag * jnp.cos(li * dt), mag * jnp.sin(li * dt)
    den = lr * lr + li * li
    cf_re = ((ab_re - 1.0) * lr + ab_im * li) / den
    cf_im = (ab_im * lr - (ab_re - 1.0) * li) / den
    br, bi = _f32(b_re), _f32(b_im)
    bb_re = cf_re[..., None] * br - cf_im[..., None] * bi
    bb_im = cf_re[..., None] * bi + cf_im[..., None] * br
    bu_re = jnp.einsum('gpc,blgc->blgp', bb_re, ug)
    bu_im = jnp.einsum('gpc,blgc->blgp', bb_im, ug)
    a_re = jnp.broadcast_to(ab_re, bu_re.shape)
    a_im = jnp.broadcast_to(ab_im, bu_im.shape)

    def combine(e1, e2):
        a1r, a1i, b1r, b1i = e1
        a2r, a2i, b2r, b2i = e2
        return (a1r * a2r - a1i * a2i, a1r * a2i + a1i * a2r,
                a2r * b1r - a2i * b1i + b2r, a2r * b1i + a2i * b1r + b2i)

    A_re, A_im, H_re, H_im = lax.associative_scan(combine, (a_re, a_im, bu_re, bu_im), axis=1)
    h0r, h0i = _f32(h0_re)[:, None], _f32(h0_im)[:, None]
    h_re = A_re * h0r - A_im * h0i + H_re
    h_im = A_re * h0i + A_im * h0r + H_im
    y = (jnp.einsum('gcp,blgp->blgc', _f32(c_re), h_re)
         - jnp.einsum('gcp,blgp->blgc', _f32(c_im), h_im)
         + _f32(d_skip) * ug)
    y = jax.nn.gelu(y.reshape(bsz, L, S5_WIDTH))
    y = y * jax.nn.sigmoid(y @ _f32(w_glu) + _f32(b_glu))
    return y, h_re[:, -1], h_im[:, -1]


def rwkv7_mixer(proj, prev_row, S0, mu, w0, w1, w2, a0, a1, a2, g1, g2, k_k, k_a, r_k, gn_g, gn_b):
    bsz, L, _ = proj.shape
    p = _f32(proj)
    p_prev = jnp.concatenate([_f32(prev_row)[:, None], p[:, :-1]], axis=1)
    dp = p_prev - p
    r_in, k_in, v_in, z = jnp.split(p, 4, axis=-1)
    dr, dk, dv, dz = jnp.split(dp, 4, axis=-1)
    mu = _f32(mu)
    r = r_in + dr * mu[0]
    xw = z + dz * mu[1]
    k = k_in + dk * mu[2]
    v = v_in + dv * mu[3]
    xa = z + dz * mu[4]
    xg = z + dz * mu[5]
    w_log = -jax.nn.softplus(-(_f32(w0) + jnp.tanh(xw @ _f32(w1)) @ _f32(w2))) - 0.5
    decay = jnp.exp(-jnp.exp(w_log))
    a = jax.nn.sigmoid(_f32(a0) + (xa @ _f32(a1)) @ _f32(a2))
    g = jax.nn.sigmoid(xg @ _f32(g1)) @ _f32(g2)
    heads = lambda t: t.reshape(bsz, L, RW_HEADS, HEAD_DIM)
    kk = heads(k * _f32(k_k))
    kk = kk * lax.rsqrt(jnp.maximum(jnp.sum(kk * kk, -1, keepdims=True), 1e-24))
    k = k * (1.0 + (a - 1.0) * _f32(k_a))
    rh, wh, kh, vh, ah = heads(r), heads(decay), heads(k), heads(v), heads(a)

    def step(S, inp):
        r_t, w_t, k_t, v_t, kk_t, a_t = inp
        sa = jnp.einsum('bhvk,bhk->bhv', S, -kk_t)
        S = (S * w_t[:, :, None, :] + sa[..., None] * (kk_t * a_t)[:, :, None, :]
             + v_t[..., None] * k_t[:, :, None, :])
        return S, jnp.einsum('bhvk,bhk->bhv', S, r_t)

    tm = lambda t: jnp.moveaxis(t, 1, 0)
    S_fin, y = lax.scan(step, _f32(S0), (tm(rh), tm(wh), tm(kh), tm(vh), tm(kk), tm(ah)))
    y = jnp.moveaxis(y, 0, 1)
    m = y.mean(-1, keepdims=True)
    var = jnp.mean(jnp.square(y - m), -1, keepdims=True)
    y = ((y - m) * lax.rsqrt(var + RW_GN_EPS)).reshape(bsz, L, RW_WIDTH) * _f32(gn_g) + _f32(gn_b)
    bonus = jnp.sum(rh * kh * _f32(r_k), -1, keepdims=True) * vh
    y = (y + bonus.reshape(bsz, L, RW_WIDTH)) * g
    return y, S_fin, p[:, -1]


def partial_rope(x, pos):
    half = ROPE_DIM // 2
    inv_freq = jnp.exp(-math.log(ROPE_THETA) * jnp.arange(half, dtype=jnp.float32) * (2.0 / ROPE_DIM))
    ang = pos.astype(jnp.float32)[:, None] * inv_freq[None, :]
    cos = jnp.cos(ang)[None, :, None, :]
    sin = jnp.sin(ang)[None, :, None, :]
    x1, x2 = x[..., :half], x[..., half:ROPE_DIM]
    return jnp.concatenate([x1 * cos - x2 * sin, x2 * cos + x1 * sin, x[..., ROPE_DIM:]], axis=-1)


def dilated_attention(q, k_all, v_all, q_idx):
    lses, outs = [], []
    for window, dil in DILATED_PATTERNS:
        offs = dil * jnp.arange(window // dil + 1)
        idx = q_idx[:, None] - offs[None, :]
        valid = idx >= 0
        idc = jnp.maximum(idx, 0)
        kg = k_all[:, idc]
        vg = v_all[:, idc]
        s = jnp.einsum('bqhd,bqnhd->bhqn', q, kg) * ATT_SCALE
        s = jnp.where(valid, s, NEG_INF)
        m = s.max(-1, keepdims=True)
        p = jnp.exp(s - m)
        den = p.sum(-1, keepdims=True)
        outs.append(jnp.einsum('bhqn,bqnhd->bqhd', p / den, vg))
        lses.append((m + jnp.log(den))[..., 0])
    wts = jax.nn.softmax(jnp.stack(lses, 0), axis=0)
    return jnp.einsum('gbhq,gbqhd->bqhd', wts, jnp.stack(outs, 0))


def dilated_attn_mixer(q, k, v, k_buf, v_buf, pos0):
    bsz, L = q.shape[:2]
    pos = pos0 + jnp.arange(L)
    hd = lambda t: _f32(t).reshape(bsz, L, ATT_HEADS, HEAD_DIM)
    qh = partial_rope(hd(q), pos)
    kh = partial_rope(hd(k), pos)
    vh = hd(v)
    buf_len = k_buf.shape[1]
    k_all = jnp.concatenate([_f32(k_buf), kh], axis=1)
    v_all = jnp.concatenate([_f32(v_buf), vh], axis=1)
    q_idx = buf_len + jnp.arange(L)
    if L > ATT_Q_BLOCK and L % ATT_Q_BLOCK == 0:
        nb = L // ATT_Q_BLOCK
        qb = jnp.moveaxis(qh.reshape(bsz, nb, ATT_Q_BLOCK, ATT_HEADS, HEAD_DIM), 1, 0)
        ob = lax.map(lambda blk: dilated_attention(blk[0], k_all, v_all, blk[1]),
                     (qb, q_idx.reshape(nb, ATT_Q_BLOCK)))
        o = jnp.moveaxis(ob, 0, 1).reshape(bsz, L, ATT_WIDTH)
    else:
        o = dilated_attention(qh, k_all, v_all, q_idx).reshape(bsz, L, ATT_WIDTH)
    keep = min(WIN_MAX, max(buf_len, L))
    return o, k_all[:, -keep:], v_all[:, -keep:]


def route(xt, w_router, router_bias):
    scores = jax.nn.sigmoid(_f32(xt) @ _f32(w_router))
    sel = scores + _f32(router_bias)
    per_group = N_EXPERTS // N_EXPERT_GROUPS
    grp = lax.top_k(sel.reshape(-1, N_EXPERT_GROUPS, per_group), 2)[0].sum(-1)
    _, top_g = lax.top_k(grp, TOPK_GROUPS)
    gmask = jax.nn.one_hot(top_g, N_EXPERT_GROUPS, dtype=jnp.float32).sum(1)
    emask = jnp.repeat(gmask, per_group, axis=1)
    _, idx = lax.top_k(jnp.where(emask > 0, sel, NEG_INF), TOP_K)
    w = jnp.take_along_axis(scores, idx, axis=1)
    w = w / w.sum(-1, keepdims=True) * ROUTED_SCALE
    return idx, w


def routed_experts(xt, idx, gates, e_gate, e_up, e_down):
    n_tok, d = xt.shape
    n_rows = n_tok * TOP_K
    n_blocks = -(-n_rows // MOE_BLOCK) + N_EXPERTS
    flat_e = idx.reshape(-1)
    order = jnp.argsort(flat_e)
    sorted_e = flat_e[order]
    counts = jnp.bincount(flat_e, length=N_EXPERTS)
    starts = jnp.cumsum(counts) - counts
    padded = (counts + MOE_BLOCK - 1) // MOE_BLOCK * MOE_BLOCK
    pad_ends = jnp.cumsum(padded)
    pad_starts = pad_ends - padded
    dest = pad_starts[sorted_e] + jnp.arange(n_rows) - starts[sorted_e]
    n_slots = n_blocks * MOE_BLOCK
    row_tok = jnp.full((n_slots,), n_tok, jnp.int32).at[dest].set((order // TOP_K).astype(jnp.int32))
    row_gate = jnp.zeros((n_slots,), jnp.float32).at[dest].set(gates.reshape(-1)[order])
    block_e = jnp.minimum(jnp.searchsorted(pad_ends, jnp.arange(n_blocks) * MOE_BLOCK, side='right'),
                          N_EXPERTS - 1)
    x_pad = jnp.concatenate([xt, jnp.zeros((1, d), xt.dtype)], axis=0)

    def body(acc, blk):
        rows, g, e = blk
        xb = x_pad[rows]
        h = jax.nn.silu(xb @ e_gate[e]) * (xb @ e_up[e])
        return acc.at[rows].add(_f32(h @ e_down[e]) * g[:, None]), None

    acc, _ = lax.scan(body, jnp.zeros((n_tok + 1, d), jnp.float32),
                      (row_tok.reshape(n_blocks, MOE_BLOCK), row_gate.reshape(n_blocks, MOE_BLOCK), block_e))
    return acc[:n_tok].astype(xt.dtype)


def moe_ffn(x, w_router, router_bias, e_gate, e_up, e_down, s_gate, s_up, s_down):
    bsz, L, d = x.shape
    xt = x.reshape(-1, d)
    idx, gates = route(xt, w_router, router_bias)
    routed = routed_experts(xt, idx, gates, e_gate, e_up, e_down)
    shared = (jax.nn.silu(xt @ s_gate) * (xt @ s_up)) @ s_down
    return (routed + shared).reshape(bsz, L, d)


def trunk_layer(x, pos0, h0_re, h0_im, rw_S0, rw_prev, k_buf, v_buf, lp):
    proj = x @ lp['w_in']
    o1 = S5_WIDTH
    o2 = o1 + 4 * RW_WIDTH
    o3 = o2 + ATT_WIDTH
    o4 = o3 + ATT_WIDTH
    ya, h_re, h_im = s5_mixer(proj[..., :o1], h0_re, h0_im, lp['s5_lambda_re'], lp['s5_lambda_im'],
                              lp['s5_b_re'], lp['s5_b_im'], lp['s5_c_re'], lp['s5_c_im'], lp['s5_d'],
                              lp['s5_log_step'], lp['s5_w_glu'], lp['s5_b_glu'])
    yb, rw_S, rw_row = rwkv7_mixer(proj[..., o1:o2], rw_prev, rw_S0, lp['rw_mu'], lp['rw_w0'], lp['rw_w1'],
                                   lp['rw_w2'], lp['rw_a0'], lp['rw_a1'], lp['rw_a2'], lp['rw_g1'], lp['rw_g2'],
                                   lp['rw_k_k'], lp['rw_k_a'], lp['rw_r_k'], lp['rw_gn_g'], lp['rw_gn_b'])
    yc, kb, vb = dilated_attn_mixer(proj[..., o2:o3], proj[..., o3:o4], proj[..., o4:], k_buf, v_buf, pos0)
    mix = jnp.concatenate([ya, yb, yc], axis=-1).astype(x.dtype) @ lp['w_out']
    x = layer_norm(DN_ALPHA * x + mix, lp['ln_g'][0], lp['ln_b'][0])
    ffn = moe_ffn(x, lp['w_router'], lp['router_bias'], lp['expert_w_gate'], lp['expert_w_up'],
                  lp['expert_w_down'], lp['shared_w_gate'], lp['shared_w_up'], lp['shared_w_down'])
    x = layer_norm(DN_ALPHA * x + ffn, lp['ln_g'][1], lp['ln_b'][1])
    return x, (h_re, h_im, rw_S, rw_row, kb, vb)


def setup_inputs(seed: int = 0) -> dict:
    key = jax.random.key(seed)
    keys = iter(jax.random.split(key, 64))

    def nrm(shape, scale):
        return scale * jax.random.normal(next(keys), shape, jnp.float32)

    def unif(shape, lo, hi):
        return jax.random.uniform(next(keys), shape, jnp.float32, lo, hi)

    L, G, P, C, H, N = DEPTH, S5_GROUPS, S5_STATE, S5_GROUP, RW_HEADS, HEAD_DIM
    E, F, D = N_EXPERTS, EXPERT_FF, D_MODEL
    return {
        'x_prompt': nrm((BATCH, SEQ, D), 1.0),
        'x_sample': nrm((DEC_BATCH, DEC_SEQ, D), 1.0),
        'state_s5_re': nrm((L, DEC_BATCH, G, P), 0.3),
        'state_s5_im': nrm((L, DEC_BATCH, G, P), 0.3),
        'state_rwkv': nrm((L, DEC_BATCH, H, N, N), 0.5),
        'state_rwkv_shift': nrm((L, DEC_BATCH, 4 * RW_WIDTH), 1.0),
        'cache_attn_k': nrm((L, DEC_BATCH, WIN_BUF, ATT_HEADS, HEAD_DIM), 1.0),
        'cache_attn_v': nrm((L, DEC_BATCH, WIN_BUF, ATT_HEADS, HEAD_DIM), 1.0),
        'w_in': nrm((L, D, N_IN), D ** -0.5),
        's5_lambda_re': -0.5 + nrm((L, G, P), 0.01),
        's5_lambda_im': math.pi * jnp.arange(P, dtype=jnp.float32) + nrm((L, G, P), 0.01),
        's5_b_re': nrm((L, G, P, C), (2 * C) ** -0.5),
        's5_b_im': nrm((L, G, P, C), (2 * C) ** -0.5),
        's5_c_re': nrm((L, G, C, P), (2 * P) ** -0.5),
        's5_c_im': nrm((L, G, C, P), (2 * P) ** -0.5),
        's5_d': nrm((L, G, C), 0.5),
        's5_log_step': unif((L, G), math.log(1e-3), math.log(1e-1)),
        's5_w_glu': nrm((L, S5_WIDTH, S5_WIDTH), S5_WIDTH ** -0.5),
        's5_b_glu': nrm((L, S5_WIDTH), 0.02),
        'rw_mu': unif((L, 6, RW_WIDTH), 0.0, 1.0),
        'rw_w0': unif((L, RW_WIDTH), -5.5, 0.5),
        'rw_w1': nrm((L, RW_WIDTH, RW_DECAY_LORA), RW_WIDTH ** -0.5),
        'rw_w2': nrm((L, RW_DECAY_LORA, RW_WIDTH), 0.1 * RW_DECAY_LORA ** -0.5),
        'rw_a0': nrm((L, RW_WIDTH), 0.1),
        'rw_a1': nrm((L, RW_WIDTH, RW_A_LORA), RW_WIDTH ** -0.5),
        'rw_a2': nrm((L, RW_A_LORA, RW_WIDTH), 0.5 * RW_A_LORA ** -0.5),
        'rw_g1': nrm((L, RW_WIDTH, RW_GATE_LORA), RW_WIDTH ** -0.5),
        'rw_g2': nrm((L, RW_GATE_LORA, RW_WIDTH), RW_GATE_LORA ** -0.5),
        'rw_k_k': 0.85 + nrm((L, RW_WIDTH), 0.02),
        'rw_k_a': 1.0 + nrm((L, RW_WIDTH), 0.02),
        'rw_r_k': nrm((L, H, N), 0.1),
        'rw_gn_g': 1.0 + nrm((L, RW_WIDTH), 0.02),
        'rw_gn_b': nrm((L, RW_WIDTH), 0.02),
        'w_out': nrm((L, MIX_WIDTH, D), DN_BETA * MIX_WIDTH ** -0.5),
        'ln_g': 1.0 + nrm((L, 2, D), 0.02),
        'ln_b': nrm((L, 2, D), 0.02),
        'w_router': nrm((L, D, E), D ** -0.5),
        'router_bias': nrm((L, E), 0.01),
        'expert_w_gate': nrm((L, E, D, F), D ** -0.5),
        'expert_w_up': nrm((L, E, D, F), D ** -0.5),
        'expert_w_down': nrm((L, E, F, D), DN_BETA * F ** -0.5),
        'shared_w_gate': nrm((L, D, SHARED_FF), D ** -0.5),
        'shared_w_up': nrm((L, D, SHARED_FF), D ** -0.5),
        'shared_w_down': nrm((L, SHARED_FF, D), DN_BETA * SHARED_FF ** -0.5),
    }


def reference(x_prompt, x_sample, state_s5_re, state_s5_im, state_rwkv, state_rwkv_shift, cache_attn_k,
              cache_attn_v, w_in, s5_lambda_re, s5_lambda_im, s5_b_re, s5_b_im, s5_c_re, s5_c_im, s5_d,
              s5_log_step, s5_w_glu, s5_b_glu, rw_mu, rw_w0, rw_w1, rw_w2, rw_a0, rw_a1, rw_a2, rw_g1, rw_g2,
              rw_k_k, rw_k_a, rw_r_k, rw_gn_g, rw_gn_b, w_out, ln_g, ln_b, w_router, router_bias,
              expert_w_gate, expert_w_up, expert_w_down, shared_w_gate, shared_w_up, shared_w_down):
    y_prompt, y_sample = x_prompt, x_sample
    new_p = [[], [], [], [], [], []]
    new_s = [[], [], [], [], [], []]
    bsz = x_prompt.shape[0]
    for l in range(DEPTH):
        lp = {
            'w_in': w_in[l], 's5_lambda_re': s5_lambda_re[l], 's5_lambda_im': s5_lambda_im[l],
            's5_b_re': s5_b_re[l], 's5_b_im': s5_b_im[l], 's5_c_re': s5_c_re[l], 's5_c_im': s5_c_im[l],
            's5_d': s5_d[l], 's5_log_step': s5_log_step[l], 's5_w_glu': s5_w_glu[l], 's5_b_glu': s5_b_glu[l],
            'rw_mu': rw_mu[l], 'rw_w0': rw_w0[l], 'rw_w1': rw_w1[l], 'rw_w2': rw_w2[l], 'rw_a0': rw_a0[l],
            'rw_a1': rw_a1[l], 'rw_a2': rw_a2[l], 'rw_g1': rw_g1[l], 'rw_g2': rw_g2[l], 'rw_k_k': rw_k_k[l],
            'rw_k_a': rw_k_a[l], 'rw_r_k': rw_r_k[l], 'rw_gn_g': rw_gn_g[l], 'rw_gn_b': rw_gn_b[l],
            'w_out': w_out[l], 'ln_g': ln_g[l], 'ln_b': ln_b[l], 'w_router': w_router[l],
            'router_bias': router_bias[l], 'expert_w_gate': expert_w_gate[l], 'expert_w_up': expert_w_up[l],
            'expert_w_down': expert_w_down[l], 'shared_w_gate': shared_w_gate[l],
            'shared_w_up': shared_w_up[l], 'shared_w_down': shared_w_down[l],
        }
        zs = jnp.zeros((bsz, S5_GROUPS, S5_STATE), jnp.float32)
        zS = jnp.zeros((bsz, RW_HEADS, HEAD_DIM, HEAD_DIM), jnp.float32)
        zrow = jnp.zeros((bsz, 4 * RW_WIDTH), jnp.float32)
        zbuf = jnp.zeros((bsz, 0, ATT_HEADS, HEAD_DIM), jnp.float32)
        y_prompt, st_p = trunk_layer(y_prompt, 0, zs, zs, zS, zrow, zbuf, zbuf, lp)
        y_sample, st_s = trunk_layer(y_sample, PAST_LEN, state_s5_re[l], state_s5_im[l], state_rwkv[l],
                                     state_rwkv_shift[l], cache_attn_k[l], cache_attn_v[l], lp)
        for lst, st in zip(new_p, st_p):
            lst.append(st)
        for lst, st in zip(new_s, st_s):
            lst.append(st)
    refs = (state_s5_re, state_s5_im, state_rwkv, state_rwkv_shift, cache_attn_k, cache_attn_v)
    p_s5_re, p_s5_im, p_rwkv, p_shift, p_k, p_v = [jnp.stack(v, 0).astype(r.dtype) for v, r in zip(new_p, refs)]
    s_s5_re, s_s5_im, s_rwkv, s_shift, s_k, s_v = [jnp.stack(v, 0).astype(r.dtype) for v, r in zip(new_s, refs)]
    return (y_prompt, y_sample, p_s5_re, p_s5_im, p_rwkv, p_shift, p_k, p_v,
            s_s5_re, s_s5_im, s_rwkv, s_shift, s_k, s_v)
```

```python
import functools
import math

import numpy as np
import jax
import jax.numpy as jnp
from jax import lax
from jax.experimental import pallas as pl
from jax.experimental.pallas import tpu as pltpu

F32 = jnp.float32
BF16 = jnp.bfloat16

D_MODEL = 1024
PAST_LEN = 8192
HEAD_DIM = 64
S5_WIDTH = 256
S5_GROUP = 16
S5_GROUPS = 16
S5_STATE = 64
RW_WIDTH = 384
RW_HEADS = 6
ATT_WIDTH = 384
ATT_HEADS = 6
N_IN = S5_WIDTH + 4 * RW_WIDTH + 3 * ATT_WIDTH
RW_GN_EPS = 64e-5
DILATED_PATTERNS = ((128, 1), (512, 4), (2048, 16))
WIN_MAX = 2048
ATT_SCALE = HEAD_DIM ** -0.5
ROPE_THETA = 500000.0
ROPE_DIM = HEAD_DIM // 4
NEG_INF = -1e30
N_EXPERTS = 64
N_EXPERT_GROUPS = 8
TOPK_GROUPS = 4
TOP_K = 8
EXPERT_FF = 256
ROUTED_SCALE = 2.5
LN_EPS = 1e-5

S5_CHUNK = 16
RW_CHUNK = 64
ATT_BLOCK = 256
VMEM_LIMIT = 56 * 1024 * 1024


def _cparams(*sem):
    return pltpu.CompilerParams(dimension_semantics=sem, vmem_limit_bytes=VMEM_LIMIT)


def _bdot(a, b):
    return jnp.dot(a.astype(BF16), b.astype(BF16), preferred_element_type=F32)


def _split(x):
    hi = x.astype(BF16)
    lo = (x - hi.astype(F32)).astype(BF16)
    return hi, lo


def _dot3(a, b):
    ah, al = _split(a)
    bh, bl = _split(b)
    d = functools.partial(jnp.dot, preferred_element_type=F32)
    return d(ah, bh) + (d(ah, bl) + d(al, bh))


def _dot2_exact_rhs(a, b_bf16):
    ah, al = _split(a)
    d = functools.partial(jnp.dot, preferred_element_type=F32)
    return d(ah, b_bf16) + d(al, b_bf16)


def _sigmoid(x):
    return 1.0 / (1.0 + jnp.exp(-x))


def _silu(x):
    return x * _sigmoid(x)


def _layer_norm(x, g, b):
    mu = jnp.mean(x, axis=-1, keepdims=True)
    xc = x - mu
    var = jnp.mean(xc * xc, axis=-1, keepdims=True)
    return xc * lax.rsqrt(var + LN_EPS) * g + b


def _row_tile(t, pref):
    tm = min(t, pref)
    assert t % tm == 0
    return tm


def _proj_kernel(x_ref, w_ref, u_ref, rw_ref, qkv_ref):
    p = jnp.dot(x_ref[...].astype(BF16), w_ref[...], preferred_element_type=F32)
    o1 = S5_WIDTH
    o2 = o1 + 4 * RW_WIDTH
    u_ref[...] = p[:, :o1]
    rw_ref[...] = p[:, o1:o2]
    qkv_ref[...] = p[:, o2:]


def _proj(x, w_bf16):
    t = x.shape[0]
    tm = _row_tile(t, 512)
    return pl.pallas_call(
        _proj_kernel,
        grid=(t // tm,),
        in_specs=[pl.BlockSpec((tm, D_MODEL), lambda i: (i, 0)),
                  pl.BlockSpec((D_MODEL, N_IN), lambda i: (0, 0))],
        out_specs=[pl.BlockSpec((tm, S5_WIDTH), lambda i: (i, 0)),
                   pl.BlockSpec((tm, 4 * RW_WIDTH), lambda i: (i, 0)),
                   pl.BlockSpec((tm, 3 * ATT_WIDTH), lambda i: (i, 0))],
        out_shape=[jax.ShapeDtypeStruct((t, S5_WIDTH), F32),
                   jax.ShapeDtypeStruct((t, 4 * RW_WIDTH), F32),
                   jax.ShapeDtypeStruct((t, 3 * ATT_WIDTH), F32)],
        compiler_params=_cparams("parallel"),
        name="proj",
    )(x, w_bf16)


def _s5_discretize(lr, li, log_step):
    dt = jnp.exp(log_step)
    mag = jnp.exp(lr * dt)
    ab_re = mag * jnp.cos(li * dt)
    ab_im = mag * jnp.sin(li * dt)
    den = lr * lr + li * li
    cf_re = ((ab_re - 1.0) * lr + ab_im * li) / den
    cf_im = (ab_im * lr - (ab_re - 1.0) * li) / den
    return dt, ab_re, ab_im, cf_re, cf_im


def _s5_power(lr, li, dt, n):
    mag = jnp.exp(n * (lr * dt))
    ang = n * (li * dt)
    return mag * jnp.cos(ang), mag * jnp.sin(ang)


def _s5_prep_kernel(lr_ref, li_ref, ls_ref, lrc_ref, lic_ref, lsc_ref, bt_re_ref, bt_im_ref,
                    ct_re_ref, ct_im_ref,
                    kmat_ref, bout_re_ref, bout_im_ref, win_re_ref, win_im_ref,
                    aq_re_ref, aq_im_ref, ab_re_ref, ab_im_ref, bb_re_ref, bb_im_ref):
    q = S5_CHUNK
    n_rows = q * S5_GROUP
    lr, li, ls = lr_ref[0], li_ref[0], ls_ref[0]
    dt, ab_re, ab_im, cf_re, cf_im = _s5_discretize(lr, li, ls)
    ab_re_ref[0] = ab_re
    ab_im_ref[0] = ab_im
    aq_re, aq_im = _s5_power(lr, li, dt, jnp.float32(q))
    aq_re_ref[0] = aq_re
    aq_im_ref[0] = aq_im
    bt_re, bt_im = bt_re_ref[0], bt_im_ref[0]
    bb_re = cf_re * bt_re - cf_im * bt_im
    bb_im = cf_re * bt_im + cf_im * bt_re
    bb_re_ref[0] = bb_re[:S5_GROUP]
    bb_im_ref[0] = bb_im[:S5_GROUP]
    step = lax.broadcasted_iota(jnp.int32, (n_rows, S5_STATE), 0) // S5_GROUP
    pw_re, pw_im = _s5_power(lr, li, dt, (q - 1 - step).astype(F32))
    bout_re_ref[0] = (bb_re * pw_re - bb_im * pw_im).astype(BF16)
    bout_im_ref[0] = (bb_re * pw_im + bb_im * pw_re).astype(BF16)
    lrc, lic, lsc = lrc_ref[0], lic_ref[0], lsc_ref[0]
    dtc, abc_re, abc_im, _, _ = _s5_discretize(lrc, lic, lsc)
    lag = (lax.broadcasted_iota(jnp.int32, (S5_STATE, n_rows), 1) // S5_GROUP).astype(F32)
    p0_re, p0_im = _s5_power(lrc, lic, dtc, lag)
    ct_re, ct_im = ct_re_ref[0], ct_im_ref[0]
    e0_re = ct_re * p0_re - ct_im * p0_im
    e0_im = ct_re * p0_im + ct_im * p0_re
    e1_re = e0_re * abc_re - e0_im * abc_im
    e1_im = e0_re * abc_im + e0_im * abc_re
    win_re_ref[0] = e1_re.astype(BF16)
    win_im_ref[0] = (-e1_im).astype(BF16)
    g = _dot3(bb_re[:S5_GROUP], e0_re) - _dot3(bb_im[:S5_GROUP], e0_im)
    lane = lax.broadcasted_iota(jnp.int32, (S5_GROUP, n_rows), 1)
    blocks = []
    for i in range(q):
        if i == 0:
            blocks.append(g)
        else:
            shifted = pltpu.roll(g, S5_GROUP * i, axis=1)
            blocks.append(jnp.where(lane >= S5_GROUP * i, shifted, 0.0))
    kmat_ref[0] = jnp.concatenate(blocks, axis=0).astype(BF16)


def _s5_prep(lam_re, lam_im, log_step, b_re, b_im, c_re, c_im):
    g, p, c, q = S5_GROUPS, S5_STATE, S5_GROUP, S5_CHUNK
    n = q * c
    row = lambda a: a.reshape(g, 1, p)
    col = lambda a: jnp.broadcast_to(a.reshape(g, p, 1), (g, p, n))
    ls_row = jnp.broadcast_to(log_step.reshape(g, 1, 1), (g, 1, p))
    ls_col = jnp.broadcast_to(log_step.reshape(g, 1, 1), (g, p, n))
    bt = lambda a: jnp.tile(jnp.swapaxes(a, 1, 2), (1, q, 1))
    ct = lambda a: jnp.tile(jnp.swapaxes(a, 1, 2), (1, 1, q))
    spec = lambda *s: pl.BlockSpec((1,) + s, lambda i: (i, 0, 0))
    sds = lambda s, d: jax.ShapeDtypeStruct((g,) + s, d)
    return pl.pallas_call(
        _s5_prep_kernel,
        grid=(g,),
        in_specs=[spec(1, p), spec(1, p), spec(1, p), spec(p, n), spec(p, n), spec(p, n),
                  spec(n, p), spec(n, p), spec(p, n), spec(p, n)],
        out_specs=[spec(n, n), spec(n, p), spec(n, p), spec(p, n), spec(p, n),
                   spec(1, p), spec(1, p), spec(1, p), spec(1, p), spec(c, p), spec(c, p)],
        out_shape=[sds((n, n), BF16), sds((n, p), BF16), sds((n, p), BF16), sds((p, n), BF16),
                   sds((p, n), BF16), sds((1, p), F32), sds((1, p), F32), sds((1, p), F32),
                   sds((1, p), F32), sds((c, p), F32), sds((c, p), F32)],
        compiler_params=_cparams("parallel"),
        name="s5_prep",
    )(row(lam_re), row(lam_im), ls_row, col(lam_re), col(lam_im), ls_col,
      bt(b_re), bt(b_im), ct(c_re), ct(c_im))


def _s5_scan_kernel(x_ref, kmat_ref, bout_re_ref, bout_im_ref, win_re_ref, win_im_ref,
                    aq_re_ref, aq_im_ref, d_ref, h0_re_ref, h0_im_ref,
                    y_ref, hf_re_ref, hf_im_ref,
                    s_re, s_im, hin_re, hin_im, *, bsz, n_chunks):
    x = x_ref[0]
    xb = x.astype(BF16)
    s_re[...] = jnp.dot(xb, bout_re_ref[0], preferred_element_type=F32)
    s_im[...] = jnp.dot(xb, bout_im_ref[0], preferred_element_type=F32)
    ar, ai = aq_re_ref[0], aq_im_ref[0]

    def body(k, carry):
        new = []
        for b in range(bsz):
            hr, hi = carry[2 * b], carry[2 * b + 1]
            r = b * n_chunks + k
            hin_re[pl.ds(r, 1), :] = hr
            hin_im[pl.ds(r, 1), :] = hi
            sr = s_re[pl.ds(r, 1), :]
            si = s_im[pl.ds(r, 1), :]
            new.append(ar * hr - ai * hi + sr)
            new.append(ar * hi + ai * hr + si)
        return tuple(new)

    init = []
    for b in range(bsz):
        init.append(h0_re_ref[0, b:b + 1, :])
        init.append(h0_im_ref[0, b:b + 1, :])
    fin = lax.fori_loop(0, n_chunks, body, tuple(init))
    for b in range(bsz):
        hf_re_ref[0, b:b + 1, :] = fin[2 * b]
        hf_im_ref[0, b:b + 1, :] = fin[2 * b + 1]
    y = jnp.dot(xb, kmat_ref[0], preferred_element_type=F32)
    y = y + jnp.dot(hin_re[...].astype(BF16), win_re_ref[0], preferred_element_type=F32)
    y = y + jnp.dot(hin_im[...].astype(BF16), win_im_ref[0], preferred_element_type=F32)
    y_ref[0] = y + x * d_ref[0]


def _s5_scan(u, mats, d_skip, h0_re, h0_im, bsz, seq):
    g, p, c, q = S5_GROUPS, S5_STATE, S5_GROUP, S5_CHUNK
    n = q * c
    kmat, bout_re, bout_im, win_re, win_im, aq_re, aq_im = mats
    n_chunks = seq // q
    rows = bsz * n_chunks
    x = u.reshape(bsz, n_chunks, q, g, c).transpose(3, 0, 1, 2, 4).reshape(g, rows, n)
    d_t = jnp.tile(d_skip.reshape(g, 1, c), (1, 1, q))
    h0r = jnp.swapaxes(h0_re, 0, 1)
    h0i = jnp.swapaxes(h0_im, 0, 1)
    spec = lambda *s: pl.BlockSpec((1,) + s, lambda i: (i, 0, 0))
    y, hf_re, hf_im = pl.pallas_call(
        functools.partial(_s5_scan_kernel, bsz=bsz, n_chunks=n_chunks),
        grid=(g,),
        in_specs=[spec(rows, n), spec(n, n), spec(n, p), spec(n, p), spec(p, n), spec(p, n),
                  spec(1, p), spec(1, p), spec(1, n), spec(bsz, p), spec(bsz, p)],
        out_specs=[spec(rows, n), spec(bsz, p), spec(bsz, p)],
        out_shape=[jax.ShapeDtypeStruct((g, rows, n), F32),
                   jax.ShapeDtypeStruct((g, bsz, p), F32),
                   jax.ShapeDtypeStruct((g, bsz, p), F32)],
        scratch_shapes=[pltpu.VMEM((rows, p), F32)] * 4,
        compiler_params=_cparams("parallel"),
        name="s5_scan",
    )(x, kmat, bout_re, bout_im, win_re, win_im, aq_re, aq_im, d_t, h0r, h0i)
    y = y.reshape(g, bsz, n_chunks, q, c).transpose(1, 2, 3, 0, 4).reshape(bsz * seq, g * c)
    return y, jnp.swapaxes(hf_re, 0, 1), jnp.swapaxes(hf_im, 0, 1)


def _s5_step_kernel(u_ref, h0_re_ref, h0_im_ref, a_re_ref, a_im_ref, bb_re_ref, bb_im_ref,
                    cc_re_ref, cc_im_ref, d_ref, y_ref, h_re_ref, h_im_ref):
    u = u_ref[...]
    ub = u.astype(BF16)
    a_re, a_im = a_re_ref[...], a_im_ref[...]
    h0r, h0i = h0_re_ref[...], h0_im_ref[...]
    hr = a_re * h0r - a_im * h0i + jnp.dot(ub, bb_re_ref[...], preferred_element_type=F32)
    hi = a_re * h0i + a_im * h0r + jnp.dot(ub, bb_im_ref[...], preferred_element_type=F32)
    h_re_ref[...] = hr
    h_im_ref[...] = hi
    y = (jnp.dot(hr.astype(BF16), cc_re_ref[...], preferred_element_type=F32)
         - jnp.dot(hi.astype(BF16), cc_im_ref[...], preferred_element_type=F32))
    y_ref[...] = y + u * d_ref[...]


def _block_diag(blocks):
    g, r, c = blocks.shape
    eye = jnp.eye(g, dtype=blocks.dtype)
    return (blocks[:, :, None, :] * eye[:, None, :, None]).reshape(g * r, g * c)


def _s5_step(u, h0_re, h0_im, ab_re, ab_im, bb_re, bb_im, c_re, c_im, d_skip):
    n = u.shape[0]
    gp = S5_GROUPS * S5_STATE
    bbd_re = _block_diag(bb_re).astype(BF16)
    bbd_im = _block_diag(bb_im).astype(BF16)
    ccd_re = _block_diag(jnp.swapaxes(c_re, 1, 2)).astype(BF16)
    ccd_im = _block_diag(jnp.swapaxes(c_im, 1, 2)).astype(BF16)
    y, h_re, h_im = pl.pallas_call(
        _s5_step_kernel,
        out_shape=[jax.ShapeDtypeStruct((n, S5_WIDTH), F32),
                   jax.ShapeDtypeStruct((n, gp), F32),
                   jax.ShapeDtypeStruct((n, gp), F32)],
        compiler_params=pltpu.CompilerParams(vmem_limit_bytes=VMEM_LIMIT),
        name="s5_step",
    )(u, h0_re.reshape(n, gp), h0_im.reshape(n, gp), ab_re.reshape(1, gp), ab_im.reshape(1, gp),
      bbd_re, bbd_im, ccd_re, ccd_im, d_skip.reshape(1, S5_WIDTH))
    return y, h_re.reshape(n, S5_GROUPS, S5_STATE), h_im.reshape(n, S5_GROUPS, S5_STATE)


def _rw_prep_kernel(p_ref, pp_ref, mu_ref, w0_ref, w1_ref, w2_ref, a0_ref, a1_ref, a2_ref,
                    g1_ref, g2_ref, kk_ref, ka_ref, rk_ref, hsum_ref,
                    r_out, lw_out, k_out, v_out, ah_out, b_out, g_out, bonus_out):
    w = RW_WIDTH
    p = p_ref[...]
    dp = pp_ref[...] - p
    r_in, k_in, v_in, z = p[:, :w], p[:, w:2 * w], p[:, 2 * w:3 * w], p[:, 3 * w:]
    dr, dk, dv, dz = dp[:, :w], dp[:, w:2 * w], dp[:, 2 * w:3 * w], dp[:, 3 * w:]
    mu = mu_ref[...]
    r = r_in + dr * mu[0:1]
    xw = z + dz * mu[1:2]
    k = k_in + dk * mu[2:3]
    v = v_in + dv * mu[3:4]
    xa = z + dz * mu[4:5]
    xg = z + dz * mu[5:6]
    t = w0_ref[...] + _bdot(jnp.tanh(_bdot(xw, w1_ref[...])), w2_ref[...])
    nt = -t
    softplus = jnp.maximum(nt, 0.0) + jnp.log(1.0 + jnp.exp(-jnp.abs(nt)))
    w_log = -softplus - 0.5
    lw_out[...] = -jnp.exp(w_log)
    a = _sigmoid(a0_ref[...] + _bdot(_bdot(xa, a1_ref[...]), a2_ref[...]))
    g_out[...] = _bdot(_sigmoid(_bdot(xg, g1_ref[...])), g2_ref[...])
    kk = k * kk_ref[...]
    ss = _dot2_exact_rhs(kk * kk, hsum_ref[...])
    kk = kk * lax.rsqrt(jnp.maximum(ss, 1e-24))
    k2 = k * (1.0 + (a - 1.0) * ka_ref[...])
    r_out[...] = r
    k_out[...] = k2
    v_out[...] = v
    ah_out[...] = -kk
    b_out[...] = kk * a
    bonus_out[...] = _dot2_exact_rhs(r * k2 * rk_ref[...], hsum_ref[...]) * v


def _head_sum_matrix(width):
    idx = np.arange(width) // HEAD_DIM
    return jnp.asarray((idx[:, None] == idx[None, :]).astype(np.float32), dtype=BF16)


def _rw_prep(p, p_prev, lp):
    t = p.shape[0]
    tm = _row_tile(t, 512)
    w = RW_WIDTH
    row = lambda a: a.reshape(1, w)
    full = lambda a: pl.BlockSpec(a.shape, lambda i: (0,) * a.ndim)
    params = [lp['rw_mu'], row(lp['rw_w0']), lp['rw_w1'].astype(BF16), lp['rw_w2'].astype(BF16),
              row(lp['rw_a0']), lp['rw_a1'].astype(BF16), lp['rw_a2'].astype(BF16),
              lp['rw_g1'].astype(BF16), lp['rw_g2'].astype(BF16), row(lp['rw_k_k']),
              row(lp['rw_k_a']), row(lp['rw_r_k']), _head_sum_matrix(w)]
    tile = lambda n: pl.BlockSpec((tm, n), lambda i: (i, 0))
    return pl.pallas_call(
        _rw_prep_kernel,
        grid=(t // tm,),
        in_specs=[tile(4 * w), tile(4 * w)] + [full(a) for a in params],
        out_specs=[tile(w)] * 8,
        out_shape=[jax.ShapeDtypeStruct((t, w), F32)] * 8,
        compiler_params=_cparams("parallel"),
        name="rw_prep",
    )(p, p_prev, *params)


def _rw_scan_kernel(r_ref, ah_ref, lw_ref, v_ref, bt_ref, kt_ref, lwt_ref, h0_ref,
                    y_ref, hf_ref, h_scr):
    c = RW_CHUNK
    ci = pl.program_id(1)

    @pl.when(ci == 0)
    def _():
        h_scr[...] = h0_ref[0]

    row = lax.broadcasted_iota(jnp.int32, (c, c), 0)
    col = lax.broadcasted_iota(jnp.int32, (c, c), 1)
    incl = row >= col
    strict = row > col
    tri_lo = jnp.where(incl, 1.0, 0.0).astype(BF16)
    tri_up = jnp.where(row <= col, 1.0, 0.0).astype(BF16)
    eye = jnp.where(row == col, 1.0, 0.0)

    for h in range(RW_HEADS):
        sl = slice(h * HEAD_DIM, (h + 1) * HEAD_DIM)
        r = r_ref[0, :, sl]
        ah = ah_ref[0, :, sl]
        lw = lw_ref[0, :, sl]
        v = v_ref[0, :, sl]
        bt = bt_ref[0, 0, sl, :]
        kt = kt_ref[0, 0, sl, :]
        lwt = lwt_ref[0, 0, sl, :]
        lw_h, lw_m = _split(lw)
        lw_l = (lw - lw_h.astype(F32) - lw_m.astype(F32)).astype(BF16)
        d = functools.partial(jnp.dot, preferred_element_type=F32)
        cum = d(tri_lo, lw_h) + (d(tri_lo, lw_m) + d(tri_lo, lw_l))
        lt_h, lt_m = _split(lwt)
        lt_l = (lwt - lt_h.astype(F32) - lt_m.astype(F32)).astype(BF16)
        cum_t = d(lt_h, tri_up) + (d(lt_m, tri_up) + d(lt_l, tri_up))
        a_t = ah * jnp.exp(cum - lw)
        r_t = r * jnp.exp(cum)
        w_inv_t = jnp.exp(-cum_t)
        b_t = (bt * w_inv_t).astype(BF16)
        k_t = (kt * w_inv_t).astype(BF16)
        cum_end = cum_t[:, c - 1:c]
        dec_t = jnp.exp(cum_end - cum_t)
        b_d = (bt * dec_t).astype(BF16)
        k_d = (kt * dec_t).astype(BF16)
        ar = jnp.concatenate([a_t, r_t], axis=0).astype(BF16)
        mb = d(ar, b_t)
        mk = d(ar, k_t)
        a_ab = jnp.where(strict, mb[:c], 0.0)
        a_ak = jnp.where(strict, mk[:c], 0.0)
        a_rb = jnp.where(incl, mb[c:], 0.0)
        a_rk = jnp.where(incl, mk[c:], 0.0)
        pw = a_ab
        inv = eye + a_ab
        for _ in range(int(math.log2(c)) - 1):
            pw = _bdot(pw, pw)
            inv = inv + _bdot(inv, pw)
        h0 = h_scr[h]
        vb = v.astype(BF16)
        ar_h = d(ar, h0.astype(BF16))
        u = _bdot(inv, ar_h[:c] + d(a_ak.astype(BF16), vb))
        ub = u.astype(BF16)
        y = ar_h[c:] + d(a_rb.astype(BF16), ub) + d(a_rk.astype(BF16), vb)
        y_ref[0, :, sl] = y
        h_scr[h] = jnp.exp(cum_end) * h0 + d(b_d, ub) + d(k_d, vb)

    @pl.when(ci == pl.num_programs(1) - 1)
    def _():
        hf_ref[0] = h_scr[...]


def _rw_scan(r, ah, lw, v, b, k2, s0, bsz, seq):
    w, c = RW_WIDTH, RW_CHUNK
    pad = (-seq) % c
    sp = seq + pad

    def rows(a):
        a = a.reshape(bsz, seq, w)
        return jnp.pad(a, ((0, 0), (0, pad), (0, 0))) if pad else a

    r3, ah3, lw3, v3, b3, k3 = (rows(a) for a in (r, ah, lw, v, b, k2))
    n_chunks = sp // c
    tr = lambda a: jnp.swapaxes(a.reshape(bsz, n_chunks, c, w), 2, 3)
    h0 = jnp.swapaxes(s0, 2, 3)
    rspec = pl.BlockSpec((1, c, w), lambda i, j: (i, j, 0))
    cspec = pl.BlockSpec((1, 1, w, c), lambda i, j: (i, j, 0, 0))
    sspec = pl.BlockSpec((1, RW_HEADS, HEAD_DIM, HEAD_DIM), lambda i, j: (i, 0, 0, 0))
    y, hf = pl.pallas_call(
        _rw_scan_kernel,
        grid=(bsz, n_chunks),
        in_specs=[rspec, rspec, rspec, rspec, cspec, cspec, cspec, sspec],
        out_specs=[rspec, sspec],
        out_shape=[jax.ShapeDtypeStruct((bsz, sp, w), F32),
                   jax.ShapeDtypeStruct((bsz, RW_HEADS, HEAD_DIM, HEAD_DIM), F32)],
        scratch_shapes=[pltpu.VMEM((RW_HEADS, HEAD_DIM, HEAD_DIM), F32)],
        compiler_params=_cparams("parallel", "arbitrary"),
        name="rw_scan",
    )(r3, ah3, lw3, v3, tr(b3), tr(k3), tr(lw3), h0)
    return y[:, :seq].reshape(bsz * seq, w), jnp.swapaxes(hf, 2, 3)


def _rope_kernel(qkv_ref, cos_ref, sin_lo_ref, sin_hi_ref, q_ref, k_ref, v_ref):
    w = ATT_WIDTH
    half = ROPE_DIM // 2
    x = qkv_ref[...]
    rep = lambda t: jnp.concatenate([t] * (w // 128), axis=1)
    cos, s_lo, s_hi = rep(cos_ref[...]), rep(sin_lo_ref[...]), rep(sin_hi_ref[...])

    def rot(t):
        up = pltpu.roll(t, w - half, axis=1)
        dn = pltpu.roll(t, half, axis=1)
        return t * cos + up * s_lo + dn * s_hi

    q_ref[...] = rot(x[:, :w]) * ATT_SCALE
    k_ref[...] = rot(x[:, w:2 * w])
    v_ref[...] = x[:, 2 * w:]


def _rope_tables(pos):
    half = ROPE_DIM // 2
    inv_freq = jnp.exp(-math.log(ROPE_THETA) * jnp.arange(half, dtype=jnp.float32) * (2.0 / ROPE_DIM))
    ang = pos.astype(jnp.float32)[:, None] * inv_freq[None, :]
    cos, sin = jnp.cos(ang), jnp.sin(ang)
    n = pos.shape[0]
    one = jnp.ones((n, HEAD_DIM - ROPE_DIM), F32)
    zero = jnp.zeros((n, HEAD_DIM - ROPE_DIM), F32)
    z8 = jnp.zeros((n, half), F32)
    cos_h = jnp.concatenate([cos, cos, one], axis=1)
    lo_h = jnp.concatenate([-sin, z8, zero], axis=1)
    hi_h = jnp.concatenate([z8, sin, zero], axis=1)
    two = lambda t: jnp.concatenate([t, t], axis=1)
    return two(cos_h), two(lo_h), two(hi_h)


def _rope(qkv, tables):
    t = qkv.shape[0]
    period = tables[0].shape[0]
    tm = _row_tile(period, 512)
    nb = period // tm
    w = ATT_WIDTH
    tspec = pl.BlockSpec((tm, 128), lambda i: (i % nb, 0))
    ospec = pl.BlockSpec((tm, w), lambda i: (i, 0))
    return pl.pallas_call(
        _rope_kernel,
        grid=(t // tm,),
        in_specs=[pl.BlockSpec((tm, 3 * w), lambda i: (i, 0)), tspec, tspec, tspec],
        out_specs=[ospec] * 3,
        out_shape=[jax.ShapeDtypeStruct((t, w), F32)] * 3,
        compiler_params=_cparams("parallel"),
        name="rope",
    )(qkv, *tables)


def _multiplicity(dist):
    m = np.zeros(dist.shape, np.float64)
    for window, dil in DILATED_PATTERNS:
        m += ((dist >= 0) & (dist <= window) & (dist % dil == 0))
    return m


def _dist_bias(dist):
    m = _multiplicity(dist)
    return np.where(m > 0, np.log(np.maximum(m, 1.0)), NEG_INF).astype(np.float32)


def _attn_kernel(q_ref, k_ref, v_ref, bias_ref, o_ref, m_scr, l_scr, acc_scr):
    qi = pl.program_id(1)
    j = pl.program_id(2)
    nj = pl.num_programs(2)

    @pl.when(j == 0)
    def _():
        m_scr[...] = jnp.full(m_scr.shape, NEG_INF, F32)
        l_scr[...] = jnp.zeros(l_scr.shape, F32)
        acc_scr[...] = jnp.zeros(acc_scr.shape, F32)

    @pl.when(qi - j >= 0)
    def _():
        q = q_ref[0].astype(BF16)
        k = k_ref[0].astype(BF16)
        s = lax.dot_general(q, k, (((1,), (1,)), ((), ())), preferred_element_type=F32)
        s = s + bias_ref[j]
        m_old = m_scr[...]
        m_new = jnp.maximum(m_old, jnp.max(s, axis=1, keepdims=True))
        alpha = jnp.exp(m_old - m_new)
        p = jnp.exp(s - m_new)
        l_scr[...] = alpha * l_scr[...] + jnp.sum(p, axis=1, keepdims=True)
        acc_scr[...] = alpha * acc_scr[...] + jnp.dot(p.astype(BF16), v_ref[0].astype(BF16),
                                                      preferred_element_type=F32)
        m_scr[...] = m_new

    @pl.when(j == nj - 1)
    def _():
        o_ref[0] = acc_scr[...] / l_scr[...]


def _attn_prompt(q, k, v, bsz, seq):
    blk = ATT_BLOCK
    nq = seq // blk
    nback = min(WIN_MAX // blk, nq - 1)
    nj = nback + 1
    heads = lambda a: a.reshape(bsz, seq, ATT_HEADS, HEAD_DIM).transpose(0, 2, 1, 3).reshape(
        bsz * ATT_HEADS, seq, HEAD_DIM)
    i_idx = np.arange(blk)
    bias = np.stack([_dist_bias(dlt * blk + i_idx[:, None] - i_idx[None, :]) for dlt in range(nj)])
    qspec = pl.BlockSpec((1, blk, HEAD_DIM), lambda b, i, j: (b, i, 0))
    kspec = pl.BlockSpec((1, blk, HEAD_DIM), lambda b, i, j: (b, jnp.maximum(i - j, 0), 0))
    o = pl.pallas_call(
        _attn_kernel,
        grid=(bsz * ATT_HEADS, nq, nj),
        in_specs=[qspec, kspec, kspec, pl.BlockSpec((nj, blk, blk), lambda b, i, j: (0, 0, 0))],
        out_specs=qspec,
        out_shape=jax.ShapeDtypeStruct((bsz * ATT_HEADS, seq, HEAD_DIM), F32),
        scratch_shapes=[pltpu.VMEM((blk, 1), F32), pltpu.VMEM((blk, 1), F32),
                        pltpu.VMEM((blk, HEAD_DIM), F32)],
        compiler_params=_cparams("parallel", "parallel", "arbitrary"),
        name="attn_prompt",
    )(heads(q), heads(k), heads(v), jnp.asarray(bias))
    return o.reshape(bsz, ATT_HEADS, seq, HEAD_DIM).transpose(0, 2, 1, 3).reshape(bsz * seq, ATT_WIDTH)


def _attn_step_kernel(q_ref, kn_ref, vn_ref, kc_ref, vc_ref, bias_ref, sel_ref, selt_ref, o_ref):
    q = q_ref[0]
    kn, vn = kn_ref[0], vn_ref[0]
    kc, vc = kc_ref[0], vc_ref[0]
    sel, selt = sel_ref[...], selt_ref[...]
    s_c = _dot2_exact_rhs(kc * q, sel) + bias_ref[...]
    s_n = _dot2_exact_rhs(kn * q, sel) + math.log(len(DILATED_PATTERNS))
    m = jnp.maximum(jnp.max(s_c, axis=0, keepdims=True), s_n)
    p_c = jnp.exp(s_c - m)
    p_n = jnp.exp(s_n - m)
    den = jnp.sum(p_c, axis=0, keepdims=True) + p_n
    pe_c = _dot2_exact_rhs(p_c, selt)
    pe_n = _dot2_exact_rhs(p_n, selt)
    den_e = _dot2_exact_rhs(den, selt)
    o_ref[0] = (jnp.sum(pe_c * vc, axis=0, keepdims=True) + pe_n * vn) / den_e


def _attn_step(q, k_new, v_new, k_buf, v_buf):
    n, n_buf = k_buf.shape[0], k_buf.shape[1]
    w = ATT_WIDTH
    dist = n_buf - np.arange(n_buf)
    bias = np.broadcast_to(_dist_bias(dist)[:, None], (n_buf, 128)).copy()
    head = np.arange(w) // HEAD_DIM
    sel = (head[:, None] == np.arange(128)[None, :]).astype(np.float32)
    vec = pl.BlockSpec((1, 1, w), lambda i: (i, 0, 0))
    buf = pl.BlockSpec((1, n_buf, w), lambda i: (i, 0, 0))
    full = lambda s: pl.BlockSpec(s, lambda i: (0, 0))
    o = pl.pallas_call(
        _attn_step_kernel,
        grid=(n,),
        in_specs=[vec, vec, vec, buf, buf, full((n_buf, 128)), full((w, 128)), full((128, w))],
        out_specs=vec,
        out_shape=jax.ShapeDtypeStruct((n, 1, w), F32),
        compiler_params=_cparams("parallel"),
        name="attn_step",
    )(q.reshape(n, 1, w), k_new.reshape(n, 1, w), v_new.reshape(n, 1, w),
      k_buf.reshape(n, n_buf, w), v_buf.reshape(n, n_buf, w), jnp.asarray(bias),
      jnp.asarray(sel, dtype=BF16), jnp.asarray(sel.T, dtype=BF16))
    return o.reshape(n, w)


def _gelu_tanh(x):
    return 0.5 * x * (1.0 + jnp.tanh(math.sqrt(2.0 / math.pi) * (x + 0.044715 * (x * x * x))))


def _mix_kernel(x_ref, ys_ref, yr_ref, bonus_ref, g_ref, ya_ref, wglu_ref, bglu_ref, gng_ref, gnb_ref,
                havg_ref, wout_ref, lng_ref, lnb_ref, o_ref, *, alpha):
    ys = _gelu_tanh(ys_ref[...])
    ya = ys * _sigmoid(_bdot(ys, wglu_ref[...]) + bglu_ref[...])
    yr = yr_ref[...]
    havg = havg_ref[...]
    mean = _dot2_exact_rhs(yr, havg) * (1.0 / HEAD_DIM)
    yc = yr - mean
    var = _dot2_exact_rhs(yc * yc, havg) * (1.0 / HEAD_DIM)
    yb = (yc * lax.rsqrt(var + RW_GN_EPS) * gng_ref[...] + gnb_ref[...] + bonus_ref[...]) * g_ref[...]
    wout = wout_ref[...]
    o1, o2 = S5_WIDTH, S5_WIDTH + RW_WIDTH
    mix = _bdot(ya, wout[:o1]) + _bdot(yb, wout[o1:o2]) + _bdot(ya_ref[...], wout[o2:])
    o_ref[...] = _layer_norm(alpha * x_ref[...] + mix, lng_ref[...], lnb_ref[...])


def _mix(x, y_s5, y_rw, bonus, gate, y_att, lp, alpha):
    t = x.shape[0]
    tm = _row_tile(t, 512)
    tile = lambda n: pl.BlockSpec((tm, n), lambda i: (i, 0))
    full = lambda a: pl.BlockSpec(a.shape, lambda i: (0,) * a.ndim)
    params = [lp['s5_w_glu'].astype(BF16), lp['s5_b_glu'].reshape(1, -1), lp['rw_gn_g'].reshape(1, -1),
              lp['rw_gn_b'].reshape(1, -1), _head_sum_matrix(RW_WIDTH), lp['w_out'].astype(BF16),
              lp['ln_g'][0].reshape(1, -1), lp['ln_b'][0].reshape(1, -1)]
    return pl.pallas_call(
        functools.partial(_mix_kernel, alpha=alpha),
        grid=(t // tm,),
        in_specs=[tile(D_MODEL), tile(S5_WIDTH), tile(RW_WIDTH), tile(RW_WIDTH), tile(RW_WIDTH),
                  tile(ATT_WIDTH)] + [full(a) for a in params],
        out_specs=tile(D_MODEL),
        out_shape=jax.ShapeDtypeStruct((t, D_MODEL), F32),
        compiler_params=_cparams("parallel"),
        name="mix",
    )(x, y_s5, y_rw, bonus, gate, y_att, *params)


def _rank_select(vals, n_rows, keep):
    ridx = lax.broadcasted_iota(jnp.int32, vals.shape, 0)
    cnt = jnp.zeros(vals.shape, jnp.int32)
    for j in range(n_rows):
        vj = vals[j:j + 1, :]
        beats = jnp.where(vj > vals, 1, jnp.where(vj == vals, jnp.where(ridx > j, 1, 0), 0))
        cnt = cnt + beats
    return cnt < keep


def _router_kernel(x_ref, wt_ref, bias_ref, gates_ref, gscore_scr, ekeep_scr):
    e, ng = N_EXPERTS, N_EXPERT_GROUPS
    per = e // ng
    logits = lax.dot_general(wt_ref[...].astype(BF16), x_ref[...].astype(BF16), (((1,), (1,)), ((), ())),
                             preferred_element_type=F32)
    scores = _sigmoid(logits)
    sel = scores + bias_ref[:, 0:1]
    t = sel.shape[1]
    pos = lax.broadcasted_iota(jnp.int32, (per, t), 0)
    for gi in range(ng):
        grp = sel[gi * per:(gi + 1) * per]
        m1 = jnp.max(grp, axis=0, keepdims=True)
        first = jnp.min(jnp.where(grp == m1, pos, per), axis=0, keepdims=True)
        m2 = jnp.max(jnp.where(pos == first, NEG_INF, grp), axis=0, keepdims=True)
        gscore_scr[gi:gi + 1, :] = m1 + m2
    gkeep = jnp.where(_rank_select(gscore_scr[...], ng, TOPK_GROUPS), 1.0, 0.0)
    for gi in range(ng):
        ekeep_scr[gi * per:(gi + 1) * per, :] = jnp.broadcast_to(gkeep[gi:gi + 1], (per, t))
    masked = jnp.where(ekeep_scr[...] > 0.0, sel, NEG_INF)
    chosen = _rank_select(masked, e, TOP_K)
    w = jnp.where(chosen, scores, 0.0)
    gates_ref[...] = w / jnp.sum(w, axis=0, keepdims=True) * ROUTED_SCALE


def _router(x, w_router, router_bias):
    t = x.shape[0]
    tm = _row_tile(t, 512)
    e = N_EXPERTS
    gates_t = pl.pallas_call(
        _router_kernel,
        grid=(t // tm,),
        in_specs=[pl.BlockSpec((tm, D_MODEL), lambda i: (i, 0)),
                  pl.BlockSpec((e, D_MODEL), lambda i: (0, 0)),
                  pl.BlockSpec((e, 128), lambda i: (0, 0))],
        out_specs=pl.BlockSpec((e, tm), lambda i: (0, i)),
        out_shape=jax.ShapeDtypeStruct((e, t), F32),
        scratch_shapes=[pltpu.VMEM((N_EXPERT_GROUPS, tm), F32), pltpu.VMEM((e, tm), F32)],
        compiler_params=_cparams("parallel"),
        name="router",
    )(x, w_router.T, jnp.broadcast_to(router_bias.reshape(e, 1), (e, 128)))
    return gates_t.T


def _moe_kernel(x_ref, gates_ref, wg_ref, wu_ref, wd_ref, sg_ref, su_ref, sd_ref, lng_ref, lnb_ref,
                o_ref, xb_scr, acc_scr, *, alpha):
    e = pl.program_id(1)
    ne = pl.num_programs(1)

    @pl.when(e == 0)
    def _():
        xb = x_ref[...].astype(BF16)
        xb_scr[...] = xb
        h = _silu(jnp.dot(xb, sg_ref[...].astype(BF16), preferred_element_type=F32)) * jnp.dot(
            xb, su_ref[...].astype(BF16), preferred_element_type=F32)
        acc_scr[...] = jnp.dot(h.astype(BF16), sd_ref[...].astype(BF16), preferred_element_type=F32)

    xb = xb_scr[...]
    h = _silu(jnp.dot(xb, wg_ref[0].astype(BF16), preferred_element_type=F32)) * jnp.dot(
        xb, wu_ref[0].astype(BF16), preferred_element_type=F32)
    onehot = jnp.where(lax.broadcasted_iota(jnp.int32, (N_EXPERTS, EXPERT_FF), 0) == e, 1.0, 0.0).astype(BF16)
    gate = _dot2_exact_rhs(gates_ref[...], onehot)
    gate = jnp.concatenate([gate] * (D_MODEL // EXPERT_FF), axis=1)
    acc_scr[...] += jnp.dot(h.astype(BF16), wd_ref[0].astype(BF16), preferred_element_type=F32) * gate

    @pl.when(e == ne - 1)
    def _():
        o_ref[...] = _layer_norm(alpha * x_ref[...] + acc_scr[...], lng_ref[...], lnb_ref[...])


def _moe(x, gates, lp, alpha):
    t = x.shape[0]
    tm = _row_tile(t, 1024)
    d, f, e = D_MODEL, EXPERT_FF, N_EXPERTS
    full = lambda a: pl.BlockSpec(a.shape, lambda i, j: (0,) * a.ndim)
    lng, lnb = lp['ln_g'][1].reshape(1, -1), lp['ln_b'][1].reshape(1, -1)
    return pl.pallas_call(
        functools.partial(_moe_kernel, alpha=alpha),
        grid=(t // tm, e),
        in_specs=[pl.BlockSpec((tm, d), lambda i, j: (i, 0)),
                  pl.BlockSpec((tm, e), lambda i, j: (i, 0)),
                  pl.BlockSpec((1, d, f), lambda i, j: (j, 0, 0)),
                  pl.BlockSpec((1, d, f), lambda i, j: (j, 0, 0)),
                  pl.BlockSpec((1, f, d), lambda i, j: (j, 0, 0)),
                  full(lp['shared_w_gate']), full(lp['shared_w_up']), full(lp['shared_w_down']),
                  full(lng), full(lnb)],
        out_specs=pl.BlockSpec((tm, d), lambda i, j: (i, 0)),
        out_shape=jax.ShapeDtypeStruct((t, d), F32),
        scratch_shapes=[pltpu.VMEM((tm, d), BF16), pltpu.VMEM((tm, d), F32)],
        compiler_params=_cparams("parallel", "arbitrary"),
        name="moe",
    )(x, gates, lp['expert_w_gate'], lp['expert_w_up'], lp['expert_w_down'],
      lp['shared_w_gate'], lp['shared_w_up'], lp['shared_w_down'], lng, lnb)


def _trunk_layer(x, bsz, seq, pos0, h0_re, h0_im, rw_s0, rw_prev, k_buf, v_buf, lp, s5p, rope_tables, alpha):
    t = bsz * seq
    u, p_rw, qkv = _proj(x, lp['w_in_bf16'])
    (kmat, bout_re, bout_im, win_re, win_im, aq_re, aq_im, ab_re, ab_im, bb_re, bb_im) = s5p
    if seq == 1:
        y_s5, h_re, h_im = _s5_step(u, h0_re, h0_im, ab_re, ab_im, bb_re, bb_im,
                                    lp['s5_c_re'], lp['s5_c_im'], lp['s5_d'])
    else:
        y_s5, h_re, h_im = _s5_scan(u, (kmat, bout_re, bout_im, win_re, win_im, aq_re, aq_im),
                                    lp['s5_d'], h0_re, h0_im, bsz, seq)
    p3 = p_rw.reshape(bsz, seq, 4 * RW_WIDTH)
    p_prev = jnp.concatenate([rw_prev[:, None, :], p3[:, :-1]], axis=1).reshape(t, 4 * RW_WIDTH)
    r, lw, k2, v, ah, b, gate, bonus = _rw_prep(p_rw, p_prev, lp)
    y_rw, rw_s = _rw_scan(r, ah, lw, v, b, k2, rw_s0, bsz, seq)
    rw_row = p3[:, -1]
    q, k, vv = _rope(qkv, rope_tables)
    k4 = k.reshape(bsz, seq, ATT_HEADS, HEAD_DIM)
    v4 = vv.reshape(bsz, seq, ATT_HEADS, HEAD_DIM)
    if seq == 1:
        y_att = _attn_step(q, k, vv, k_buf, v_buf)
    else:
        assert k_buf.shape[1] == 0
        y_att = _attn_prompt(q, k, vv, bsz, seq)
    keep = min(WIN_MAX, max(k_buf.shape[1], seq))
    kb = jnp.concatenate([k_buf, k4], axis=1)[:, -keep:]
    vb = jnp.concatenate([v_buf, v4], axis=1)[:, -keep:]
    x1 = _mix(x, y_s5, y_rw, bonus, gate, y_att, lp, alpha)
    gates = _router(x1, lp['w_router'], lp['router_bias'])
    x2 = _moe(x1, gates, lp, alpha)
    return x2, (h_re, h_im, rw_s, rw_row, kb, vb)


def kernel(x_prompt, x_sample, state_s5_re, state_s5_im, state_rwkv, state_rwkv_shift, cache_attn_k, cache_attn_v, w_in, s5_lambda_re, s5_lambda_im, s5_b_re, s5_b_im, s5_c_re, s5_c_im, s5_d, s5_log_step, s5_w_glu, s5_b_glu, rw_mu, rw_w0, rw_w1, rw_w2, rw_a0, rw_a1, rw_a2, rw_g1, rw_g2, rw_k_k, rw_k_a, rw_r_k, rw_gn_g, rw_gn_b, w_out, ln_g, ln_b, w_router, router_bias, expert_w_gate, expert_w_up, expert_w_down, shared_w_gate, shared_w_up, shared_w_down):
    depth = w_in.shape[0]
    alpha = (2 * depth) ** 0.25
    bsz, seq, d = x_prompt.shape
    dbsz, dseq, _ = x_sample.shape
    past = cache_attn_k.shape[2]
    names = dict(s5_c_re=s5_c_re, s5_c_im=s5_c_im, s5_d=s5_d, s5_w_glu=s5_w_glu, s5_b_glu=s5_b_glu,
                 rw_mu=rw_mu, rw_w0=rw_w0, rw_w1=rw_w1, rw_w2=rw_w2, rw_a0=rw_a0, rw_a1=rw_a1, rw_a2=rw_a2,
                 rw_g1=rw_g1, rw_g2=rw_g2, rw_k_k=rw_k_k, rw_k_a=rw_k_a, rw_r_k=rw_r_k, rw_gn_g=rw_gn_g,
                 rw_gn_b=rw_gn_b, w_out=w_out, ln_g=ln_g, ln_b=ln_b, w_router=w_router,
                 router_bias=router_bias, expert_w_gate=expert_w_gate, expert_w_up=expert_w_up,
                 expert_w_down=expert_w_down, shared_w_gate=shared_w_gate, shared_w_up=shared_w_up,
                 shared_w_down=shared_w_down)
    assert dseq == 1
    tables_p = _rope_tables(jnp.arange(seq))
    tables_s = _rope_tables(jnp.full((dbsz,), PAST_LEN))
    yp = x_prompt.reshape(bsz * seq, d)
    ys = x_sample.reshape(dbsz * dseq, d)
    new_p, new_s = [], []
    for l in range(depth):
        lp = {k: v[l] for k, v in names.items()}
        lp['w_in_bf16'] = w_in[l].astype(BF16)
        s5p = _s5_prep(s5_lambda_re[l], s5_lambda_im[l], s5_log_step[l], s5_b_re[l], s5_b_im[l],
                       s5_c_re[l], s5_c_im[l])
        zs = jnp.zeros((bsz, S5_GROUPS, S5_STATE), F32)
        z_s = jnp.zeros((bsz, RW_HEADS, HEAD_DIM, HEAD_DIM), F32)
        zrow = jnp.zeros((bsz, 4 * RW_WIDTH), F32)
        zbuf = jnp.zeros((bsz, 0, ATT_HEADS, HEAD_DIM), F32)
        yp, st_p = _trunk_layer(yp, bsz, seq, 0, zs, zs, z_s, zrow, zbuf, zbuf, lp, s5p, tables_p, alpha)
        ys, st_s = _trunk_layer(ys, dbsz, dseq, past, state_s5_re[l], state_s5_im[l], state_rwkv[l],
                                state_rwkv_shift[l], cache_attn_k[l], cache_attn_v[l], lp, s5p, tables_s, alpha)
        new_p.append(st_p)
        new_s.append(st_s)
    stack = lambda sts: [jnp.stack([s[i] for s in sts], 0) for i in range(6)]
    return (yp.reshape(bsz, seq, d), ys.reshape(dbsz, dseq, d), *stack(new_p), *stack(new_s))
```

```python
import functools
import math

import numpy as np
import jax
import jax.numpy as jnp
from jax import lax
from jax.experimental import pallas as pl
from jax.experimental.pallas import tpu as pltpu

F32 = jnp.float32
BF16 = jnp.bfloat16

D_MODEL = 1024
PAST_LEN = 8192
HEAD_DIM = 64
S5_WIDTH = 256
S5_GROUP = 16
S5_GROUPS = 16
S5_STATE = 64
RW_WIDTH = 384
RW_HEADS = 6
ATT_WIDTH = 384
ATT_HEADS = 6
N_IN = S5_WIDTH + 4 * RW_WIDTH + 3 * ATT_WIDTH
RW_GN_EPS = 64e-5
DILATED_PATTERNS = ((128, 1), (512, 4), (2048, 16))
WIN_MAX = 2048
ATT_SCALE = HEAD_DIM ** -0.5
ROPE_THETA = 500000.0
ROPE_DIM = HEAD_DIM // 4
NEG_INF = -1e30
N_EXPERTS = 64
N_EXPERT_GROUPS = 8
TOPK_GROUPS = 4
TOP_K = 8
EXPERT_FF = 256
ROUTED_SCALE = 2.5
LN_EPS = 1e-5

S5_CHUNK = 16
RW_CHUNK = 64
RW_SEQS_PER_STEP = 2
ATT_TILE = 2048
ATT_SUB = 128
VMEM_LIMIT = 56 * 1024 * 1024


def _cparams(*sem):
    return pltpu.CompilerParams(dimension_semantics=sem, vmem_limit_bytes=VMEM_LIMIT)


def _bdot(a, b):
    return jnp.dot(a.astype(BF16), b.astype(BF16), preferred_element_type=F32)


def _split(x):
    hi = x.astype(BF16)
    lo = (x - hi.astype(F32)).astype(BF16)
    return hi, lo


def _dot3(a, b):
    ah, al = _split(a)
    bh, bl = _split(b)
    d = functools.partial(jnp.dot, preferred_element_type=F32)
    return d(ah, bh) + (d(ah, bl) + d(al, bh))


def _dot2_exact_rhs(a, b_bf16):
    ah, al = _split(a)
    d = functools.partial(jnp.dot, preferred_element_type=F32)
    return d(ah, b_bf16) + d(al, b_bf16)


def _sigmoid(x):
    return 1.0 / (1.0 + jnp.exp(-x))


def _silu(x):
    return x * _sigmoid(x)


def _layer_norm(x, g, b):
    mu = jnp.mean(x, axis=-1, keepdims=True)
    xc = x - mu
    var = jnp.mean(xc * xc, axis=-1, keepdims=True)
    return xc * lax.rsqrt(var + LN_EPS) * g + b


def _row_tile(t, pref):
    tm = min(t, pref)
    assert t % tm == 0
    return tm


def _proj_kernel(x_ref, w_ref, u_ref, rw_ref, qkv_ref):
    p = jnp.dot(x_ref[...].astype(BF16), w_ref[...], preferred_element_type=F32)
    o1 = S5_WIDTH
    o2 = o1 + 4 * RW_WIDTH
    u_ref[...] = p[:, :o1]
    rw_ref[...] = p[:, o1:o2]
    qkv_ref[...] = p[:, o2:]


def _proj(x, w_bf16):
    t = x.shape[0]
    tm = _row_tile(t, 512)
    return pl.pallas_call(
        _proj_kernel,
        grid=(t // tm,),
        in_specs=[pl.BlockSpec((tm, D_MODEL), lambda i: (i, 0)),
                  pl.BlockSpec((D_MODEL, N_IN), lambda i: (0, 0))],
        out_specs=[pl.BlockSpec((tm, S5_WIDTH), lambda i: (i, 0)),
                   pl.BlockSpec((tm, 4 * RW_WIDTH), lambda i: (i, 0)),
                   pl.BlockSpec((tm, 3 * ATT_WIDTH), lambda i: (i, 0))],
        out_shape=[jax.ShapeDtypeStruct((t, S5_WIDTH), F32),
                   jax.ShapeDtypeStruct((t, 4 * RW_WIDTH), F32),
                   jax.ShapeDtypeStruct((t, 3 * ATT_WIDTH), F32)],
        compiler_params=_cparams("parallel"),
        name="proj",
    )(x, w_bf16)


def _s5_discretize(lr, li, log_step):
    dt = jnp.exp(log_step)
    mag = jnp.exp(lr * dt)
    ab_re = mag * jnp.cos(li * dt)
    ab_im = mag * jnp.sin(li * dt)
    den = lr * lr + li * li
    cf_re = ((ab_re - 1.0) * lr + ab_im * li) / den
    cf_im = (ab_im * lr - (ab_re - 1.0) * li) / den
    return dt, ab_re, ab_im, cf_re, cf_im


def _s5_power(lr, li, dt, n):
    mag = jnp.exp(n * (lr * dt))
    ang = n * (li * dt)
    return mag * jnp.cos(ang), mag * jnp.sin(ang)


def _s5_prep_kernel(lr_ref, li_ref, ls_ref, lrc_ref, lic_ref, lsc_ref, bt_re_ref, bt_im_ref,
                    ct_re_ref, ct_im_ref,
                    kmat_ref, bout_re_ref, bout_im_ref, win_re_ref, win_im_ref,
                    aq_re_ref, aq_im_ref, ab_re_ref, ab_im_ref, bb_re_ref, bb_im_ref):
    q = S5_CHUNK
    n_rows = q * S5_GROUP
    lr, li, ls = lr_ref[0], li_ref[0], ls_ref[0]
    dt, ab_re, ab_im, cf_re, cf_im = _s5_discretize(lr, li, ls)
    ab_re_ref[0] = ab_re
    ab_im_ref[0] = ab_im
    aq_re, aq_im = _s5_power(lr, li, dt, jnp.float32(q))
    aq_re_ref[0] = aq_re
    aq_im_ref[0] = aq_im
    bt_re, bt_im = bt_re_ref[0], bt_im_ref[0]
    bb_re = cf_re * bt_re - cf_im * bt_im
    bb_im = cf_re * bt_im + cf_im * bt_re
    bb_re_ref[0] = bb_re[:S5_GROUP]
    bb_im_ref[0] = bb_im[:S5_GROUP]
    step = lax.broadcasted_iota(jnp.int32, (n_rows, S5_STATE), 0) // S5_GROUP
    pw_re, pw_im = _s5_power(lr, li, dt, (q - 1 - step).astype(F32))
    bout_re_ref[0] = (bb_re * pw_re - bb_im * pw_im).astype(BF16)
    bout_im_ref[0] = (bb_re * pw_im + bb_im * pw_re).astype(BF16)
    lrc, lic, lsc = lrc_ref[0], lic_ref[0], lsc_ref[0]
    dtc, abc_re, abc_im, _, _ = _s5_discretize(lrc, lic, lsc)
    lag = (lax.broadcasted_iota(jnp.int32, (S5_STATE, n_rows), 1) // S5_GROUP).astype(F32)
    p0_re, p0_im = _s5_power(lrc, lic, dtc, lag)
    ct_re, ct_im = ct_re_ref[0], ct_im_ref[0]
    e0_re = ct_re * p0_re - ct_im * p0_im
    e0_im = ct_re * p0_im + ct_im * p0_re
    e1_re = e0_re * abc_re - e0_im * abc_im
    e1_im = e0_re * abc_im + e0_im * abc_re
    win_re_ref[0] = e1_re.astype(BF16)
    win_im_ref[0] = (-e1_im).astype(BF16)
    g = _dot3(bb_re[:S5_GROUP], e0_re) - _dot3(bb_im[:S5_GROUP], e0_im)
    lane = lax.broadcasted_iota(jnp.int32, (S5_GROUP, n_rows), 1)
    blocks = []
    for i in range(q):
        if i == 0:
            blocks.append(g)
        else:
            shifted = pltpu.roll(g, S5_GROUP * i, axis=1)
            blocks.append(jnp.where(lane >= S5_GROUP * i, shifted, 0.0))
    kmat_ref[0] = jnp.concatenate(blocks, axis=0).astype(BF16)


def _s5_prep(lam_re, lam_im, log_step, b_re, b_im, c_re, c_im):
    g, p, c, q = S5_GROUPS, S5_STATE, S5_GROUP, S5_CHUNK
    n = q * c
    row = lambda a: a.reshape(g, 1, p)
    col = lambda a: jnp.broadcast_to(a.reshape(g, p, 1), (g, p, n))
    ls_row = jnp.broadcast_to(log_step.reshape(g, 1, 1), (g, 1, p))
    ls_col = jnp.broadcast_to(log_step.reshape(g, 1, 1), (g, p, n))
    bt = lambda a: jnp.tile(jnp.swapaxes(a, 1, 2), (1, q, 1))
    ct = lambda a: jnp.tile(jnp.swapaxes(a, 1, 2), (1, 1, q))
    spec = lambda *s: pl.BlockSpec((1,) + s, lambda i: (i, 0, 0))
    sds = lambda s, d: jax.ShapeDtypeStruct((g,) + s, d)
    return pl.pallas_call(
        _s5_prep_kernel,
        grid=(g,),
        in_specs=[spec(1, p), spec(1, p), spec(1, p), spec(p, n), spec(p, n), spec(p, n),
                  spec(n, p), spec(n, p), spec(p, n), spec(p, n)],
        out_specs=[spec(n, n), spec(n, p), spec(n, p), spec(p, n), spec(p, n),
                   spec(1, p), spec(1, p), spec(1, p), spec(1, p), spec(c, p), spec(c, p)],
        out_shape=[sds((n, n), BF16), sds((n, p), BF16), sds((n, p), BF16), sds((p, n), BF16),
                   sds((p, n), BF16), sds((1, p), F32), sds((1, p), F32), sds((1, p), F32),
                   sds((1, p), F32), sds((c, p), F32), sds((c, p), F32)],
        compiler_params=_cparams("parallel"),
        name="s5_prep",
    )(row(lam_re), row(lam_im), ls_row, col(lam_re), col(lam_im), ls_col,
      bt(b_re), bt(b_im), ct(c_re), ct(c_im))


def _s5_scan_kernel(x_ref, kmat_ref, bout_re_ref, bout_im_ref, win_re_ref, win_im_ref,
                    aq_re_ref, aq_im_ref, d_ref, h0_re_ref, h0_im_ref,
                    y_ref, hf_re_ref, hf_im_ref,
                    s_re, s_im, hin_re, hin_im, *, bsz, n_chunks):
    x = x_ref[0]
    xb = x.astype(BF16)
    s_re[...] = jnp.dot(xb, bout_re_ref[0], preferred_element_type=F32)
    s_im[...] = jnp.dot(xb, bout_im_ref[0], preferred_element_type=F32)
    ar, ai = aq_re_ref[0], aq_im_ref[0]

    def body(k, carry):
        new = []
        for b in range(bsz):
            hr, hi = carry[2 * b], carry[2 * b + 1]
            r = b * n_chunks + k
            hin_re[pl.ds(r, 1), :] = hr
            hin_im[pl.ds(r, 1), :] = hi
            sr = s_re[pl.ds(r, 1), :]
            si = s_im[pl.ds(r, 1), :]
            new.append(ar * hr - ai * hi + sr)
            new.append(ar * hi + ai * hr + si)
        return tuple(new)

    init = []
    for b in range(bsz):
        init.append(h0_re_ref[0, b:b + 1, :])
        init.append(h0_im_ref[0, b:b + 1, :])
    fin = lax.fori_loop(0, n_chunks, body, tuple(init))
    for b in range(bsz):
        hf_re_ref[0, b:b + 1, :] = fin[2 * b]
        hf_im_ref[0, b:b + 1, :] = fin[2 * b + 1]
    y = jnp.dot(xb, kmat_ref[0], preferred_element_type=F32)
    y = y + jnp.dot(hin_re[...].astype(BF16), win_re_ref[0], preferred_element_type=F32)
    y = y + jnp.dot(hin_im[...].astype(BF16), win_im_ref[0], preferred_element_type=F32)
    y_ref[0] = y + x * d_ref[0]


def _s5_scan(u, mats, d_skip, h0_re, h0_im, bsz, seq):
    g, p, c, q = S5_GROUPS, S5_STATE, S5_GROUP, S5_CHUNK
    n = q * c
    kmat, bout_re, bout_im, win_re, win_im, aq_re, aq_im = mats
    n_chunks = seq // q
    rows = bsz * n_chunks
    x = u.reshape(bsz, n_chunks, q, g, c).transpose(3, 0, 1, 2, 4).reshape(g, rows, n)
    d_t = jnp.tile(d_skip.reshape(g, 1, c), (1, 1, q))
    h0r = jnp.swapaxes(h0_re, 0, 1)
    h0i = jnp.swapaxes(h0_im, 0, 1)
    spec = lambda *s: pl.BlockSpec((1,) + s, lambda i: (i, 0, 0))
    y, hf_re, hf_im = pl.pallas_call(
        functools.partial(_s5_scan_kernel, bsz=bsz, n_chunks=n_chunks),
        grid=(g,),
        in_specs=[spec(rows, n), spec(n, n), spec(n, p), spec(n, p), spec(p, n), spec(p, n),
                  spec(1, p), spec(1, p), spec(1, n), spec(bsz, p), spec(bsz, p)],
        out_specs=[spec(rows, n), spec(bsz, p), spec(bsz, p)],
        out_shape=[jax.ShapeDtypeStruct((g, rows, n), F32),
                   jax.ShapeDtypeStruct((g, bsz, p), F32),
                   jax.ShapeDtypeStruct((g, bsz, p), F32)],
        scratch_shapes=[pltpu.VMEM((rows, p), F32)] * 4,
        compiler_params=_cparams("parallel"),
        name="s5_scan",
    )(x, kmat, bout_re, bout_im, win_re, win_im, aq_re, aq_im, d_t, h0r, h0i)
    y = y.reshape(g, bsz, n_chunks, q, c).transpose(1, 2, 3, 0, 4).reshape(bsz * seq, g * c)
    return y, jnp.swapaxes(hf_re, 0, 1), jnp.swapaxes(hf_im, 0, 1)


def _s5_step_kernel(u_ref, h0_re_ref, h0_im_ref, a_re_ref, a_im_ref, bb_re_ref, bb_im_ref,
                    cc_re_ref, cc_im_ref, d_ref, y_ref, h_re_ref, h_im_ref):
    u = u_ref[...]
    ub = u.astype(BF16)
    a_re, a_im = a_re_ref[...], a_im_ref[...]
    h0r, h0i = h0_re_ref[...], h0_im_ref[...]
    hr = a_re * h0r - a_im * h0i + jnp.dot(ub, bb_re_ref[...], preferred_element_type=F32)
    hi = a_re * h0i + a_im * h0r + jnp.dot(ub, bb_im_ref[...], preferred_element_type=F32)
    h_re_ref[...] = hr
    h_im_ref[...] = hi
    y = (jnp.dot(hr.astype(BF16), cc_re_ref[...], preferred_element_type=F32)
         - jnp.dot(hi.astype(BF16), cc_im_ref[...], preferred_element_type=F32))
    y_ref[...] = y + u * d_ref[...]


def _block_diag(blocks):
    g, r, c = blocks.shape
    eye = jnp.eye(g, dtype=blocks.dtype)
    return (blocks[:, :, None, :] * eye[:, None, :, None]).reshape(g * r, g * c)


def _s5_step(u, h0_re, h0_im, ab_re, ab_im, bb_re, bb_im, c_re, c_im, d_skip):
    n = u.shape[0]
    gp = S5_GROUPS * S5_STATE
    bbd_re = _block_diag(bb_re).astype(BF16)
    bbd_im = _block_diag(bb_im).astype(BF16)
    ccd_re = _block_diag(jnp.swapaxes(c_re, 1, 2)).astype(BF16)
    ccd_im = _block_diag(jnp.swapaxes(c_im, 1, 2)).astype(BF16)
    y, h_re, h_im = pl.pallas_call(
        _s5_step_kernel,
        out_shape=[jax.ShapeDtypeStruct((n, S5_WIDTH), F32),
                   jax.ShapeDtypeStruct((n, gp), F32),
                   jax.ShapeDtypeStruct((n, gp), F32)],
        compiler_params=pltpu.CompilerParams(vmem_limit_bytes=VMEM_LIMIT),
        name="s5_step",
    )(u, h0_re.reshape(n, gp), h0_im.reshape(n, gp), ab_re.reshape(1, gp), ab_im.reshape(1, gp),
      bbd_re, bbd_im, ccd_re, ccd_im, d_skip.reshape(1, S5_WIDTH))
    return y, h_re.reshape(n, S5_GROUPS, S5_STATE), h_im.reshape(n, S5_GROUPS, S5_STATE)


def _rw_prep_kernel(p_ref, tail_ref, prev_ref, mu_ref, w0_ref, w1_ref, w2_ref, a0_ref, a1_ref, a2_ref,
                    g1_ref, g2_ref, kk_ref, ka_ref, rk_ref, hsum_ref,
                    r_out, lw_out, k_out, v_out, ah_out, b_out, g_out, bonus_out, *, seq_tiles):
    w = RW_WIDTH
    p = p_ref[...]
    if seq_tiles:
        first = pl.program_id(0) % seq_tiles == 0
        before = jnp.where(first, prev_ref[0], tail_ref[7:8, :])
        rowid = lax.broadcasted_iota(jnp.int32, p.shape, 0)
        p_prev = jnp.where(rowid == 0, before, pltpu.roll(p, 1, axis=0))
    else:
        p_prev = prev_ref[...]
    dp = p_prev - p
    r_in, k_in, v_in, z = p[:, :w], p[:, w:2 * w], p[:, 2 * w:3 * w], p[:, 3 * w:]
    dr, dk, dv, dz = dp[:, :w], dp[:, w:2 * w], dp[:, 2 * w:3 * w], dp[:, 3 * w:]
    mu = mu_ref[...]
    r = r_in + dr * mu[0:1]
    xw = z + dz * mu[1:2]
    k = k_in + dk * mu[2:3]
    v = v_in + dv * mu[3:4]
    xa = z + dz * mu[4:5]
    xg = z + dz * mu[5:6]
    t = w0_ref[...] + _bdot(jnp.tanh(_bdot(xw, w1_ref[...])), w2_ref[...])
    nt = -t
    softplus = jnp.maximum(nt, 0.0) + jnp.log(1.0 + jnp.exp(-jnp.abs(nt)))
    w_log = -softplus - 0.5
    lw_out[...] = -jnp.exp(w_log)
    a = _sigmoid(a0_ref[...] + _bdot(_bdot(xa, a1_ref[...]), a2_ref[...]))
    g_out[...] = _bdot(_sigmoid(_bdot(xg, g1_ref[...])), g2_ref[...])
    kk = k * kk_ref[...]
    ss = _dot2_exact_rhs(kk * kk, hsum_ref[...])
    kk = kk * lax.rsqrt(jnp.maximum(ss, 1e-24))
    k2 = k * (1.0 + (a - 1.0) * ka_ref[...])
    r_out[...] = r
    k_out[...] = k2
    v_out[...] = v
    ah_out[...] = -kk
    b_out[...] = kk * a
    bonus_out[...] = _dot2_exact_rhs(r * k2 * rk_ref[...], hsum_ref[...]) * v


def _head_sum_matrix(width):
    idx = np.arange(width) // HEAD_DIM
    return jnp.asarray((idx[:, None] == idx[None, :]).astype(np.float32), dtype=BF16)


def _rw_prep(p, rw_prev, lp, bsz, seq):
    t = p.shape[0]
    w = RW_WIDTH
    if seq == 1:
        tm, seq_tiles = t, 0
        tail_spec = pl.BlockSpec((t, 4 * w), lambda i: (0, 0))
        prev_spec = pl.BlockSpec((t, 4 * w), lambda i: (0, 0))
        prev_arg = rw_prev
    else:
        tm = _row_tile(seq, 512)
        seq_tiles = seq // tm
        tail_spec = pl.BlockSpec((8, 4 * w), lambda i: (jnp.maximum(i * (tm // 8) - 1, 0), 0))
        prev_spec = pl.BlockSpec((1, 1, 4 * w), lambda i: (i // seq_tiles, 0, 0))
        prev_arg = rw_prev.reshape(bsz, 1, 4 * w)
    row = lambda a: a.reshape(1, w)
    full = lambda a: pl.BlockSpec(a.shape, lambda i: (0,) * a.ndim)
    params = [lp['rw_mu'], row(lp['rw_w0']), lp['rw_w1'].astype(BF16), lp['rw_w2'].astype(BF16),
              row(lp['rw_a0']), lp['rw_a1'].astype(BF16), lp['rw_a2'].astype(BF16),
              lp['rw_g1'].astype(BF16), lp['rw_g2'].astype(BF16), row(lp['rw_k_k']),
              row(lp['rw_k_a']), row(lp['rw_r_k']), _head_sum_matrix(w)]
    tile = lambda n: pl.BlockSpec((tm, n), lambda i: (i, 0))
    return pl.pallas_call(
        functools.partial(_rw_prep_kernel, seq_tiles=seq_tiles),
        grid=(t // tm,),
        in_specs=[tile(4 * w), tail_spec, prev_spec] + [full(a) for a in params],
        out_specs=[tile(w)] * 8,
        out_shape=[jax.ShapeDtypeStruct((t, w), F32)] * 8,
        compiler_params=_cparams("parallel"),
        name="rw_prep",
    )(p, p, prev_arg, *params)


def _rw_scan_kernel(r_ref, ah_ref, lw_ref, v_ref, bt_ref, kt_ref, lwt_ref, h0_ref,
                    y_ref, hf_ref, h_scr):
    c = RW_CHUNK
    nseq = r_ref.shape[0]
    ci = pl.program_id(1)

    @pl.when(ci == 0)
    def _():
        h_scr[...] = h0_ref[...]

    row = lax.broadcasted_iota(jnp.int32, (c, c), 0)
    col = lax.broadcasted_iota(jnp.int32, (c, c), 1)
    incl = row >= col
    strict = row > col
    tri_lo = jnp.where(incl, 1.0, 0.0).astype(BF16)
    tri_up = jnp.where(row <= col, 1.0, 0.0).astype(BF16)
    eye = jnp.where(row == col, 1.0, 0.0)
    d = functools.partial(jnp.dot, preferred_element_type=F32)

    def split3(x):
        hi, mid = _split(x)
        return hi, mid, (x - hi.astype(F32) - mid.astype(F32)).astype(BF16)

    cums, cums_t = [], []
    for s in range(nseq):
        p3 = split3(lw_ref[s])
        cums.append(d(tri_lo, p3[0]) + (d(tri_lo, p3[1]) + d(tri_lo, p3[2])))
        q3 = split3(lwt_ref[s, 0])
        cums_t.append(d(q3[0], tri_up) + (d(q3[1], tri_up) + d(q3[2], tri_up)))

    units = [(s, h) for s in range(nseq) for h in range(RW_HEADS)]
    sl = lambda h: slice(h * HEAD_DIM, (h + 1) * HEAD_DIM)
    each = lambda f: [f(s, h) for s, h in units]
    idx = range(len(units))
    cum = each(lambda s, h: cums[s][:, sl(h)])
    cum_t = each(lambda s, h: cums_t[s][sl(h), :])
    ar = each(lambda s, h: None)
    for i, (s, h) in enumerate(units):
        a_t = ah_ref[s, :, sl(h)] * jnp.exp(cum[i] - lw_ref[s, :, sl(h)])
        r_t = r_ref[s, :, sl(h)] * jnp.exp(cum[i])
        ar[i] = jnp.concatenate([a_t, r_t], axis=0).astype(BF16)
    w_inv_t = [jnp.exp(-cum_t[i]) for i in idx]
    cum_end = [cum_t[i][:, c - 1:c] for i in idx]
    dec_t = [jnp.exp(cum_end[i] - cum_t[i]) for i in idx]
    bt = each(lambda s, h: bt_ref[s, 0, sl(h), :])
    kt = each(lambda s, h: kt_ref[s, 0, sl(h), :])
    b_t = [(bt[i] * w_inv_t[i]).astype(BF16) for i in idx]
    k_t = [(kt[i] * w_inv_t[i]).astype(BF16) for i in idx]
    b_d = [(bt[i] * dec_t[i]).astype(BF16) for i in idx]
    k_d = [(kt[i] * dec_t[i]).astype(BF16) for i in idx]
    mb = [d(ar[i], b_t[i]) for i in idx]
    mk = [d(ar[i], k_t[i]) for i in idx]
    a_ab = [jnp.where(strict, mb[i][:c], 0.0) for i in idx]
    a_ak = [jnp.where(strict, mk[i][:c], 0.0).astype(BF16) for i in idx]
    a_rb = [jnp.where(incl, mb[i][c:], 0.0).astype(BF16) for i in idx]
    a_rk = [jnp.where(incl, mk[i][c:], 0.0).astype(BF16) for i in idx]
    pw = a_ab
    inv = [eye + a_ab[i] for i in idx]
    for _ in range(int(math.log2(c)) - 1):
        pw = [_bdot(pw[i], pw[i]) for i in idx]
        inv = [inv[i] + _bdot(inv[i], pw[i]) for i in idx]
    h0 = each(lambda s, h: h_scr[s, h])
    vb = each(lambda s, h: v_ref[s, :, sl(h)].astype(BF16))
    ar_h = [d(ar[i], h0[i].astype(BF16)) for i in idx]
    rhs = [ar_h[i][:c] + d(a_ak[i], vb[i]) for i in idx]
    ub = [_bdot(inv[i], rhs[i]).astype(BF16) for i in idx]
    y = [ar_h[i][c:] + d(a_rb[i], ub[i]) + d(a_rk[i], vb[i]) for i in idx]
    h_new = [jnp.exp(cum_end[i]) * h0[i] + d(b_d[i], ub[i]) + d(k_d[i], vb[i]) for i in idx]
    for i, (s, h) in enumerate(units):
        h_scr[s, h] = h_new[i]
    for s in range(nseq):
        y_ref[s] = jnp.concatenate([y[s * RW_HEADS + h] for h in range(RW_HEADS)], axis=1)

    @pl.when(ci == pl.num_programs(1) - 1)
    def _():
        hf_ref[...] = h_scr[...]


def _rw_scan(r, ah, lw, v, b, k2, s0, bsz, seq):
    w, c = RW_WIDTH, RW_CHUNK
    pad = (-seq) % c
    sp = seq + pad

    def rows(a):
        a = a.reshape(bsz, seq, w)
        return jnp.pad(a, ((0, 0), (0, pad), (0, 0))) if pad else a

    r3, ah3, lw3, v3, b3, k3 = (rows(a) for a in (r, ah, lw, v, b, k2))
    n_chunks = sp // c
    tr = lambda a: jnp.swapaxes(a.reshape(bsz, n_chunks, c, w), 2, 3)
    h0 = jnp.swapaxes(s0, 2, 3)
    ns = RW_SEQS_PER_STEP
    assert bsz % ns == 0
    rspec = pl.BlockSpec((ns, c, w), lambda i, j: (i, j, 0))
    cspec = pl.BlockSpec((ns, 1, w, c), lambda i, j: (i, j, 0, 0))
    sspec = pl.BlockSpec((ns, RW_HEADS, HEAD_DIM, HEAD_DIM), lambda i, j: (i, 0, 0, 0))
    y, hf = pl.pallas_call(
        _rw_scan_kernel,
        grid=(bsz // ns, n_chunks),
        in_specs=[rspec, rspec, rspec, rspec, cspec, cspec, cspec, sspec],
        out_specs=[rspec, sspec],
        out_shape=[jax.ShapeDtypeStruct((bsz, sp, w), F32),
                   jax.ShapeDtypeStruct((bsz, RW_HEADS, HEAD_DIM, HEAD_DIM), F32)],
        scratch_shapes=[pltpu.VMEM((ns, RW_HEADS, HEAD_DIM, HEAD_DIM), F32)],
        compiler_params=_cparams("parallel", "arbitrary"),
        name="rw_scan",
    )(r3, ah3, lw3, v3, tr(b3), tr(k3), tr(lw3), h0)
    return y[:, :seq].reshape(bsz * seq, w), jnp.swapaxes(hf, 2, 3)


def _rope_kernel(qkv_ref, cos_ref, sin_lo_ref, sin_hi_ref, q_ref, k_ref, v_ref):
    w = ATT_WIDTH
    half = ROPE_DIM // 2
    x = qkv_ref[...]
    rep = lambda t: jnp.concatenate([t] * (w // 128), axis=1)
    cos, s_lo, s_hi = rep(cos_ref[...]), rep(sin_lo_ref[...]), rep(sin_hi_ref[...])

    def rot(t):
        up = pltpu.roll(t, w - half, axis=1)
        dn = pltpu.roll(t, half, axis=1)
        return t * cos + up * s_lo + dn * s_hi

    q_ref[...] = rot(x[:, :w]) * ATT_SCALE
    k_ref[...] = rot(x[:, w:2 * w])
    v_ref[...] = x[:, 2 * w:]


def _rope_tables(pos):
    half = ROPE_DIM // 2
    inv_freq = jnp.exp(-math.log(ROPE_THETA) * jnp.arange(half, dtype=jnp.float32) * (2.0 / ROPE_DIM))
    ang = pos.astype(jnp.float32)[:, None] * inv_freq[None, :]
    cos, sin = jnp.cos(ang), jnp.sin(ang)
    n = pos.shape[0]
    one = jnp.ones((n, HEAD_DIM - ROPE_DIM), F32)
    zero = jnp.zeros((n, HEAD_DIM - ROPE_DIM), F32)
    z8 = jnp.zeros((n, half), F32)
    cos_h = jnp.concatenate([cos, cos, one], axis=1)
    lo_h = jnp.concatenate([-sin, z8, zero], axis=1)
    hi_h = jnp.concatenate([z8, sin, zero], axis=1)
    two = lambda t: jnp.concatenate([t, t], axis=1)
    return two(cos_h), two(lo_h), two(hi_h)


def _rope(qkv, tables):
    t = qkv.shape[0]
    period = tables[0].shape[0]
    tm = _row_tile(period, 512)
    nb = period // tm
    w = ATT_WIDTH
    tspec = pl.BlockSpec((tm, 128), lambda i: (i % nb, 0))
    ospec = pl.BlockSpec((tm, w), lambda i: (i, 0))
    return pl.pallas_call(
        _rope_kernel,
        grid=(t // tm,),
        in_specs=[pl.BlockSpec((tm, 3 * w), lambda i: (i, 0)), tspec, tspec, tspec],
        out_specs=[ospec] * 3,
        out_shape=[jax.ShapeDtypeStruct((t, w), F32)] * 3,
        compiler_params=_cparams("parallel"),
        name="rope",
    )(qkv, *tables)


def _multiplicity(dist):
    m = np.zeros(dist.shape, np.float64)
    for window, dil in DILATED_PATTERNS:
        m += ((dist >= 0) & (dist <= window) & (dist % dil == 0))
    return m


def _dist_bias(dist):
    m = _multiplicity(dist)
    return np.where(m > 0, np.log(np.maximum(m, 1.0)), NEG_INF).astype(np.float32)


def _attn_kernel(q_ref, kc_ref, kp_ref, vc_ref, vp_ref, o_ref, kwin, vwin, acc_scr, m_scr, l_scr):
    tile, sub = ATT_TILE, ATT_SUB
    first = pl.program_id(2) == 0
    kwin[0:tile] = kp_ref[0]
    kwin[tile:] = kc_ref[0]
    vwin[0:tile] = vp_ref[0]
    vwin[tile:] = vc_ref[0]
    row = lax.broadcasted_iota(jnp.int32, (sub, 2 * sub), 0)
    col = lax.broadcasted_iota(jnp.int32, (sub, 2 * sub), 1)
    bias = jnp.where(col < sub, jnp.where(col >= row, 0.0, NEG_INF), jnp.where(col - sub <= row, 0.0, NEG_INF))
    bias_first = jnp.where(col < sub, NEG_INF, bias)
    head0 = lax.broadcasted_iota(jnp.int32, (sub, 2 * HEAD_DIM), 1) < HEAD_DIM
    nt = (((1,), (1,)), ((), ()))
    for p, (window, dil) in enumerate(DILATED_PATTERNS):
        assert window == dil * sub
        nsub = tile // dil // sub
        for rho in range(dil):
            for a in range(nsub):
                q_rows = pl.ds(rho + dil * sub * a, sub, stride=dil) if dil > 1 else pl.ds(sub * a, sub)
                k_start = tile + rho + dil * sub * (a - 1)
                k_rows = pl.ds(k_start, 2 * sub, stride=dil) if dil > 1 else pl.ds(k_start, 2 * sub)
                q = q_ref[0, q_rows, :]
                k = kwin[k_rows, :].astype(BF16)
                v = vwin[k_rows, :].astype(BF16)
                b = jnp.where(first, bias_first, bias) if a == 0 else bias
                stats = []
                for hh in range(2):
                    qh = jnp.where(head0 if hh == 0 else jnp.logical_not(head0), q, 0.0).astype(BF16)
                    s = lax.dot_general(qh, k, nt, preferred_element_type=F32) + b
                    m = jnp.max(s, axis=1, keepdims=True)
                    e = jnp.exp(s - m)
                    l = jnp.sum(e, axis=1, keepdims=True)
                    acc = jnp.dot(e.astype(BF16), v, preferred_element_type=F32)
                    stats.append((m, l, acc))
                pick = lambda i: jnp.where(head0, stats[0][i], stats[1][i])
                m_scr[p, q_rows, :] = pick(0)
                l_scr[p, q_rows, :] = pick(1)
                acc_scr[p, q_rows, :] = pick(2)
    ms = [m_scr[p] for p in range(len(DILATED_PATTERNS))]
    m = functools.reduce(jnp.maximum, ms)
    ws = [jnp.exp(mp - m) for mp in ms]
    den = sum(w * l_scr[p] for p, w in enumerate(ws))
    num = sum(w * acc_scr[p] for p, w in enumerate(ws))
    o_ref[0] = num / den


def _attn_prompt(q, k, v, bsz, seq):
    tile = ATT_TILE
    assert seq % tile == 0 and tile == WIN_MAX
    lanes = 2 * HEAD_DIM
    npair = ATT_WIDTH // lanes
    npat = len(DILATED_PATTERNS)
    r3 = lambda a: a.reshape(bsz, seq, ATT_WIDTH)
    cur = pl.BlockSpec((1, tile, lanes), lambda b, h, i: (b, i, h))
    prev = pl.BlockSpec((1, tile, lanes), lambda b, h, i: (b, jnp.maximum(i - 1, 0), h))
    o = pl.pallas_call(
        _attn_kernel,
        grid=(bsz, npair, seq // tile),
        in_specs=[cur, cur, prev, cur, prev],
        out_specs=cur,
        out_shape=jax.ShapeDtypeStruct((bsz, seq, ATT_WIDTH), F32),
        scratch_shapes=[pltpu.VMEM((2 * tile, lanes), F32), pltpu.VMEM((2 * tile, lanes), F32),
                        pltpu.VMEM((npat, tile, lanes), F32), pltpu.VMEM((npat, tile, lanes), F32),
                        pltpu.VMEM((npat, tile, lanes), F32)],
        compiler_params=_cparams("parallel", "parallel", "arbitrary"),
        name="attn_prompt",
    )(r3(q), r3(k), r3(k), r3(v), r3(v))
    return o.reshape(bsz * seq, ATT_WIDTH)


def _attn_step_kernel(q_ref, kn_ref, vn_ref, kc_ref, vc_ref, bias_ref, sel_ref, selt_ref, o_ref):
    q = q_ref[0]
    kn, vn = kn_ref[0], vn_ref[0]
    kc, vc = kc_ref[0], vc_ref[0]
    sel, selt = sel_ref[...], selt_ref[...]
    s_c = _dot2_exact_rhs(kc * q, sel) + bias_ref[...]
    s_n = _dot2_exact_rhs(kn * q, sel) + math.log(len(DILATED_PATTERNS))
    m = jnp.maximum(jnp.max(s_c, axis=0, keepdims=True), s_n)
    p_c = jnp.exp(s_c - m)
    p_n = jnp.exp(s_n - m)
    den = jnp.sum(p_c, axis=0, keepdims=True) + p_n
    pe_c = _dot2_exact_rhs(p_c, selt)
    pe_n = _dot2_exact_rhs(p_n, selt)
    den_e = _dot2_exact_rhs(den, selt)
    o_ref[0] = (jnp.sum(pe_c * vc, axis=0, keepdims=True) + pe_n * vn) / den_e


def _attn_step(q, k_new, v_new, k_buf, v_buf):
    n, n_buf = k_buf.shape[0], k_buf.shape[1]
    w = ATT_WIDTH
    dist = n_buf - np.arange(n_buf)
    bias = np.broadcast_to(_dist_bias(dist)[:, None], (n_buf, 128)).copy()
    head = np.arange(w) // HEAD_DIM
    sel = (head[:, None] == np.arange(128)[None, :]).astype(np.float32)
    vec = pl.BlockSpec((1, 1, w), lambda i: (i, 0, 0))
    buf = pl.BlockSpec((1, n_buf, w), lambda i: (i, 0, 0))
    full = lambda s: pl.BlockSpec(s, lambda i: (0, 0))
    o = pl.pallas_call(
        _attn_step_kernel,
        grid=(n,),
        in_specs=[vec, vec, vec, buf, buf, full((n_buf, 128)), full((w, 128)), full((128, w))],
        out_specs=vec,
        out_shape=jax.ShapeDtypeStruct((n, 1, w), F32),
        compiler_params=_cparams("parallel"),
        name="attn_step",
    )(q.reshape(n, 1, w), k_new.reshape(n, 1, w), v_new.reshape(n, 1, w),
      k_buf.reshape(n, n_buf, w), v_buf.reshape(n, n_buf, w), jnp.asarray(bias),
      jnp.asarray(sel, dtype=BF16), jnp.asarray(sel.T, dtype=BF16))
    return o.reshape(n, w)


def _gelu_tanh(x):
    return 0.5 * x * (1.0 + jnp.tanh(math.sqrt(2.0 / math.pi) * (x + 0.044715 * (x * x * x))))


def _mix_kernel(x_ref, ys_ref, yr_ref, bonus_ref, g_ref, ya_ref, wglu_ref, bglu_ref, gng_ref, gnb_ref,
                havg_ref, wout_ref, lng_ref, lnb_ref, o_ref, *, alpha):
    ys = _gelu_tanh(ys_ref[...])
    ya = ys * _sigmoid(_bdot(ys, wglu_ref[...]) + bglu_ref[...])
    yr = yr_ref[...]
    havg = havg_ref[...]
    mean = _dot2_exact_rhs(yr, havg) * (1.0 / HEAD_DIM)
    yc = yr - mean
    var = _dot2_exact_rhs(yc * yc, havg) * (1.0 / HEAD_DIM)
    yb = (yc * lax.rsqrt(var + RW_GN_EPS) * gng_ref[...] + gnb_ref[...] + bonus_ref[...]) * g_ref[...]
    wout = wout_ref[...]
    o1, o2 = S5_WIDTH, S5_WIDTH + RW_WIDTH
    mix = _bdot(ya, wout[:o1]) + _bdot(yb, wout[o1:o2]) + _bdot(ya_ref[...], wout[o2:])
    o_ref[...] = _layer_norm(alpha * x_ref[...] + mix, lng_ref[...], lnb_ref[...])


def _mix(x, y_s5, y_rw, bonus, gate, y_att, lp, alpha):
    t = x.shape[0]
    tm = _row_tile(t, 512)
    tile = lambda n: pl.BlockSpec((tm, n), lambda i: (i, 0))
    full = lambda a: pl.BlockSpec(a.shape, lambda i: (0,) * a.ndim)
    params = [lp['s5_w_glu'].astype(BF16), lp['s5_b_glu'].reshape(1, -1), lp['rw_gn_g'].reshape(1, -1),
              lp['rw_gn_b'].reshape(1, -1), _head_sum_matrix(RW_WIDTH), lp['w_out'].astype(BF16),
              lp['ln_g'][0].reshape(1, -1), lp['ln_b'][0].reshape(1, -1)]
    return pl.pallas_call(
        functools.partial(_mix_kernel, alpha=alpha),
        grid=(t // tm,),
        in_specs=[tile(D_MODEL), tile(S5_WIDTH), tile(RW_WIDTH), tile(RW_WIDTH), tile(RW_WIDTH),
                  tile(ATT_WIDTH)] + [full(a) for a in params],
        out_specs=tile(D_MODEL),
        out_shape=jax.ShapeDtypeStruct((t, D_MODEL), F32),
        compiler_params=_cparams("parallel"),
        name="mix",
    )(x, y_s5, y_rw, bonus, gate, y_att, *params)


def _rank_select(vals, n_rows, keep):
    ridx = lax.broadcasted_iota(jnp.int32, vals.shape, 0)
    cnt = jnp.zeros(vals.shape, jnp.int32)
    for j in range(n_rows):
        vj = vals[j:j + 1, :]
        beats = jnp.where(vj > vals, 1, jnp.where(vj == vals, jnp.where(ridx > j, 1, 0), 0))
        cnt = cnt + beats
    return cnt < keep


def _router_kernel(x_ref, wt_ref, bias_ref, gates_ref, gscore_scr, ekeep_scr):
    e, ng = N_EXPERTS, N_EXPERT_GROUPS
    per = e // ng
    logits = lax.dot_general(wt_ref[...].astype(BF16), x_ref[...].astype(BF16), (((1,), (1,)), ((), ())),
                             preferred_element_type=F32)
    scores = _sigmoid(logits)
    sel = scores + bias_ref[:, 0:1]
    t = sel.shape[1]
    pos = lax.broadcasted_iota(jnp.int32, (per, t), 0)
    for gi in range(ng):
        grp = sel[gi * per:(gi + 1) * per]
        m1 = jnp.max(grp, axis=0, keepdims=True)
        first = jnp.min(jnp.where(grp == m1, pos, per), axis=0, keepdims=True)
        m2 = jnp.max(jnp.where(pos == first, NEG_INF, grp), axis=0, keepdims=True)
        gscore_scr[gi:gi + 1, :] = m1 + m2
    gkeep = jnp.where(_rank_select(gscore_scr[...], ng, TOPK_GROUPS), 1.0, 0.0)
    for gi in range(ng):
        ekeep_scr[gi * per:(gi + 1) * per, :] = jnp.broadcast_to(gkeep[gi:gi + 1], (per, t))
    masked = jnp.where(ekeep_scr[...] > 0.0, sel, NEG_INF)
    chosen = _rank_select(masked, e, TOP_K)
    w = jnp.where(chosen, scores, 0.0)
    gates_ref[...] = w / jnp.sum(w, axis=0, keepdims=True) * ROUTED_SCALE


def _router(x, w_router, router_bias):
    t = x.shape[0]
    tm = _row_tile(t, 512)
    e = N_EXPERTS
    gates_t = pl.pallas_call(
        _router_kernel,
        grid=(t // tm,),
        in_specs=[pl.BlockSpec((tm, D_MODEL), lambda i: (i, 0)),
                  pl.BlockSpec((e, D_MODEL), lambda i: (0, 0)),
                  pl.BlockSpec((e, 128), lambda i: (0, 0))],
        out_specs=pl.BlockSpec((e, tm), lambda i: (0, i)),
        out_shape=jax.ShapeDtypeStruct((e, t), F32),
        scratch_shapes=[pltpu.VMEM((N_EXPERT_GROUPS, tm), F32), pltpu.VMEM((e, tm), F32)],
        compiler_params=_cparams("parallel"),
        name="router",
    )(x, w_router.T, jnp.broadcast_to(router_bias.reshape(e, 1), (e, 128)))
    return gates_t.T


def _moe_kernel(x_ref, gates_ref, wg_ref, wu_ref, wd_ref, sg_ref, su_ref, sd_ref, lng_ref, lnb_ref,
                o_ref, xb_scr, acc_scr, *, alpha):
    e = pl.program_id(1)
    ne = pl.num_programs(1)

    @pl.when(e == 0)
    def _():
        xb = x_ref[...].astype(BF16)
        xb_scr[...] = xb
        h = _silu(jnp.dot(xb, sg_ref[...].astype(BF16), preferred_element_type=F32)) * jnp.dot(
            xb, su_ref[...].astype(BF16), preferred_element_type=F32)
        acc_scr[...] = jnp.dot(h.astype(BF16), sd_ref[...].astype(BF16), preferred_element_type=F32)

    xb = xb_scr[...]
    h = _silu(jnp.dot(xb, wg_ref[0].astype(BF16), preferred_element_type=F32)) * jnp.dot(
        xb, wu_ref[0].astype(BF16), preferred_element_type=F32)
    onehot = jnp.where(lax.broadcasted_iota(jnp.int32, (N_EXPERTS, EXPERT_FF), 0) == e, 1.0, 0.0).astype(BF16)
    gate = _dot2_exact_rhs(gates_ref[...], onehot)
    gate = jnp.concatenate([gate] * (D_MODEL // EXPERT_FF), axis=1)
    acc_scr[...] += jnp.dot(h.astype(BF16), wd_ref[0].astype(BF16), preferred_element_type=F32) * gate

    @pl.when(e == ne - 1)
    def _():
        o_ref[...] = _layer_norm(alpha * x_ref[...] + acc_scr[...], lng_ref[...], lnb_ref[...])


def _moe(x, gates, lp, alpha):
    t = x.shape[0]
    tm = _row_tile(t, 1024)
    d, f, e = D_MODEL, EXPERT_FF, N_EXPERTS
    full = lambda a: pl.BlockSpec(a.shape, lambda i, j: (0,) * a.ndim)
    lng, lnb = lp['ln_g'][1].reshape(1, -1), lp['ln_b'][1].reshape(1, -1)
    return pl.pallas_call(
        functools.partial(_moe_kernel, alpha=alpha),
        grid=(t // tm, e),
        in_specs=[pl.BlockSpec((tm, d), lambda i, j: (i, 0)),
                  pl.BlockSpec((tm, e), lambda i, j: (i, 0)),
                  pl.BlockSpec((1, d, f), lambda i, j: (j, 0, 0)),
                  pl.BlockSpec((1, d, f), lambda i, j: (j, 0, 0)),
                  pl.BlockSpec((1, f, d), lambda i, j: (j, 0, 0)),
                  full(lp['shared_w_gate']), full(lp['shared_w_up']), full(lp['shared_w_down']),
                  full(lng), full(lnb)],
        out_specs=pl.BlockSpec((tm, d), lambda i, j: (i, 0)),
        out_shape=jax.ShapeDtypeStruct((t, d), F32),
        scratch_shapes=[pltpu.VMEM((tm, d), BF16), pltpu.VMEM((tm, d), F32)],
        compiler_params=_cparams("parallel", "arbitrary"),
        name="moe",
    )(x, gates, lp['expert_w_gate'], lp['expert_w_up'], lp['expert_w_down'],
      lp['shared_w_gate'], lp['shared_w_up'], lp['shared_w_down'], lng, lnb)


def _trunk_layer(x, bsz, seq, pos0, h0_re, h0_im, rw_s0, rw_prev, k_buf, v_buf, lp, s5p, rope_tables, alpha):
    t = bsz * seq
    u, p_rw, qkv = _proj(x, lp['w_in_bf16'])
    (kmat, bout_re, bout_im, win_re, win_im, aq_re, aq_im, ab_re, ab_im, bb_re, bb_im) = s5p
    if seq == 1:
        y_s5, h_re, h_im = _s5_step(u, h0_re, h0_im, ab_re, ab_im, bb_re, bb_im,
                                    lp['s5_c_re'], lp['s5_c_im'], lp['s5_d'])
    else:
        y_s5, h_re, h_im = _s5_scan(u, (kmat, bout_re, bout_im, win_re, win_im, aq_re, aq_im),
                                    lp['s5_d'], h0_re, h0_im, bsz, seq)
    r, lw, k2, v, ah, b, gate, bonus = _rw_prep(p_rw, rw_prev, lp, bsz, seq)
    y_rw, rw_s = _rw_scan(r, ah, lw, v, b, k2, rw_s0, bsz, seq)
    rw_row = p_rw.reshape(bsz, seq, 4 * RW_WIDTH)[:, -1]
    q, k, vv = _rope(qkv, rope_tables)
    if seq == 1:
        y_att = _attn_step(q, k, vv, k_buf, v_buf)
    else:
        assert k_buf is None
        y_att = _attn_prompt(q, k, vv, bsz, seq)
    keep = min(WIN_MAX, seq)
    k_new = k.reshape(bsz, seq, ATT_HEADS, HEAD_DIM)[:, seq - keep:]
    v_new = vv.reshape(bsz, seq, ATT_HEADS, HEAD_DIM)[:, seq - keep:]
    x1 = _mix(x, y_s5, y_rw, bonus, gate, y_att, lp, alpha)
    gates = _router(x1, lp['w_router'], lp['router_bias'])
    x2 = _moe(x1, gates, lp, alpha)
    return x2, (h_re, h_im, rw_s, rw_row, k_new, v_new)


def kernel(x_prompt, x_sample, state_s5_re, state_s5_im, state_rwkv, state_rwkv_shift, cache_attn_k, cache_attn_v, w_in, s5_lambda_re, s5_lambda_im, s5_b_re, s5_b_im, s5_c_re, s5_c_im, s5_d, s5_log_step, s5_w_glu, s5_b_glu, rw_mu, rw_w0, rw_w1, rw_w2, rw_a0, rw_a1, rw_a2, rw_g1, rw_g2, rw_k_k, rw_k_a, rw_r_k, rw_gn_g, rw_gn_b, w_out, ln_g, ln_b, w_router, router_bias, expert_w_gate, expert_w_up, expert_w_down, shared_w_gate, shared_w_up, shared_w_down):
    depth = w_in.shape[0]
    alpha = (2 * depth) ** 0.25
    bsz, seq, d = x_prompt.shape
    dbsz, dseq, _ = x_sample.shape
    past = cache_attn_k.shape[2]
    names = dict(s5_c_re=s5_c_re, s5_c_im=s5_c_im, s5_d=s5_d, s5_w_glu=s5_w_glu, s5_b_glu=s5_b_glu,
                 rw_mu=rw_mu, rw_w0=rw_w0, rw_w1=rw_w1, rw_w2=rw_w2, rw_a0=rw_a0, rw_a1=rw_a1, rw_a2=rw_a2,
                 rw_g1=rw_g1, rw_g2=rw_g2, rw_k_k=rw_k_k, rw_k_a=rw_k_a, rw_r_k=rw_r_k, rw_gn_g=rw_gn_g,
                 rw_gn_b=rw_gn_b, w_out=w_out, ln_g=ln_g, ln_b=ln_b, w_router=w_router,
                 router_bias=router_bias, expert_w_gate=expert_w_gate, expert_w_up=expert_w_up,
                 expert_w_down=expert_w_down, shared_w_gate=shared_w_gate, shared_w_up=shared_w_up,
                 shared_w_down=shared_w_down)
    assert dseq == 1
    tables_p = _rope_tables(jnp.arange(seq))
    tables_s = _rope_tables(jnp.full((dbsz,), PAST_LEN))
    yp = x_prompt.reshape(bsz * seq, d)
    ys = x_sample.reshape(dbsz * dseq, d)
    new_p, new_s = [], []
    for l in range(depth):
        lp = {k: v[l] for k, v in names.items()}
        lp['w_in_bf16'] = w_in[l].astype(BF16)
        s5p = _s5_prep(s5_lambda_re[l], s5_lambda_im[l], s5_log_step[l], s5_b_re[l], s5_b_im[l],
                       s5_c_re[l], s5_c_im[l])
        zs = jnp.zeros((bsz, S5_GROUPS, S5_STATE), F32)
        z_s = jnp.zeros((bsz, RW_HEADS, HEAD_DIM, HEAD_DIM), F32)
        zrow = jnp.zeros((bsz, 4 * RW_WIDTH), F32)
        yp, st_p = _trunk_layer(yp, bsz, seq, 0, zs, zs, z_s, zrow, None, None, lp, s5p, tables_p, alpha)
        ys, st_s = _trunk_layer(ys, dbsz, dseq, past, state_s5_re[l], state_s5_im[l], state_rwkv[l],
                                state_rwkv_shift[l], cache_attn_k[l], cache_attn_v[l], lp, s5p, tables_s, alpha)
        new_p.append(st_p)
        new_s.append(st_s)
    stack = lambda sts: [jnp.stack([s[i] for s in sts], 0) for i in range(6)]
    out_p, out_s = stack(new_p), stack(new_s)
    assert past == WIN_MAX
    out_s[4] = jnp.concatenate([cache_attn_k[:, :, dseq:], out_s[4]], axis=2)
    out_s[5] = jnp.concatenate([cache_attn_v[:, :, dseq:], out_s[5]], axis=2)
    return (yp.reshape(bsz, seq, d), ys.reshape(dbsz, dseq, d), *out_p, *out_s)
```

```python
import functools
import math

import numpy as np
import jax
import jax.numpy as jnp
from jax import lax
from jax.experimental import pallas as pl
from jax.experimental.pallas import tpu as pltpu

F32 = jnp.float32
BF16 = jnp.bfloat16

D_MODEL = 1024
PAST_LEN = 8192
HEAD_DIM = 64
S5_WIDTH = 256
S5_GROUP = 16
S5_GROUPS = 16
S5_STATE = 64
RW_WIDTH = 384
RW_HEADS = 6
ATT_WIDTH = 384
ATT_HEADS = 6
N_IN = S5_WIDTH + 4 * RW_WIDTH + 3 * ATT_WIDTH
RW_GN_EPS = 64e-5
DILATED_PATTERNS = ((128, 1), (512, 4), (2048, 16))
WIN_MAX = 2048
ATT_SCALE = HEAD_DIM ** -0.5
ROPE_THETA = 500000.0
ROPE_DIM = HEAD_DIM // 4
NEG_INF = -1e30
N_EXPERTS = 64
N_EXPERT_GROUPS = 8
TOPK_GROUPS = 4
TOP_K = 8
EXPERT_FF = 256
ROUTED_SCALE = 2.5
LN_EPS = 1e-5

S5_CHUNK = 16
S5_STATE_BLOCK = 256
S5_OUT_BLOCK = 512
RW_CHUNK = 64
RW_SEQS_PER_STEP = 2
ATT_TILE = 2048
ATT_SUB = 128
VMEM_LIMIT = 56 * 1024 * 1024


def _cparams(*sem):
    return pltpu.CompilerParams(dimension_semantics=sem, vmem_limit_bytes=VMEM_LIMIT)


def _bdot(a, b):
    return jnp.dot(a.astype(BF16), b.astype(BF16), preferred_element_type=F32)


def _split(x):
    hi = x.astype(BF16)
    lo = (x - hi.astype(F32)).astype(BF16)
    return hi, lo


def _dot3(a, b):
    ah, al = _split(a)
    bh, bl = _split(b)
    d = functools.partial(jnp.dot, preferred_element_type=F32)
    return d(ah, bh) + (d(ah, bl) + d(al, bh))


def _dot2_exact_rhs(a, b_bf16):
    ah, al = _split(a)
    d = functools.partial(jnp.dot, preferred_element_type=F32)
    return d(ah, b_bf16) + d(al, b_bf16)


def _sigmoid(x):
    return 1.0 / (1.0 + jnp.exp(-x))


def _silu(x):
    return x * _sigmoid(x)


def _layer_norm(x, g, b):
    mu = jnp.mean(x, axis=-1, keepdims=True)
    xc = x - mu
    var = jnp.mean(xc * xc, axis=-1, keepdims=True)
    return xc * lax.rsqrt(var + LN_EPS) * g + b


def _row_tile(t, pref):
    tm = min(t, pref)
    assert t % tm == 0
    return tm


def _proj_kernel(x_ref, w_ref, u_ref, ub_ref, rw_ref, qkv_ref):
    p = jnp.dot(x_ref[...].astype(BF16), w_ref[...], preferred_element_type=F32)
    o1 = S5_WIDTH
    o2 = o1 + 4 * RW_WIDTH
    u_ref[...] = p[:, :o1]
    ub_ref[...] = p[:, :o1].astype(BF16)
    rw_ref[...] = p[:, o1:o2]
    qkv_ref[...] = p[:, o2:]


def _proj(x, w_bf16):
    t = x.shape[0]
    tm = _row_tile(t, 512)
    return pl.pallas_call(
        _proj_kernel,
        grid=(t // tm,),
        in_specs=[pl.BlockSpec((tm, D_MODEL), lambda i: (i, 0)),
                  pl.BlockSpec((D_MODEL, N_IN), lambda i: (0, 0))],
        out_specs=[pl.BlockSpec((tm, S5_WIDTH), lambda i: (i, 0)),
                   pl.BlockSpec((tm, S5_WIDTH), lambda i: (i, 0)),
                   pl.BlockSpec((tm, 4 * RW_WIDTH), lambda i: (i, 0)),
                   pl.BlockSpec((tm, 3 * ATT_WIDTH), lambda i: (i, 0))],
        out_shape=[jax.ShapeDtypeStruct((t, S5_WIDTH), F32),
                   jax.ShapeDtypeStruct((t, S5_WIDTH), BF16),
                   jax.ShapeDtypeStruct((t, 4 * RW_WIDTH), F32),
                   jax.ShapeDtypeStruct((t, 3 * ATT_WIDTH), F32)],
        compiler_params=_cparams("parallel"),
        name="proj",
    )(x, w_bf16)


def _s5_discretize(lr, li, log_step):
    dt = jnp.exp(log_step)
    mag = jnp.exp(lr * dt)
    ab_re = mag * jnp.cos(li * dt)
    ab_im = mag * jnp.sin(li * dt)
    den = lr * lr + li * li
    cf_re = ((ab_re - 1.0) * lr + ab_im * li) / den
    cf_im = (ab_im * lr - (ab_re - 1.0) * li) / den
    return dt, ab_re, ab_im, cf_re, cf_im


def _s5_power(lr, li, dt, n):
    mag = jnp.exp(n * (lr * dt))
    ang = n * (li * dt)
    return mag * jnp.cos(ang), mag * jnp.sin(ang)


def _s5_prep_kernel(lr_ref, li_ref, ls_ref, lrc_ref, lic_ref, lsc_ref, bt_re_ref, bt_im_ref,
                    ct_re_ref, ct_im_ref,
                    kmat_ref, bout_re_ref, bout_im_ref, win_re_ref, win_im_ref,
                    aq_re_ref, aq_im_ref, ab_re_ref, ab_im_ref, bb_re_ref, bb_im_ref):
    q = S5_CHUNK
    n_rows = q * S5_GROUP
    lr, li, ls = lr_ref[0], li_ref[0], ls_ref[0]
    dt, ab_re, ab_im, cf_re, cf_im = _s5_discretize(lr, li, ls)
    ab_re_ref[0] = ab_re
    ab_im_ref[0] = ab_im
    aq_re, aq_im = _s5_power(lr, li, dt, jnp.float32(q))
    aq_re_ref[0] = aq_re
    aq_im_ref[0] = aq_im
    bt_re, bt_im = bt_re_ref[0], bt_im_ref[0]
    bb_re = cf_re * bt_re - cf_im * bt_im
    bb_im = cf_re * bt_im + cf_im * bt_re
    bb_re_ref[0] = bb_re[:S5_GROUP]
    bb_im_ref[0] = bb_im[:S5_GROUP]
    step = lax.broadcasted_iota(jnp.int32, (n_rows, S5_STATE), 0) // S5_GROUP
    pw_re, pw_im = _s5_power(lr, li, dt, (q - 1 - step).astype(F32))
    bout_re_ref[0] = (bb_re * pw_re - bb_im * pw_im).astype(BF16)
    bout_im_ref[0] = (bb_re * pw_im + bb_im * pw_re).astype(BF16)
    lrc, lic, lsc = lrc_ref[0], lic_ref[0], lsc_ref[0]
    dtc, abc_re, abc_im, _, _ = _s5_discretize(lrc, lic, lsc)
    lag = (lax.broadcasted_iota(jnp.int32, (S5_STATE, n_rows), 1) // S5_GROUP).astype(F32)
    p0_re, p0_im = _s5_power(lrc, lic, dtc, lag)
    ct_re, ct_im = ct_re_ref[0], ct_im_ref[0]
    e0_re = ct_re * p0_re - ct_im * p0_im
    e0_im = ct_re * p0_im + ct_im * p0_re
    e1_re = e0_re * abc_re - e0_im * abc_im
    e1_im = e0_re * abc_im + e0_im * abc_re
    win_re_ref[0] = e1_re.astype(BF16)
    win_im_ref[0] = (-e1_im).astype(BF16)
    g = _dot3(bb_re[:S5_GROUP], e0_re) - _dot3(bb_im[:S5_GROUP], e0_im)
    lane = lax.broadcasted_iota(jnp.int32, (S5_GROUP, n_rows), 1)
    blocks = []
    for i in range(q):
        if i == 0:
            blocks.append(g)
        else:
            shifted = pltpu.roll(g, S5_GROUP * i, axis=1)
            blocks.append(jnp.where(lane >= S5_GROUP * i, shifted, 0.0))
    kmat_ref[0] = jnp.concatenate(blocks, axis=0).astype(BF16)


def _s5_prep(lam_re, lam_im, log_step, b_re, b_im, c_re, c_im):
    g, p, c, q = S5_GROUPS, S5_STATE, S5_GROUP, S5_CHUNK
    n = q * c
    row = lambda a: a.reshape(g, 1, p)
    col = lambda a: jnp.broadcast_to(a.reshape(g, p, 1), (g, p, n))
    ls_row = jnp.broadcast_to(log_step.reshape(g, 1, 1), (g, 1, p))
    ls_col = jnp.broadcast_to(log_step.reshape(g, 1, 1), (g, p, n))
    bt = lambda a: jnp.tile(jnp.swapaxes(a, 1, 2), (1, q, 1))
    ct = lambda a: jnp.tile(jnp.swapaxes(a, 1, 2), (1, 1, q))
    spec = lambda *s: pl.BlockSpec((1,) + s, lambda i: (i, 0, 0))
    sds = lambda s, d: jax.ShapeDtypeStruct((g,) + s, d)
    return pl.pallas_call(
        _s5_prep_kernel,
        grid=(g,),
        in_specs=[spec(1, p), spec(1, p), spec(1, p), spec(p, n), spec(p, n), spec(p, n),
                  spec(n, p), spec(n, p), spec(p, n), spec(p, n)],
        out_specs=[spec(n, n), spec(n, p), spec(n, p), spec(p, n), spec(p, n),
                   spec(1, p), spec(1, p), spec(1, p), spec(1, p), spec(c, p), spec(c, p)],
        out_shape=[sds((n, n), BF16), sds((n, p), BF16), sds((n, p), BF16), sds((p, n), BF16),
                   sds((p, n), BF16), sds((1, p), F32), sds((1, p), F32), sds((1, p), F32),
                   sds((1, p), F32), sds((c, p), F32), sds((c, p), F32)],
        compiler_params=_cparams("parallel"),
        name="s5_prep",
    )(row(lam_re), row(lam_im), ls_row, col(lam_re), col(lam_im), ls_col,
      bt(b_re), bt(b_im), ct(c_re), ct(c_im))


def _s5_embed(mats, d_skip):
    g, p, c, q = S5_GROUPS, S5_STATE, S5_GROUP, S5_CHUNK
    kmat, bout_re, bout_im, win_re, win_im, aq_re, aq_im = mats
    eye = jnp.eye(g, dtype=BF16)
    n = q * g * c
    kf = jnp.einsum('gicjd,gh->igcjhd', kmat.reshape(g, q, c, q, c), eye).reshape(n, n)
    bo = lambda m: jnp.einsum('gicp,gh->igchp', m.reshape(g, q, c, p), eye).reshape(n, g * p)
    wi = lambda m: jnp.einsum('gpjc,gh->gpjhc', m.reshape(g, p, q, c), eye).reshape(g * p, n)
    d_full = jnp.tile(d_skip.reshape(1, g * c), (1, q))
    return (kf, bo(bout_re), bo(bout_im), wi(win_re), wi(win_im),
            aq_re.reshape(1, g * p), aq_im.reshape(1, g * p), d_full)


def _s5_state_kernel(xb_ref, bout_re_ref, bout_im_ref, aq_re_ref, aq_im_ref, h0_re_ref, h0_im_ref,
                     hin_re_ref, hin_im_ref, hf_re_ref, hf_im_ref, s_re, s_im, e_re, e_im, *, bsz, n_chunks):
    xb = xb_ref[...]
    s_re[...] = jnp.dot(xb, bout_re_ref[...], preferred_element_type=F32)
    s_im[...] = jnp.dot(xb, bout_im_ref[...], preferred_element_type=F32)
    ar, ai = aq_re_ref[...], aq_im_ref[...]

    def body(k, carry):
        new = []
        for b in range(bsz):
            hr, hi = carry[2 * b], carry[2 * b + 1]
            r = b * n_chunks + k
            e_re[pl.ds(r, 1), :] = hr
            e_im[pl.ds(r, 1), :] = hi
            new.append(ar * hr - ai * hi + s_re[pl.ds(r, 1), :])
            new.append(ar * hi + ai * hr + s_im[pl.ds(r, 1), :])
        return tuple(new)

    init = []
    for b in range(bsz):
        init.append(h0_re_ref[b:b + 1, :])
        init.append(h0_im_ref[b:b + 1, :])
    fin = lax.fori_loop(0, n_chunks, body, tuple(init))
    for b in range(bsz):
        hf_re_ref[b:b + 1, :] = fin[2 * b]
        hf_im_ref[b:b + 1, :] = fin[2 * b + 1]
    hin_re_ref[...] = e_re[...].astype(BF16)
    hin_im_ref[...] = e_im[...].astype(BF16)


def _s5_out_kernel(xb_ref, x_ref, kf_ref, hin_re_ref, hin_im_ref, win_re_ref, win_im_ref, d_ref, y_ref):
    y = jnp.dot(xb_ref[...], kf_ref[...], preferred_element_type=F32)
    y = y + jnp.dot(hin_re_ref[...], win_re_ref[...], preferred_element_type=F32)
    y = y + jnp.dot(hin_im_ref[...], win_im_ref[...], preferred_element_type=F32)
    y_ref[...] = y + x_ref[...] * d_ref[...]


def _s5_scan(u, u_bf16, emb, h0_re, h0_im, bsz, seq):
    g, p, c, q = S5_GROUPS, S5_STATE, S5_GROUP, S5_CHUNK
    kf, bout_re, bout_im, win_re, win_im, aq_re, aq_im, d_full = emb
    n, gp = q * g * c, g * p
    n_chunks = seq // q
    rows = bsz * n_chunks
    x = u.reshape(rows, n)
    xb = u_bf16.reshape(rows, n)
    sb = S5_STATE_BLOCK
    col = lambda r, w: pl.BlockSpec((r, w), lambda i: (0, i))
    fix = lambda r, w: pl.BlockSpec((r, w), lambda i: (0, 0))
    hin_re, hin_im, hf_re, hf_im = pl.pallas_call(
        functools.partial(_s5_state_kernel, bsz=bsz, n_chunks=n_chunks),
        grid=(gp // sb,),
        in_specs=[fix(rows, n), col(n, sb), col(n, sb), col(1, sb), col(1, sb), col(bsz, sb), col(bsz, sb)],
        out_specs=[col(rows, sb), col(rows, sb), col(bsz, sb), col(bsz, sb)],
        out_shape=[jax.ShapeDtypeStruct((rows, gp), BF16), jax.ShapeDtypeStruct((rows, gp), BF16),
                   jax.ShapeDtypeStruct((bsz, gp), F32), jax.ShapeDtypeStruct((bsz, gp), F32)],
        scratch_shapes=[pltpu.VMEM((rows, sb), F32)] * 4,
        compiler_params=_cparams("parallel"),
        name="s5_state",
    )(xb, bout_re, bout_im, aq_re, aq_im, h0_re.reshape(bsz, gp), h0_im.reshape(bsz, gp))
    ob = S5_OUT_BLOCK
    y = pl.pallas_call(
        _s5_out_kernel,
        grid=(n // ob,),
        in_specs=[fix(rows, n), col(rows, ob), col(n, ob), fix(rows, gp), fix(rows, gp), col(gp, ob),
                  col(gp, ob), col(1, ob)],
        out_specs=col(rows, ob),
        out_shape=jax.ShapeDtypeStruct((rows, n), F32),
        compiler_params=_cparams("parallel"),
        name="s5_out",
    )(xb, x, kf, hin_re, hin_im, win_re, win_im, d_full)
    return (y.reshape(bsz * seq, g * c), hf_re.reshape(bsz, g, p), hf_im.reshape(bsz, g, p))


def _s5_step_kernel(u_ref, h0_re_ref, h0_im_ref, a_re_ref, a_im_ref, bb_re_ref, bb_im_ref,
                    cc_re_ref, cc_im_ref, d_ref, y_ref, h_re_ref, h_im_ref):
    u = u_ref[...]
    ub = u.astype(BF16)
    a_re, a_im = a_re_ref[...], a_im_ref[...]
    h0r, h0i = h0_re_ref[...], h0_im_ref[...]
    hr = a_re * h0r - a_im * h0i + jnp.dot(ub, bb_re_ref[...], preferred_element_type=F32)
    hi = a_re * h0i + a_im * h0r + jnp.dot(ub, bb_im_ref[...], preferred_element_type=F32)
    h_re_ref[...] = hr
    h_im_ref[...] = hi
    y = (jnp.dot(hr.astype(BF16), cc_re_ref[...], preferred_element_type=F32)
         - jnp.dot(hi.astype(BF16), cc_im_ref[...], preferred_element_type=F32))
    y_ref[...] = y + u * d_ref[...]


def _block_diag(blocks):
    g, r, c = blocks.shape
    eye = jnp.eye(g, dtype=blocks.dtype)
    return (blocks[:, :, None, :] * eye[:, None, :, None]).reshape(g * r, g * c)


def _s5_step(u, h0_re, h0_im, ab_re, ab_im, bb_re, bb_im, c_re, c_im, d_skip):
    n = u.shape[0]
    gp = S5_GROUPS * S5_STATE
    bbd_re = _block_diag(bb_re).astype(BF16)
    bbd_im = _block_diag(bb_im).astype(BF16)
    ccd_re = _block_diag(jnp.swapaxes(c_re, 1, 2)).astype(BF16)
    ccd_im = _block_diag(jnp.swapaxes(c_im, 1, 2)).astype(BF16)
    y, h_re, h_im = pl.pallas_call(
        _s5_step_kernel,
        out_shape=[jax.ShapeDtypeStruct((n, S5_WIDTH), F32),
                   jax.ShapeDtypeStruct((n, gp), F32),
                   jax.ShapeDtypeStruct((n, gp), F32)],
        compiler_params=pltpu.CompilerParams(vmem_limit_bytes=VMEM_LIMIT),
        name="s5_step",
    )(u, h0_re.reshape(n, gp), h0_im.reshape(n, gp), ab_re.reshape(1, gp), ab_im.reshape(1, gp),
      bbd_re, bbd_im, ccd_re, ccd_im, d_skip.reshape(1, S5_WIDTH))
    return y, h_re.reshape(n, S5_GROUPS, S5_STATE), h_im.reshape(n, S5_GROUPS, S5_STATE)


def _rw_prep_kernel(p_ref, tail_ref, prev_ref, mu_ref, w0_ref, w1_ref, w2_ref, a0_ref, a1_ref, a2_ref,
                    g1_ref, g2_ref, kk_ref, ka_ref, rk_ref, hsum_ref,
                    r_out, lw_out, k_out, v_out, ah_out, b_out, g_out, bonus_out, *col_outs, seq_tiles):
    w = RW_WIDTH
    p = p_ref[...]
    if seq_tiles:
        first = pl.program_id(0) % seq_tiles == 0
        before = jnp.where(first, prev_ref[0], tail_ref[7:8, :])
        rowid = lax.broadcasted_iota(jnp.int32, p.shape, 0)
        p_prev = jnp.where(rowid == 0, before, pltpu.roll(p, 1, axis=0))
    else:
        p_prev = prev_ref[...]
    dp = p_prev - p
    r_in, k_in, v_in, z = p[:, :w], p[:, w:2 * w], p[:, 2 * w:3 * w], p[:, 3 * w:]
    dr, dk, dv, dz = dp[:, :w], dp[:, w:2 * w], dp[:, 2 * w:3 * w], dp[:, 3 * w:]
    mu = mu_ref[...]
    r = r_in + dr * mu[0:1]
    xw = z + dz * mu[1:2]
    k = k_in + dk * mu[2:3]
    v = v_in + dv * mu[3:4]
    xa = z + dz * mu[4:5]
    xg = z + dz * mu[5:6]
    t = w0_ref[...] + _bdot(jnp.tanh(_bdot(xw, w1_ref[...])), w2_ref[...])
    nt = -t
    softplus = jnp.maximum(nt, 0.0) + jnp.log(1.0 + jnp.exp(-jnp.abs(nt)))
    w_log = -softplus - 0.5
    lw = -jnp.exp(w_log)
    lw_out[...] = lw
    a = _sigmoid(a0_ref[...] + _bdot(_bdot(xa, a1_ref[...]), a2_ref[...]))
    g_out[...] = _bdot(_sigmoid(_bdot(xg, g1_ref[...])), g2_ref[...])
    kk = k * kk_ref[...]
    ss = _dot2_exact_rhs(kk * kk, hsum_ref[...])
    kk = kk * lax.rsqrt(jnp.maximum(ss, 1e-24))
    k2 = k * (1.0 + (a - 1.0) * ka_ref[...])
    r_out[...] = r
    k_out[...] = k2
    v_out[...] = v
    ah_out[...] = -kk
    b = kk * a
    b_out[...] = b
    bonus_out[...] = _dot2_exact_rhs(r * k2 * rk_ref[...], hsum_ref[...]) * v
    for arr, out in zip((b, k2, lw), col_outs):
        arr_t = arr.T
        for j in range(arr.shape[0] // RW_CHUNK):
            out[0, j] = arr_t[:, j * RW_CHUNK:(j + 1) * RW_CHUNK]


def _head_sum_matrix(width):
    idx = np.arange(width) // HEAD_DIM
    return jnp.asarray((idx[:, None] == idx[None, :]).astype(np.float32), dtype=BF16)


def _rw_prep(p, rw_prev, lp, bsz, seq):
    t = p.shape[0]
    w = RW_WIDTH
    if seq == 1:
        tm, seq_tiles = t, 0
        tail_spec = pl.BlockSpec((t, 4 * w), lambda i: (0, 0))
        prev_spec = pl.BlockSpec((t, 4 * w), lambda i: (0, 0))
        prev_arg = rw_prev
    else:
        tm = _row_tile(seq, 512)
        seq_tiles = seq // tm
        tail_spec = pl.BlockSpec((8, 4 * w), lambda i: (jnp.maximum(i * (tm // 8) - 1, 0), 0))
        prev_spec = pl.BlockSpec((1, 1, 4 * w), lambda i: (i // seq_tiles, 0, 0))
        prev_arg = rw_prev.reshape(bsz, 1, 4 * w)
    row = lambda a: a.reshape(1, w)
    full = lambda a: pl.BlockSpec(a.shape, lambda i: (0,) * a.ndim)
    params = [lp['rw_mu'], row(lp['rw_w0']), lp['rw_w1'].astype(BF16), lp['rw_w2'].astype(BF16),
              row(lp['rw_a0']), lp['rw_a1'].astype(BF16), lp['rw_a2'].astype(BF16),
              lp['rw_g1'].astype(BF16), lp['rw_g2'].astype(BF16), row(lp['rw_k_k']),
              row(lp['rw_k_a']), row(lp['rw_r_k']), _head_sum_matrix(w)]
    tile = lambda n: pl.BlockSpec((tm, n), lambda i: (i, 0))
    out_specs = [tile(w)] * 8
    out_shape = [jax.ShapeDtypeStruct((t, w), F32)] * 8
    if seq_tiles:
        assert tm % RW_CHUNK == 0
        cpt = tm // RW_CHUNK
        out_specs = out_specs + [pl.BlockSpec((1, cpt, w, RW_CHUNK),
                                              lambda i: (i // seq_tiles, i % seq_tiles, 0, 0))] * 3
        out_shape = out_shape + [jax.ShapeDtypeStruct((bsz, seq // RW_CHUNK, w, RW_CHUNK), F32)] * 3
    outs = pl.pallas_call(
        functools.partial(_rw_prep_kernel, seq_tiles=seq_tiles),
        grid=(t // tm,),
        in_specs=[tile(4 * w), tail_spec, prev_spec] + [full(a) for a in params],
        out_specs=out_specs,
        out_shape=out_shape,
        compiler_params=_cparams("parallel"),
        name="rw_prep",
    )(p, p, prev_arg, *params)
    return outs[:8], (tuple(outs[8:]) if seq_tiles else None)


def _rw_scan_kernel(r_ref, ah_ref, lw_ref, v_ref, bt_ref, kt_ref, lwt_ref, h0_ref,
                    y_ref, hf_ref, h_scr):
    c = RW_CHUNK
    nseq = r_ref.shape[0]
    ci = pl.program_id(1)

    @pl.when(ci == 0)
    def _():
        h_scr[...] = h0_ref[...]

    row = lax.broadcasted_iota(jnp.int32, (c, c), 0)
    col = lax.broadcasted_iota(jnp.int32, (c, c), 1)
    incl = row >= col
    strict = row > col
    tri_lo = jnp.where(incl, 1.0, 0.0).astype(BF16)
    tri_up = jnp.where(row <= col, 1.0, 0.0).astype(BF16)
    eye = jnp.where(row == col, 1.0, 0.0)
    d = functools.partial(jnp.dot, preferred_element_type=F32)

    def split3(x):
        hi, mid = _split(x)
        return hi, mid, (x - hi.astype(F32) - mid.astype(F32)).astype(BF16)

    cums, cums_t = [], []
    for s in range(nseq):
        p3 = split3(lw_ref[s])
        cums.append(d(tri_lo, p3[0]) + (d(tri_lo, p3[1]) + d(tri_lo, p3[2])))
        q3 = split3(lwt_ref[s, 0])
        cums_t.append(d(q3[0], tri_up) + (d(q3[1], tri_up) + d(q3[2], tri_up)))

    units = [(s, h) for s in range(nseq) for h in range(RW_HEADS)]
    sl = lambda h: slice(h * HEAD_DIM, (h + 1) * HEAD_DIM)
    each = lambda f: [f(s, h) for s, h in units]
    idx = range(len(units))
    cum = each(lambda s, h: cums[s][:, sl(h)])
    cum_t = each(lambda s, h: cums_t[s][sl(h), :])
    ar = each(lambda s, h: None)
    for i, (s, h) in enumerate(units):
        a_t = ah_ref[s, :, sl(h)] * jnp.exp(cum[i] - lw_ref[s, :, sl(h)])
        r_t = r_ref[s, :, sl(h)] * jnp.exp(cum[i])
        ar[i] = jnp.concatenate([a_t, r_t], axis=0).astype(BF16)
    w_inv_t = [jnp.exp(-cum_t[i]) for i in idx]
    cum_end = [cum_t[i][:, c - 1:c] for i in idx]
    dec_t = [jnp.exp(cum_end[i] - cum_t[i]) for i in idx]
    bt = each(lambda s, h: bt_ref[s, 0, sl(h), :])
    kt = each(lambda s, h: kt_ref[s, 0, sl(h), :])
    b_t = [(bt[i] * w_inv_t[i]).astype(BF16) for i in idx]
    k_t = [(kt[i] * w_inv_t[i]).astype(BF16) for i in idx]
    b_d = [(bt[i] * dec_t[i]).astype(BF16) for i in idx]
    k_d = [(kt[i] * dec_t[i]).astype(BF16) for i in idx]
    mb = [d(ar[i], b_t[i]) for i in idx]
    mk = [d(ar[i], k_t[i]) for i in idx]
    a_ab = [jnp.where(strict, mb[i][:c], 0.0) for i in idx]
    a_ak = [jnp.where(strict, mk[i][:c], 0.0).astype(BF16) for i in idx]
    a_rb = [jnp.where(incl, mb[i][c:], 0.0).astype(BF16) for i in idx]
    a_rk = [jnp.where(incl, mk[i][c:], 0.0).astype(BF16) for i in idx]
    pw = a_ab
    inv = [eye + a_ab[i] for i in idx]
    for _ in range(int(math.log2(c)) - 1):
        pw = [_bdot(pw[i], pw[i]) for i in idx]
        inv = [inv[i] + _bdot(inv[i], pw[i]) for i in idx]
    h0 = each(lambda s, h: h_scr[s, h])
    vb = each(lambda s, h: v_ref[s, :, sl(h)].astype(BF16))
    ar_h = [d(ar[i], h0[i].astype(BF16)) for i in idx]
    rhs = [ar_h[i][:c] + d(a_ak[i], vb[i]) for i in idx]
    ub = [_bdot(inv[i], rhs[i]).astype(BF16) for i in idx]
    y = [ar_h[i][c:] + d(a_rb[i], ub[i]) + d(a_rk[i], vb[i]) for i in idx]
    h_new = [jnp.exp(cum_end[i]) * h0[i] + d(b_d[i], ub[i]) + d(k_d[i], vb[i]) for i in idx]
    for i, (s, h) in enumerate(units):
        h_scr[s, h] = h_new[i]
    for s in range(nseq):
        y_ref[s] = jnp.concatenate([y[s * RW_HEADS + h] for h in range(RW_HEADS)], axis=1)

    for i, (s, h) in enumerate(units):
        hf_ref[s, h] = h_new[i]


def _rw_scan(r, ah, lw, v, b, k2, cols, s0, bsz, seq):
    w, c = RW_WIDTH, RW_CHUNK
    pad = (-seq) % c
    sp = seq + pad

    def rows(a):
        a = a.reshape(bsz, seq, w)
        return jnp.pad(a, ((0, 0), (0, pad), (0, 0))) if pad else a

    r3, ah3, lw3, v3 = (rows(a) for a in (r, ah, lw, v))
    n_chunks = sp // c
    if cols is None:
        tr = lambda a: jnp.swapaxes(rows(a).reshape(bsz, n_chunks, c, w), 2, 3)
        cols = (tr(b), tr(k2), tr(lw))
    h0 = jnp.swapaxes(s0, 2, 3)
    ns = RW_SEQS_PER_STEP
    assert bsz % ns == 0
    rspec = pl.BlockSpec((ns, c, w), lambda i, j: (i, j, 0))
    cspec = pl.BlockSpec((ns, 1, w, c), lambda i, j: (i, j, 0, 0))
    sspec = pl.BlockSpec((ns, RW_HEADS, HEAD_DIM, HEAD_DIM), lambda i, j: (i, 0, 0, 0))
    y, hf = pl.pallas_call(
        _rw_scan_kernel,
        grid=(bsz // ns, n_chunks),
        in_specs=[rspec, rspec, rspec, rspec, cspec, cspec, cspec, sspec],
        out_specs=[rspec, sspec],
        out_shape=[jax.ShapeDtypeStruct((bsz, sp, w), F32),
                   jax.ShapeDtypeStruct((bsz, RW_HEADS, HEAD_DIM, HEAD_DIM), F32)],
        scratch_shapes=[pltpu.VMEM((ns, RW_HEADS, HEAD_DIM, HEAD_DIM), F32)],
        compiler_params=_cparams("parallel", "arbitrary"),
        name="rw_scan",
    )(r3, ah3, lw3, v3, *cols, h0)
    return y[:, :seq].reshape(bsz * seq, w), jnp.swapaxes(hf, 2, 3)


def _rope_kernel(qkv_ref, cos_ref, sin_lo_ref, sin_hi_ref, q_ref, k_ref, v_ref):
    w = ATT_WIDTH
    half = ROPE_DIM // 2
    x = qkv_ref[...]
    rep = lambda t: jnp.concatenate([t] * (w // 128), axis=1)
    cos, s_lo, s_hi = rep(cos_ref[...]), rep(sin_lo_ref[...]), rep(sin_hi_ref[...])

    def rot(t):
        up = pltpu.roll(t, w - half, axis=1)
        dn = pltpu.roll(t, half, axis=1)
        return t * cos + up * s_lo + dn * s_hi

    q_ref[...] = rot(x[:, :w]) * ATT_SCALE
    k_ref[...] = rot(x[:, w:2 * w])
    v_ref[...] = x[:, 2 * w:]


def _rope_tables(pos):
    half = ROPE_DIM // 2
    inv_freq = jnp.exp(-math.log(ROPE_THETA) * jnp.arange(half, dtype=jnp.float32) * (2.0 / ROPE_DIM))
    ang = pos.astype(jnp.float32)[:, None] * inv_freq[None, :]
    cos, sin = jnp.cos(ang), jnp.sin(ang)
    n = pos.shape[0]
    one = jnp.ones((n, HEAD_DIM - ROPE_DIM), F32)
    zero = jnp.zeros((n, HEAD_DIM - ROPE_DIM), F32)
    z8 = jnp.zeros((n, half), F32)
    cos_h = jnp.concatenate([cos, cos, one], axis=1)
    lo_h = jnp.concatenate([-sin, z8, zero], axis=1)
    hi_h = jnp.concatenate([z8, sin, zero], axis=1)
    two = lambda t: jnp.concatenate([t, t], axis=1)
    return two(cos_h), two(lo_h), two(hi_h)


def _rope(qkv, tables):
    t = qkv.shape[0]
    period = tables[0].shape[0]
    tm = _row_tile(period, 512)
    nb = period // tm
    w = ATT_WIDTH
    tspec = pl.BlockSpec((tm, 128), lambda i: (i % nb, 0))
    ospec = pl.BlockSpec((tm, w), lambda i: (i, 0))
    return pl.pallas_call(
        _rope_kernel,
        grid=(t // tm,),
        in_specs=[pl.BlockSpec((tm, 3 * w), lambda i: (i, 0)), tspec, tspec, tspec],
        out_specs=[ospec] * 3,
        out_shape=[jax.ShapeDtypeStruct((t, w), F32)] * 3,
        compiler_params=_cparams("parallel"),
        name="rope",
    )(qkv, *tables)


def _multiplicity(dist):
    m = np.zeros(dist.shape, np.float64)
    for window, dil in DILATED_PATTERNS:
        m += ((dist >= 0) & (dist <= window) & (dist % dil == 0))
    return m


def _dist_bias(dist):
    m = _multiplicity(dist)
    return np.where(m > 0, np.log(np.maximum(m, 1.0)), NEG_INF).astype(np.float32)


def _attn_kernel(q_ref, kc_ref, kp_ref, vc_ref, vp_ref, o_ref, kwin, vwin, acc_scr, m_scr, l_scr):
    tile, sub = ATT_TILE, ATT_SUB
    first = pl.program_id(2) == 0
    kwin[0:tile] = kp_ref[0]
    kwin[tile:] = kc_ref[0]
    vwin[0:tile] = vp_ref[0]
    vwin[tile:] = vc_ref[0]
    row = lax.broadcasted_iota(jnp.int32, (sub, 2 * sub), 0)
    col = lax.broadcasted_iota(jnp.int32, (sub, 2 * sub), 1)
    bias = jnp.where(col < sub, jnp.where(col >= row, 0.0, NEG_INF), jnp.where(col - sub <= row, 0.0, NEG_INF))
    bias_first = jnp.where(col < sub, NEG_INF, bias)
    head0 = lax.broadcasted_iota(jnp.int32, (sub, 2 * HEAD_DIM), 1) < HEAD_DIM
    nt = (((1,), (1,)), ((), ()))
    for p, (window, dil) in enumerate(DILATED_PATTERNS):
        assert window == dil * sub
        nsub = tile // dil // sub
        for rho in range(dil):
            for a in range(nsub):
                q_rows = pl.ds(rho + dil * sub * a, sub, stride=dil) if dil > 1 else pl.ds(sub * a, sub)
                k_start = tile + rho + dil * sub * (a - 1)
                k_rows = pl.ds(k_start, 2 * sub, stride=dil) if dil > 1 else pl.ds(k_start, 2 * sub)
                q = q_ref[0, q_rows, :]
                k = kwin[k_rows, :].astype(BF16)
                v = vwin[k_rows, :].astype(BF16)
                b = jnp.where(first, bias_first, bias) if a == 0 else bias
                stats = []
                for hh in range(2):
                    qh = jnp.where(head0 if hh == 0 else jnp.logical_not(head0), q, 0.0).astype(BF16)
                    s = lax.dot_general(qh, k, nt, preferred_element_type=F32) + b
                    m = jnp.max(s, axis=1, keepdims=True)
                    e = jnp.exp(s - m)
                    l = jnp.sum(e, axis=1, keepdims=True)
                    acc = jnp.dot(e.astype(BF16), v, preferred_element_type=F32)
                    stats.append((m, l, acc))
                pick = lambda i: jnp.where(head0, stats[0][i], stats[1][i])
                m_scr[p, q_rows, :] = pick(0)
                l_scr[p, q_rows, :] = pick(1)
                acc_scr[p, q_rows, :] = pick(2)
    ms = [m_scr[p] for p in range(len(DILATED_PATTERNS))]
    m = functools.reduce(jnp.maximum, ms)
    ws = [jnp.exp(mp - m) for mp in ms]
    den = sum(w * l_scr[p] for p, w in enumerate(ws))
    num = sum(w * acc_scr[p] for p, w in enumerate(ws))
    o_ref[0] = num / den


def _attn_prompt(q, k, v, bsz, seq):
    tile = ATT_TILE
    assert seq % tile == 0 and tile == WIN_MAX
    lanes = 2 * HEAD_DIM
    npair = ATT_WIDTH // lanes
    npat = len(DILATED_PATTERNS)
    r3 = lambda a: a.reshape(bsz, seq, ATT_WIDTH)
    cur = pl.BlockSpec((1, tile, lanes), lambda b, h, i: (b, i, h))
    prev = pl.BlockSpec((1, tile, lanes), lambda b, h, i: (b, jnp.maximum(i - 1, 0), h))
    o = pl.pallas_call(
        _attn_kernel,
        grid=(bsz, npair, seq // tile),
        in_specs=[cur, cur, prev, cur, prev],
        out_specs=cur,
        out_shape=jax.ShapeDtypeStruct((bsz, seq, ATT_WIDTH), F32),
        scratch_shapes=[pltpu.VMEM((2 * tile, lanes), F32), pltpu.VMEM((2 * tile, lanes), F32),
                        pltpu.VMEM((npat, tile, lanes), F32), pltpu.VMEM((npat, tile, lanes), F32),
                        pltpu.VMEM((npat, tile, lanes), F32)],
        compiler_params=_cparams("parallel", "parallel", "arbitrary"),
        name="attn_prompt",
    )(r3(q), r3(k), r3(k), r3(v), r3(v))
    return o.reshape(bsz * seq, ATT_WIDTH)


def _attn_step_kernel(q_ref, kn_ref, vn_ref, kc_ref, vc_ref, bias_ref, sel_ref, selt_ref, o_ref):
    q = q_ref[0]
    kn, vn = kn_ref[0], vn_ref[0]
    kc, vc = kc_ref[0], vc_ref[0]
    sel, selt = sel_ref[...], selt_ref[...]
    s_c = _dot2_exact_rhs(kc * q, sel) + bias_ref[...]
    s_n = _dot2_exact_rhs(kn * q, sel) + math.log(len(DILATED_PATTERNS))
    m = jnp.maximum(jnp.max(s_c, axis=0, keepdims=True), s_n)
    p_c = jnp.exp(s_c - m)
    p_n = jnp.exp(s_n - m)
    den = jnp.sum(p_c, axis=0, keepdims=True) + p_n
    pe_c = _dot2_exact_rhs(p_c, selt)
    pe_n = _dot2_exact_rhs(p_n, selt)
    den_e = _dot2_exact_rhs(den, selt)
    o_ref[0] = (jnp.sum(pe_c * vc, axis=0, keepdims=True) + pe_n * vn) / den_e


def _attn_step(q, k_new, v_new, k_buf, v_buf):
    n, n_buf = k_buf.shape[0], k_buf.shape[1]
    w = ATT_WIDTH
    dist = n_buf - np.arange(n_buf)
    bias = np.broadcast_to(_dist_bias(dist)[:, None], (n_buf, 128)).copy()
    head = np.arange(w) // HEAD_DIM
    sel = (head[:, None] == np.arange(128)[None, :]).astype(np.float32)
    vec = pl.BlockSpec((1, 1, w), lambda i: (i, 0, 0))
    buf = pl.BlockSpec((1, n_buf, w), lambda i: (i, 0, 0))
    full = lambda s: pl.BlockSpec(s, lambda i: (0, 0))
    o = pl.pallas_call(
        _attn_step_kernel,
        grid=(n,),
        in_specs=[vec, vec, vec, buf, buf, full((n_buf, 128)), full((w, 128)), full((128, w))],
        out_specs=vec,
        out_shape=jax.ShapeDtypeStruct((n, 1, w), F32),
        compiler_params=_cparams("parallel"),
        name="attn_step",
    )(q.reshape(n, 1, w), k_new.reshape(n, 1, w), v_new.reshape(n, 1, w),
      k_buf.reshape(n, n_buf, w), v_buf.reshape(n, n_buf, w), jnp.asarray(bias),
      jnp.asarray(sel, dtype=BF16), jnp.asarray(sel.T, dtype=BF16))
    return o.reshape(n, w)


def _gelu_tanh(x):
    return 0.5 * x * (1.0 + jnp.tanh(math.sqrt(2.0 / math.pi) * (x + 0.044715 * (x * x * x))))


def _mix_kernel(x_ref, ys_ref, yr_ref, bonus_ref, g_ref, ya_ref, wglu_ref, bglu_ref, gng_ref, gnb_ref,
                havg_ref, wout_ref, lng_ref, lnb_ref, o_ref, *, alpha):
    ys = _gelu_tanh(ys_ref[...])
    ya = ys * _sigmoid(_bdot(ys, wglu_ref[...]) + bglu_ref[...])
    yr = yr_ref[...]
    havg = havg_ref[...]
    mean = _dot2_exact_rhs(yr, havg) * (1.0 / HEAD_DIM)
    yc = yr - mean
    var = _dot2_exact_rhs(yc * yc, havg) * (1.0 / HEAD_DIM)
    yb = (yc * lax.rsqrt(var + RW_GN_EPS) * gng_ref[...] + gnb_ref[...] + bonus_ref[...]) * g_ref[...]
    wout = wout_ref[...]
    o1, o2 = S5_WIDTH, S5_WIDTH + RW_WIDTH
    mix = _bdot(ya, wout[:o1]) + _bdot(yb, wout[o1:o2]) + _bdot(ya_ref[...], wout[o2:])
    o_ref[...] = _layer_norm(alpha * x_ref[...] + mix, lng_ref[...], lnb_ref[...])


def _mix(x, y_s5, y_rw, bonus, gate, y_att, lp, alpha):
    t = x.shape[0]
    tm = _row_tile(t, 512)
    tile = lambda n: pl.BlockSpec((tm, n), lambda i: (i, 0))
    full = lambda a: pl.BlockSpec(a.shape, lambda i: (0,) * a.ndim)
    params = [lp['s5_w_glu'].astype(BF16), lp['s5_b_glu'].reshape(1, -1), lp['rw_gn_g'].reshape(1, -1),
              lp['rw_gn_b'].reshape(1, -1), _head_sum_matrix(RW_WIDTH), lp['w_out'].astype(BF16),
              lp['ln_g'][0].reshape(1, -1), lp['ln_b'][0].reshape(1, -1)]
    return pl.pallas_call(
        functools.partial(_mix_kernel, alpha=alpha),
        grid=(t // tm,),
        in_specs=[tile(D_MODEL), tile(S5_WIDTH), tile(RW_WIDTH), tile(RW_WIDTH), tile(RW_WIDTH),
                  tile(ATT_WIDTH)] + [full(a) for a in params],
        out_specs=tile(D_MODEL),
        out_shape=jax.ShapeDtypeStruct((t, D_MODEL), F32),
        compiler_params=_cparams("parallel"),
        name="mix",
    )(x, y_s5, y_rw, bonus, gate, y_att, *params)


def _rank_select(vals, n_rows, keep):
    ridx = lax.broadcasted_iota(jnp.int32, vals.shape, 0)
    cnt = jnp.zeros(vals.shape, jnp.int32)
    for j in range(n_rows):
        vj = vals[j:j + 1, :]
        beats = jnp.where(vj > vals, 1, jnp.where(vj == vals, jnp.where(ridx > j, 1, 0), 0))
        cnt = cnt + beats
    return cnt < keep


def _router_kernel(x_ref, wt_ref, bias_ref, gates_ref, gscore_scr, ekeep_scr):
    e, ng = N_EXPERTS, N_EXPERT_GROUPS
    per = e // ng
    logits = lax.dot_general(wt_ref[...].astype(BF16), x_ref[...].astype(BF16), (((1,), (1,)), ((), ())),
                             preferred_element_type=F32)
    scores = _sigmoid(logits)
    sel = scores + bias_ref[:, 0:1]
    t = sel.shape[1]
    pos = lax.broadcasted_iota(jnp.int32, (per, t), 0)
    for gi in range(ng):
        grp = sel[gi * per:(gi + 1) * per]
        m1 = jnp.max(grp, axis=0, keepdims=True)
        first = jnp.min(jnp.where(grp == m1, pos, per), axis=0, keepdims=True)
        m2 = jnp.max(jnp.where(pos == first, NEG_INF, grp), axis=0, keepdims=True)
        gscore_scr[gi:gi + 1, :] = m1 + m2
    gkeep = jnp.where(_rank_select(gscore_scr[...], ng, TOPK_GROUPS), 1.0, 0.0)
    for gi in range(ng):
        ekeep_scr[gi * per:(gi + 1) * per, :] = jnp.broadcast_to(gkeep[gi:gi + 1], (per, t))
    masked = jnp.where(ekeep_scr[...] > 0.0, sel, NEG_INF)
    chosen = _rank_select(masked, e, TOP_K)
    w = jnp.where(chosen, scores, 0.0)
    gates_ref[...] = w / jnp.sum(w, axis=0, keepdims=True) * ROUTED_SCALE


def _router(x, w_router, router_bias):
    t = x.shape[0]
    tm = _row_tile(t, 512)
    e = N_EXPERTS
    gates_t = pl.pallas_call(
        _router_kernel,
        grid=(t // tm,),
        in_specs=[pl.BlockSpec((tm, D_MODEL), lambda i: (i, 0)),
                  pl.BlockSpec((e, D_MODEL), lambda i: (0, 0)),
                  pl.BlockSpec((e, 128), lambda i: (0, 0))],
        out_specs=pl.BlockSpec((e, tm), lambda i: (0, i)),
        out_shape=jax.ShapeDtypeStruct((e, t), F32),
        scratch_shapes=[pltpu.VMEM((N_EXPERT_GROUPS, tm), F32), pltpu.VMEM((e, tm), F32)],
        compiler_params=_cparams("parallel"),
        name="router",
    )(x, w_router.T, jnp.broadcast_to(router_bias.reshape(e, 1), (e, 128)))
    return gates_t.T


def _moe_kernel(x_ref, gates_ref, wg_ref, wu_ref, wd_ref, sg_ref, su_ref, sd_ref, lng_ref, lnb_ref,
                o_ref, xb_scr, gsplit_scr, acc_scr, *, alpha):
    e = pl.program_id(1)
    ne = pl.num_programs(1)

    @pl.when(e == 0)
    def _():
        xb = x_ref[...].astype(BF16)
        xb_scr[...] = xb
        hi, lo = _split(gates_ref[...])
        gsplit_scr[...] = jnp.concatenate([hi, lo], axis=1)
        h = _silu(jnp.dot(xb, sg_ref[0].astype(BF16), preferred_element_type=F32)) * jnp.dot(
            xb, su_ref[0].astype(BF16), preferred_element_type=F32)
        acc_scr[...] = jnp.dot(h.astype(BF16), sd_ref[0].astype(BF16), preferred_element_type=F32)

    xb = xb_scr[...]
    h = _silu(jnp.dot(xb, wg_ref[0, 0].astype(BF16), preferred_element_type=F32)) * jnp.dot(
        xb, wu_ref[0, 0].astype(BF16), preferred_element_type=F32)
    pick = lax.broadcasted_iota(jnp.int32, (2 * N_EXPERTS, EXPERT_FF), 0) % N_EXPERTS == e
    gate = jnp.dot(gsplit_scr[...], jnp.where(pick, 1.0, 0.0).astype(BF16), preferred_element_type=F32)
    gate = jnp.concatenate([gate] * (D_MODEL // EXPERT_FF), axis=1)
    acc_scr[...] += jnp.dot(h.astype(BF16), wd_ref[0, 0].astype(BF16), preferred_element_type=F32) * gate

    @pl.when(e == ne - 1)
    def _():
        o_ref[...] = _layer_norm(alpha * x_ref[...] + acc_scr[...], lng_ref[...], lnb_ref[...])


def _moe(x, gates, w, layer, alpha):
    t = x.shape[0]
    tm = _row_tile(t, 1024)
    d, f, e = D_MODEL, EXPERT_FF, N_EXPERTS
    full = lambda a: pl.BlockSpec(a.shape, lambda i, j: (0,) * a.ndim)
    lng, lnb = w['ln_g'][layer, 1].reshape(1, -1), w['ln_b'][layer, 1].reshape(1, -1)
    shared = lambda r, c: pl.BlockSpec((1, r, c), lambda i, j: (layer, 0, 0))
    return pl.pallas_call(
        functools.partial(_moe_kernel, alpha=alpha),
        grid=(t // tm, e),
        in_specs=[pl.BlockSpec((tm, d), lambda i, j: (i, 0)),
                  pl.BlockSpec((tm, e), lambda i, j: (i, 0)),
                  pl.BlockSpec((1, 1, d, f), lambda i, j: (layer, j, 0, 0)),
                  pl.BlockSpec((1, 1, d, f), lambda i, j: (layer, j, 0, 0)),
                  pl.BlockSpec((1, 1, f, d), lambda i, j: (layer, j, 0, 0)),
                  shared(d, f), shared(d, f), shared(f, d), full(lng), full(lnb)],
        out_specs=pl.BlockSpec((tm, d), lambda i, j: (i, 0)),
        out_shape=jax.ShapeDtypeStruct((t, d), F32),
        scratch_shapes=[pltpu.VMEM((tm, d), BF16), pltpu.VMEM((tm, 2 * e), BF16), pltpu.VMEM((tm, d), F32)],
        compiler_params=_cparams("parallel", "arbitrary"),
        name="moe",
    )(x, gates, w['expert_w_gate'], w['expert_w_up'], w['expert_w_down'],
      w['shared_w_gate'], w['shared_w_up'], w['shared_w_down'], lng, lnb)


def _cache_shift_kernel(c_ref, n_ref, o_ref):
    n = c_ref.shape[2]
    o_ref[0, 0, 0:n - 1] = c_ref[0, 0, 1:n]
    o_ref[0, 0, n - 1:n] = n_ref[0, 0]


def _cache_shift(cache, new_rows):
    depth, n, buf, hh, dd = cache.shape
    blk = pl.BlockSpec((1, 1, buf, hh, dd), lambda l, b: (l, b, 0, 0, 0))
    row = pl.BlockSpec((1, 1, 1, hh, dd), lambda l, b: (l, b, 0, 0, 0))
    return pl.pallas_call(
        _cache_shift_kernel,
        grid=(depth, n),
        in_specs=[blk, row],
        out_specs=blk,
        out_shape=jax.ShapeDtypeStruct(cache.shape, cache.dtype),
        compiler_params=_cparams("parallel", "parallel"),
        name="cache_shift",
    )(cache, new_rows)


def _trunk_layer(x, bsz, seq, pos0, h0_re, h0_im, rw_s0, rw_prev, k_buf, v_buf, lp, s5p, rope_tables, alpha):
    t = bsz * seq
    u, u_bf16, p_rw, qkv = _proj(x, lp['w_in_bf16'])
    if seq == 1:
        ab_re, ab_im, bb_re, bb_im = s5p[7:]
        y_s5, h_re, h_im = _s5_step(u, h0_re, h0_im, ab_re, ab_im, bb_re, bb_im,
                                    lp['s5_c_re'], lp['s5_c_im'], lp['s5_d'])
    else:
        y_s5, h_re, h_im = _s5_scan(u, u_bf16, lp['s5_emb'], h0_re, h0_im, bsz, seq)
    (r, lw, k2, v, ah, b, gate, bonus), cols = _rw_prep(p_rw, rw_prev, lp, bsz, seq)
    y_rw, rw_s = _rw_scan(r, ah, lw, v, b, k2, cols, rw_s0, bsz, seq)
    rw_row = p_rw.reshape(bsz, seq, 4 * RW_WIDTH)[:, -1]
    q, k, vv = _rope(qkv, rope_tables)
    if seq == 1:
        y_att = _attn_step(q, k, vv, k_buf, v_buf)
    else:
        assert k_buf is None
        y_att = _attn_prompt(q, k, vv, bsz, seq)
    keep = min(WIN_MAX, seq)
    k_new = k.reshape(bsz, seq, ATT_HEADS, HEAD_DIM)[:, seq - keep:]
    v_new = vv.reshape(bsz, seq, ATT_HEADS, HEAD_DIM)[:, seq - keep:]
    x1 = _mix(x, y_s5, y_rw, bonus, gate, y_att, lp, alpha)
    gates = _router(x1, lp['w_router'], lp['router_bias'])
    x2 = _moe(x1, gates, lp['all'], lp['layer'], alpha)
    return x2, (h_re, h_im, rw_s, rw_row, k_new, v_new)


def kernel(x_prompt, x_sample, state_s5_re, state_s5_im, state_rwkv, state_rwkv_shift, cache_attn_k, cache_attn_v, w_in, s5_lambda_re, s5_lambda_im, s5_b_re, s5_b_im, s5_c_re, s5_c_im, s5_d, s5_log_step, s5_w_glu, s5_b_glu, rw_mu, rw_w0, rw_w1, rw_w2, rw_a0, rw_a1, rw_a2, rw_g1, rw_g2, rw_k_k, rw_k_a, rw_r_k, rw_gn_g, rw_gn_b, w_out, ln_g, ln_b, w_router, router_bias, expert_w_gate, expert_w_up, expert_w_down, shared_w_gate, shared_w_up, shared_w_down):
    depth = w_in.shape[0]
    alpha = (2 * depth) ** 0.25
    bsz, seq, d = x_prompt.shape
    dbsz, dseq, _ = x_sample.shape
    past = cache_attn_k.shape[2]
    names = dict(s5_c_re=s5_c_re, s5_c_im=s5_c_im, s5_d=s5_d, s5_w_glu=s5_w_glu, s5_b_glu=s5_b_glu,
                 rw_mu=rw_mu, rw_w0=rw_w0, rw_w1=rw_w1, rw_w2=rw_w2, rw_a0=rw_a0, rw_a1=rw_a1, rw_a2=rw_a2,
                 rw_g1=rw_g1, rw_g2=rw_g2, rw_k_k=rw_k_k, rw_k_a=rw_k_a, rw_r_k=rw_r_k, rw_gn_g=rw_gn_g,
                 rw_gn_b=rw_gn_b, w_out=w_out, ln_g=ln_g, ln_b=ln_b, w_router=w_router,
                 router_bias=router_bias, expert_w_gate=expert_w_gate, expert_w_up=expert_w_up,
                 expert_w_down=expert_w_down, shared_w_gate=shared_w_gate, shared_w_up=shared_w_up,
                 shared_w_down=shared_w_down)
    assert dseq == 1
    tables_p = _rope_tables(jnp.arange(seq))
    tables_s = _rope_tables(jnp.full((dbsz,), PAST_LEN))
    yp = x_prompt.reshape(bsz * seq, d)
    ys = x_sample.reshape(dbsz * dseq, d)
    new_p, new_s = [], []
    for l in range(depth):
        lp = {k: v[l] for k, v in names.items() if not k.startswith(('expert_w', 'shared_w'))}
        lp['all'], lp['layer'] = names, l
        lp['w_in_bf16'] = w_in[l].astype(BF16)
        s5p = _s5_prep(s5_lambda_re[l], s5_lambda_im[l], s5_log_step[l], s5_b_re[l], s5_b_im[l],
                       s5_c_re[l], s5_c_im[l])
        lp['s5_emb'] = _s5_embed(s5p[:7], s5_d[l])
        zs = jnp.zeros((bsz, S5_GROUPS, S5_STATE), F32)
        z_s = jnp.zeros((bsz, RW_HEADS, HEAD_DIM, HEAD_DIM), F32)
        zrow = jnp.zeros((bsz, 4 * RW_WIDTH), F32)
        yp, st_p = _trunk_layer(yp, bsz, seq, 0, zs, zs, z_s, zrow, None, None, lp, s5p, tables_p, alpha)
        ys, st_s = _trunk_layer(ys, dbsz, dseq, past, state_s5_re[l], state_s5_im[l], state_rwkv[l],
                                state_rwkv_shift[l], cache_attn_k[l], cache_attn_v[l], lp, s5p, tables_s, alpha)
        new_p.append(st_p)
        new_s.append(st_s)
    stack = lambda sts: [jnp.stack([s[i] for s in sts], 0) for i in range(6)]
    out_p, out_s = stack(new_p), stack(new_s)
    assert past == WIN_MAX
    out_s[4] = _cache_shift(cache_attn_k, out_s[4])
    out_s[5] = _cache_shift(cache_attn_v, out_s[5])
    return (yp.reshape(bsz, seq, d), ys.reshape(dbsz, dseq, d), *out_p, *out_s)
```

```python
import functools
import math

import numpy as np
import jax
import jax.numpy as jnp
from jax import lax
from jax.experimental import pallas as pl
from jax.experimental.pallas import tpu as pltpu

F32 = jnp.float32
BF16 = jnp.bfloat16

D_MODEL = 1024
PAST_LEN = 8192
HEAD_DIM = 64
S5_WIDTH = 256
S5_GROUP = 16
S5_GROUPS = 16
S5_STATE = 64
RW_WIDTH = 384
RW_HEADS = 6
ATT_WIDTH = 384
ATT_HEADS = 6
N_IN = S5_WIDTH + 4 * RW_WIDTH + 3 * ATT_WIDTH
RW_GN_EPS = 64e-5
DILATED_PATTERNS = ((128, 1), (512, 4), (2048, 16))
WIN_MAX = 2048
ATT_SCALE = HEAD_DIM ** -0.5
ROPE_THETA = 500000.0
ROPE_DIM = HEAD_DIM // 4
NEG_INF = -1e30
N_EXPERTS = 64
N_EXPERT_GROUPS = 8
TOPK_GROUPS = 4
TOP_K = 8
EXPERT_FF = 256
ROUTED_SCALE = 2.5
LN_EPS = 1e-5

S5_CHUNK = 16
S5_STATE_BLOCK = 256
S5_OUT_BLOCK = 512
RW_CHUNK = 64
RW_SEQS_PER_STEP = 2
ATT_TILE = 2048
ATT_SUB = 128
VMEM_LIMIT = 56 * 1024 * 1024


def _cparams(*sem):
    return pltpu.CompilerParams(dimension_semantics=sem, vmem_limit_bytes=VMEM_LIMIT)


def _bdot(a, b):
    return jnp.dot(a.astype(BF16), b.astype(BF16), preferred_element_type=F32)


def _split(x):
    hi = x.astype(BF16)
    lo = (x - hi.astype(F32)).astype(BF16)
    return hi, lo


def _dot3(a, b):
    ah, al = _split(a)
    bh, bl = _split(b)
    d = functools.partial(jnp.dot, preferred_element_type=F32)
    return d(ah, bh) + (d(ah, bl) + d(al, bh))


def _dot2_exact_rhs(a, b_bf16):
    ah, al = _split(a)
    d = functools.partial(jnp.dot, preferred_element_type=F32)
    return d(ah, b_bf16) + d(al, b_bf16)


def _sigmoid(x):
    return 1.0 / (1.0 + jnp.exp(-x))


def _silu(x):
    return x * _sigmoid(x)


def _layer_norm(x, g, b):
    mu = jnp.mean(x, axis=-1, keepdims=True)
    xc = x - mu
    var = jnp.mean(xc * xc, axis=-1, keepdims=True)
    return xc * lax.rsqrt(var + LN_EPS) * g + b


def _row_tile(t, pref):
    tm = min(t, pref)
    assert t % tm == 0
    return tm


def _proj_kernel(x_ref, w_ref, u_ref, ub_ref, rw_ref, qkv_ref):
    p = jnp.dot(x_ref[...].astype(BF16), w_ref[...], preferred_element_type=F32)
    o1 = S5_WIDTH
    o2 = o1 + 4 * RW_WIDTH
    u_ref[...] = p[:, :o1]
    ub_ref[...] = p[:, :o1].astype(BF16)
    rw_ref[...] = p[:, o1:o2]
    qkv_ref[...] = p[:, o2:]


def _proj(x, w_bf16):
    t = x.shape[0]
    tm = _row_tile(t, 512)
    return pl.pallas_call(
        _proj_kernel,
        grid=(t // tm,),
        in_specs=[pl.BlockSpec((tm, D_MODEL), lambda i: (i, 0)),
                  pl.BlockSpec((D_MODEL, N_IN), lambda i: (0, 0))],
        out_specs=[pl.BlockSpec((tm, S5_WIDTH), lambda i: (i, 0)),
                   pl.BlockSpec((tm, S5_WIDTH), lambda i: (i, 0)),
                   pl.BlockSpec((tm, 4 * RW_WIDTH), lambda i: (i, 0)),
                   pl.BlockSpec((tm, 3 * ATT_WIDTH), lambda i: (i, 0))],
        out_shape=[jax.ShapeDtypeStruct((t, S5_WIDTH), F32),
                   jax.ShapeDtypeStruct((t, S5_WIDTH), BF16),
                   jax.ShapeDtypeStruct((t, 4 * RW_WIDTH), F32),
                   jax.ShapeDtypeStruct((t, 3 * ATT_WIDTH), F32)],
        compiler_params=_cparams("parallel"),
        name="proj",
    )(x, w_bf16)


def _s5_discretize(lr, li, log_step):
    dt = jnp.exp(log_step)
    mag = jnp.exp(lr * dt)
    ab_re = mag * jnp.cos(li * dt)
    ab_im = mag * jnp.sin(li * dt)
    den = lr * lr + li * li
    cf_re = ((ab_re - 1.0) * lr + ab_im * li) / den
    cf_im = (ab_im * lr - (ab_re - 1.0) * li) / den
    return dt, ab_re, ab_im, cf_re, cf_im


def _s5_power(lr, li, dt, n):
    mag = jnp.exp(n * (lr * dt))
    ang = n * (li * dt)
    return mag * jnp.cos(ang), mag * jnp.sin(ang)


def _s5_prep_kernel(lr_ref, li_ref, ls_ref, lrc_ref, lic_ref, lsc_ref, bt_re_ref, bt_im_ref,
                    ct_re_ref, ct_im_ref,
                    kmat_ref, bout_re_ref, bout_im_ref, win_re_ref, win_im_ref,
                    aq_re_ref, aq_im_ref, ab_re_ref, ab_im_ref, bb_re_ref, bb_im_ref):
    q = S5_CHUNK
    n_rows = q * S5_GROUP
    lr, li, ls = lr_ref[0], li_ref[0], ls_ref[0]
    dt, ab_re, ab_im, cf_re, cf_im = _s5_discretize(lr, li, ls)
    ab_re_ref[0] = ab_re
    ab_im_ref[0] = ab_im
    aq_re, aq_im = _s5_power(lr, li, dt, jnp.float32(q))
    aq_re_ref[0] = aq_re
    aq_im_ref[0] = aq_im
    bt_re, bt_im = bt_re_ref[0], bt_im_ref[0]
    bb_re = cf_re * bt_re - cf_im * bt_im
    bb_im = cf_re * bt_im + cf_im * bt_re
    bb_re_ref[0] = bb_re[:S5_GROUP]
    bb_im_ref[0] = bb_im[:S5_GROUP]
    step = lax.broadcasted_iota(jnp.int32, (n_rows, S5_STATE), 0) // S5_GROUP
    pw_re, pw_im = _s5_power(lr, li, dt, (q - 1 - step).astype(F32))
    bout_re_ref[0] = (bb_re * pw_re - bb_im * pw_im).astype(BF16)
    bout_im_ref[0] = (bb_re * pw_im + bb_im * pw_re).astype(BF16)
    lrc, lic, lsc = lrc_ref[0], lic_ref[0], lsc_ref[0]
    dtc, abc_re, abc_im, _, _ = _s5_discretize(lrc, lic, lsc)
    lag = (lax.broadcasted_iota(jnp.int32, (S5_STATE, n_rows), 1) // S5_GROUP).astype(F32)
    p0_re, p0_im = _s5_power(lrc, lic, dtc, lag)
    ct_re, ct_im = ct_re_ref[0], ct_im_ref[0]
    e0_re = ct_re * p0_re - ct_im * p0_im
    e0_im = ct_re * p0_im + ct_im * p0_re
    e1_re = e0_re * abc_re - e0_im * abc_im
    e1_im = e0_re * abc_im + e0_im * abc_re
    win_re_ref[0] = e1_re.astype(BF16)
    win_im_ref[0] = (-e1_im).astype(BF16)
    g = _dot3(bb_re[:S5_GROUP], e0_re) - _dot3(bb_im[:S5_GROUP], e0_im)
    lane = lax.broadcasted_iota(jnp.int32, (S5_GROUP, n_rows), 1)
    blocks = []
    for i in range(q):
        if i == 0:
            blocks.append(g)
        else:
            shifted = pltpu.roll(g, S5_GROUP * i, axis=1)
            blocks.append(jnp.where(lane >= S5_GROUP * i, shifted, 0.0))
    kmat_ref[0] = jnp.concatenate(blocks, axis=0).astype(BF16)


def _s5_prep(lam_re, lam_im, log_step, b_re, b_im, c_re, c_im):
    g, p, c, q = S5_GROUPS, S5_STATE, S5_GROUP, S5_CHUNK
    n = q * c
    row = lambda a: a.reshape(g, 1, p)
    col = lambda a: jnp.broadcast_to(a.reshape(g, p, 1), (g, p, n))
    ls_row = jnp.broadcast_to(log_step.reshape(g, 1, 1), (g, 1, p))
    ls_col = jnp.broadcast_to(log_step.reshape(g, 1, 1), (g, p, n))
    bt = lambda a: jnp.tile(jnp.swapaxes(a, 1, 2), (1, q, 1))
    ct = lambda a: jnp.tile(jnp.swapaxes(a, 1, 2), (1, 1, q))
    spec = lambda *s: pl.BlockSpec((1,) + s, lambda i: (i, 0, 0))
    sds = lambda s, d: jax.ShapeDtypeStruct((g,) + s, d)
    return pl.pallas_call(
        _s5_prep_kernel,
        grid=(g,),
        in_specs=[spec(1, p), spec(1, p), spec(1, p), spec(p, n), spec(p, n), spec(p, n),
                  spec(n, p), spec(n, p), spec(p, n), spec(p, n)],
        out_specs=[spec(n, n), spec(n, p), spec(n, p), spec(p, n), spec(p, n),
                   spec(1, p), spec(1, p), spec(1, p), spec(1, p), spec(c, p), spec(c, p)],
        out_shape=[sds((n, n), BF16), sds((n, p), BF16), sds((n, p), BF16), sds((p, n), BF16),
                   sds((p, n), BF16), sds((1, p), F32), sds((1, p), F32), sds((1, p), F32),
                   sds((1, p), F32), sds((c, p), F32), sds((c, p), F32)],
        compiler_params=_cparams("parallel"),
        name="s5_prep",
    )(row(lam_re), row(lam_im), ls_row, col(lam_re), col(lam_im), ls_col,
      bt(b_re), bt(b_im), ct(c_re), ct(c_im))


def _s5_embed(mats, d_skip):
    g, p, c, q = S5_GROUPS, S5_STATE, S5_GROUP, S5_CHUNK
    kmat, bout_re, bout_im, win_re, win_im, aq_re, aq_im = mats
    n = q * g * c
    by_step = lambda m: jnp.swapaxes(m.reshape(g, q, c, m.shape[-1]), 0, 1).reshape(n, m.shape[-1])
    src = jnp.arange(q * c)
    dst = jnp.arange(n)
    spread = ((src[:, None] // c == dst[None, :] // (g * c)) & (src[:, None] % c == dst[None, :] % c)).astype(BF16)
    over_groups = lambda m: jnp.dot(m, spread, preferred_element_type=BF16)
    row_group = lambda rows, per: (jnp.arange(rows) // per) % g
    col_group_out = (dst // c) % g
    col_group_state = jnp.arange(g * p) // p
    rows_in = row_group(n, c)
    rows_state = jnp.arange(g * p) // p
    keep = lambda m, rg, cg: jnp.where(rg[:, None] == cg[None, :], m, jnp.zeros((), m.dtype))
    kf = keep(over_groups(by_step(kmat)), rows_in, col_group_out)
    bo = lambda m: keep(jnp.tile(by_step(m), (1, g)), rows_in, col_group_state)
    wi = lambda m: keep(over_groups(m.reshape(g * p, q * c)), rows_state, col_group_out)
    d_full = jnp.tile(d_skip.reshape(1, g * c), (1, q))
    return (kf, bo(bout_re), bo(bout_im), wi(win_re), wi(win_im),
            aq_re.reshape(1, g * p), aq_im.reshape(1, g * p), d_full)


def _s5_state_kernel(xb_ref, bout_re_ref, bout_im_ref, aq_re_ref, aq_im_ref, h0_re_ref, h0_im_ref,
                     hin_re_ref, hin_im_ref, hf_re_ref, hf_im_ref, s_re, s_im, e_re, e_im, *, bsz, n_chunks):
    xb = xb_ref[...]
    s_re[...] = jnp.dot(xb, bout_re_ref[...], preferred_element_type=F32)
    s_im[...] = jnp.dot(xb, bout_im_ref[...], preferred_element_type=F32)
    ar, ai = aq_re_ref[...], aq_im_ref[...]

    def body(k, carry):
        new = []
        for b in range(bsz):
            hr, hi = carry[2 * b], carry[2 * b + 1]
            r = b * n_chunks + k
            e_re[pl.ds(r, 1), :] = hr
            e_im[pl.ds(r, 1), :] = hi
            new.append(ar * hr - ai * hi + s_re[pl.ds(r, 1), :])
            new.append(ar * hi + ai * hr + s_im[pl.ds(r, 1), :])
        return tuple(new)

    init = []
    for b in range(bsz):
        init.append(h0_re_ref[b:b + 1, :])
        init.append(h0_im_ref[b:b + 1, :])
    fin = lax.fori_loop(0, n_chunks, body, tuple(init))
    for b in range(bsz):
        hf_re_ref[b:b + 1, :] = fin[2 * b]
        hf_im_ref[b:b + 1, :] = fin[2 * b + 1]
    hin_re_ref[...] = e_re[...].astype(BF16)
    hin_im_ref[...] = e_im[...].astype(BF16)


def _s5_out_kernel(xb_ref, x_ref, kf_ref, hin_re_ref, hin_im_ref, win_re_ref, win_im_ref, d_ref, y_ref):
    y = jnp.dot(xb_ref[...], kf_ref[...], preferred_element_type=F32)
    y = y + jnp.dot(hin_re_ref[...], win_re_ref[...], preferred_element_type=F32)
    y = y + jnp.dot(hin_im_ref[...], win_im_ref[...], preferred_element_type=F32)
    y_ref[...] = y + x_ref[...] * d_ref[...]


def _s5_scan(u, u_bf16, emb, h0_re, h0_im, bsz, seq):
    g, p, c, q = S5_GROUPS, S5_STATE, S5_GROUP, S5_CHUNK
    kf, bout_re, bout_im, win_re, win_im, aq_re, aq_im, d_full = emb
    n, gp = q * g * c, g * p
    n_chunks = seq // q
    rows = bsz * n_chunks
    x = u.reshape(rows, n)
    xb = u_bf16.reshape(rows, n)
    sb = S5_STATE_BLOCK
    col = lambda r, w: pl.BlockSpec((r, w), lambda i: (0, i))
    fix = lambda r, w: pl.BlockSpec((r, w), lambda i: (0, 0))
    hin_re, hin_im, hf_re, hf_im = pl.pallas_call(
        functools.partial(_s5_state_kernel, bsz=bsz, n_chunks=n_chunks),
        grid=(gp // sb,),
        in_specs=[fix(rows, n), col(n, sb), col(n, sb), col(1, sb), col(1, sb), col(bsz, sb), col(bsz, sb)],
        out_specs=[col(rows, sb), col(rows, sb), col(bsz, sb), col(bsz, sb)],
        out_shape=[jax.ShapeDtypeStruct((rows, gp), BF16), jax.ShapeDtypeStruct((rows, gp), BF16),
                   jax.ShapeDtypeStruct((bsz, gp), F32), jax.ShapeDtypeStruct((bsz, gp), F32)],
        scratch_shapes=[pltpu.VMEM((rows, sb), F32)] * 4,
        compiler_params=_cparams("parallel"),
        name="s5_state",
    )(xb, bout_re, bout_im, aq_re, aq_im, h0_re.reshape(bsz, gp), h0_im.reshape(bsz, gp))
    ob = S5_OUT_BLOCK
    y = pl.pallas_call(
        _s5_out_kernel,
        grid=(n // ob,),
        in_specs=[fix(rows, n), col(rows, ob), col(n, ob), fix(rows, gp), fix(rows, gp), col(gp, ob),
                  col(gp, ob), col(1, ob)],
        out_specs=col(rows, ob),
        out_shape=jax.ShapeDtypeStruct((rows, n), F32),
        compiler_params=_cparams("parallel"),
        name="s5_out",
    )(xb, x, kf, hin_re, hin_im, win_re, win_im, d_full)
    return (y.reshape(bsz * seq, g * c), hf_re.reshape(bsz, g, p), hf_im.reshape(bsz, g, p))


def _s5_step_kernel(u_ref, h0_re_ref, h0_im_ref, a_re_ref, a_im_ref, bb_re_ref, bb_im_ref,
                    cc_re_ref, cc_im_ref, d_ref, y_ref, h_re_ref, h_im_ref):
    u = u_ref[...]
    ub = u.astype(BF16)
    a_re, a_im = a_re_ref[...], a_im_ref[...]
    h0r, h0i = h0_re_ref[...], h0_im_ref[...]
    hr = a_re * h0r - a_im * h0i + jnp.dot(ub, bb_re_ref[...], preferred_element_type=F32)
    hi = a_re * h0i + a_im * h0r + jnp.dot(ub, bb_im_ref[...], preferred_element_type=F32)
    h_re_ref[...] = hr
    h_im_ref[...] = hi
    y = (jnp.dot(hr.astype(BF16), cc_re_ref[...], preferred_element_type=F32)
         - jnp.dot(hi.astype(BF16), cc_im_ref[...], preferred_element_type=F32))
    y_ref[...] = y + u * d_ref[...]


def _block_diag(blocks):
    g, r, c = blocks.shape
    eye = jnp.eye(g, dtype=blocks.dtype)
    return (blocks[:, :, None, :] * eye[:, None, :, None]).reshape(g * r, g * c)


def _s5_step(u, h0_re, h0_im, ab_re, ab_im, bb_re, bb_im, c_re, c_im, d_skip):
    n = u.shape[0]
    gp = S5_GROUPS * S5_STATE
    bbd_re = _block_diag(bb_re).astype(BF16)
    bbd_im = _block_diag(bb_im).astype(BF16)
    ccd_re = _block_diag(jnp.swapaxes(c_re, 1, 2)).astype(BF16)
    ccd_im = _block_diag(jnp.swapaxes(c_im, 1, 2)).astype(BF16)
    y, h_re, h_im = pl.pallas_call(
        _s5_step_kernel,
        out_shape=[jax.ShapeDtypeStruct((n, S5_WIDTH), F32),
                   jax.ShapeDtypeStruct((n, gp), F32),
                   jax.ShapeDtypeStruct((n, gp), F32)],
        compiler_params=pltpu.CompilerParams(vmem_limit_bytes=VMEM_LIMIT),
        name="s5_step",
    )(u, h0_re.reshape(n, gp), h0_im.reshape(n, gp), ab_re.reshape(1, gp), ab_im.reshape(1, gp),
      bbd_re, bbd_im, ccd_re, ccd_im, d_skip.reshape(1, S5_WIDTH))
    return y, h_re.reshape(n, S5_GROUPS, S5_STATE), h_im.reshape(n, S5_GROUPS, S5_STATE)


def _rw_prep_kernel(p_ref, tail_ref, prev_ref, mu_ref, w0_ref, w1_ref, w2_ref, a0_ref, a1_ref, a2_ref,
                    g1_ref, g2_ref, kk_ref, ka_ref, rk_ref, hsum_ref,
                    r_out, lw_out, k_out, v_out, ah_out, b_out, g_out, bonus_out, *col_outs, seq_tiles):
    w = RW_WIDTH
    p = p_ref[...]
    if seq_tiles:
        first = pl.program_id(0) % seq_tiles == 0
        before = jnp.where(first, prev_ref[0], tail_ref[7:8, :])
        rowid = lax.broadcasted_iota(jnp.int32, p.shape, 0)
        p_prev = jnp.where(rowid == 0, before, pltpu.roll(p, 1, axis=0))
    else:
        p_prev = prev_ref[...]
    dp = p_prev - p
    r_in, k_in, v_in, z = p[:, :w], p[:, w:2 * w], p[:, 2 * w:3 * w], p[:, 3 * w:]
    dr, dk, dv, dz = dp[:, :w], dp[:, w:2 * w], dp[:, 2 * w:3 * w], dp[:, 3 * w:]
    mu = mu_ref[...]
    r = r_in + dr * mu[0:1]
    xw = z + dz * mu[1:2]
    k = k_in + dk * mu[2:3]
    v = v_in + dv * mu[3:4]
    xa = z + dz * mu[4:5]
    xg = z + dz * mu[5:6]
    t = w0_ref[...] + _bdot(jnp.tanh(_bdot(xw, w1_ref[...])), w2_ref[...])
    nt = -t
    softplus = jnp.maximum(nt, 0.0) + jnp.log(1.0 + jnp.exp(-jnp.abs(nt)))
    w_log = -softplus - 0.5
    lw = -jnp.exp(w_log)
    lw_out[...] = lw
    a = _sigmoid(a0_ref[...] + _bdot(_bdot(xa, a1_ref[...]), a2_ref[...]))
    g_out[...] = _bdot(_sigmoid(_bdot(xg, g1_ref[...])), g2_ref[...])
    kk = k * kk_ref[...]
    ss = _dot2_exact_rhs(kk * kk, hsum_ref[...])
    kk = kk * lax.rsqrt(jnp.maximum(ss, 1e-24))
    k2 = k * (1.0 + (a - 1.0) * ka_ref[...])
    r_out[...] = r
    k_out[...] = k2
    v_out[...] = v
    ah_out[...] = -kk
    b = kk * a
    b_out[...] = b
    bonus_out[...] = _dot2_exact_rhs(r * k2 * rk_ref[...], hsum_ref[...]) * v
    for arr, out in zip((b, k2, lw), col_outs):
        arr_t = arr.T
        for j in range(arr.shape[0] // RW_CHUNK):
            out[0, j] = arr_t[:, j * RW_CHUNK:(j + 1) * RW_CHUNK]


def _head_sum_matrix(width):
    idx = np.arange(width) // HEAD_DIM
    return jnp.asarray((idx[:, None] == idx[None, :]).astype(np.float32), dtype=BF16)


def _rw_prep(p, rw_prev, lp, bsz, seq):
    t = p.shape[0]
    w = RW_WIDTH
    if seq == 1:
        tm, seq_tiles = t, 0
        tail_spec = pl.BlockSpec((t, 4 * w), lambda i: (0, 0))
        prev_spec = pl.BlockSpec((t, 4 * w), lambda i: (0, 0))
        prev_arg = rw_prev
    else:
        tm = _row_tile(seq, 512)
        seq_tiles = seq // tm
        tail_spec = pl.BlockSpec((8, 4 * w), lambda i: (jnp.maximum(i * (tm // 8) - 1, 0), 0))
        prev_spec = pl.BlockSpec((1, 1, 4 * w), lambda i: (i // seq_tiles, 0, 0))
        prev_arg = rw_prev.reshape(bsz, 1, 4 * w)
    row = lambda a: a.reshape(1, w)
    full = lambda a: pl.BlockSpec(a.shape, lambda i: (0,) * a.ndim)
    params = [lp['rw_mu'], row(lp['rw_w0']), lp['rw_w1'].astype(BF16), lp['rw_w2'].astype(BF16),
              row(lp['rw_a0']), lp['rw_a1'].astype(BF16), lp['rw_a2'].astype(BF16),
              lp['rw_g1'].astype(BF16), lp['rw_g2'].astype(BF16), row(lp['rw_k_k']),
              row(lp['rw_k_a']), row(lp['rw_r_k']), _head_sum_matrix(w)]
    tile = lambda n: pl.BlockSpec((tm, n), lambda i: (i, 0))
    out_specs = [tile(w)] * 8
    out_shape = [jax.ShapeDtypeStruct((t, w), F32)] * 8
    if seq_tiles:
        assert tm % RW_CHUNK == 0
        cpt = tm // RW_CHUNK
        out_specs = out_specs + [pl.BlockSpec((1, cpt, w, RW_CHUNK),
                                              lambda i: (i // seq_tiles, i % seq_tiles, 0, 0))] * 3
        out_shape = out_shape + [jax.ShapeDtypeStruct((bsz, seq // RW_CHUNK, w, RW_CHUNK), F32)] * 3
    outs = pl.pallas_call(
        functools.partial(_rw_prep_kernel, seq_tiles=seq_tiles),
        grid=(t // tm,),
        in_specs=[tile(4 * w), tail_spec, prev_spec] + [full(a) for a in params],
        out_specs=out_specs,
        out_shape=out_shape,
        compiler_params=_cparams("parallel"),
        name="rw_prep",
    )(p, p, prev_arg, *params)
    return outs[:8], (tuple(outs[8:]) if seq_tiles else None)


def _rw_scan_kernel(r_ref, ah_ref, lw_ref, v_ref, bt_ref, kt_ref, lwt_ref, h0_ref,
                    y_ref, hf_ref, h_scr):
    c = RW_CHUNK
    nseq = r_ref.shape[0]
    ci = pl.program_id(1)

    @pl.when(ci == 0)
    def _():
        h_scr[...] = h0_ref[...]

    row = lax.broadcasted_iota(jnp.int32, (c, c), 0)
    col = lax.broadcasted_iota(jnp.int32, (c, c), 1)
    incl = row >= col
    strict = row > col
    tri_lo = jnp.where(incl, 1.0, 0.0).astype(BF16)
    tri_up = jnp.where(row <= col, 1.0, 0.0).astype(BF16)
    eye = jnp.where(row == col, 1.0, 0.0)
    d = functools.partial(jnp.dot, preferred_element_type=F32)

    def split3(x):
        hi, mid = _split(x)
        return hi, mid, (x - hi.astype(F32) - mid.astype(F32)).astype(BF16)

    cums, cums_t = [], []
    for s in range(nseq):
        p3 = split3(lw_ref[s])
        cums.append(d(tri_lo, p3[0]) + (d(tri_lo, p3[1]) + d(tri_lo, p3[2])))
        q3 = split3(lwt_ref[s, 0])
        cums_t.append(d(q3[0], tri_up) + (d(q3[1], tri_up) + d(q3[2], tri_up)))

    units = [(s, h) for s in range(nseq) for h in range(RW_HEADS)]
    sl = lambda h: slice(h * HEAD_DIM, (h + 1) * HEAD_DIM)
    each = lambda f: [f(s, h) for s, h in units]
    idx = range(len(units))
    cum = each(lambda s, h: cums[s][:, sl(h)])
    cum_t = each(lambda s, h: cums_t[s][sl(h), :])
    ar = each(lambda s, h: None)
    for i, (s, h) in enumerate(units):
        a_t = ah_ref[s, :, sl(h)] * jnp.exp(cum[i] - lw_ref[s, :, sl(h)])
        r_t = r_ref[s, :, sl(h)] * jnp.exp(cum[i])
        ar[i] = jnp.concatenate([a_t, r_t], axis=0).astype(BF16)
    w_inv_t = [jnp.exp(-cum_t[i]) for i in idx]
    cum_end = [cum_t[i][:, c - 1:c] for i in idx]
    dec_t = [jnp.exp(cum_end[i] - cum_t[i]) for i in idx]
    bt = each(lambda s, h: bt_ref[s, 0, sl(h), :])
    kt = each(lambda s, h: kt_ref[s, 0, sl(h), :])
    b_t = [(bt[i] * w_inv_t[i]).astype(BF16) for i in idx]
    k_t = [(kt[i] * w_inv_t[i]).astype(BF16) for i in idx]
    b_d = [(bt[i] * dec_t[i]).astype(BF16) for i in idx]
    k_d = [(kt[i] * dec_t[i]).astype(BF16) for i in idx]
    mb = [d(ar[i], b_t[i]) for i in idx]
    mk = [d(ar[i], k_t[i]) for i in idx]
    a_ab = [jnp.where(strict, mb[i][:c], 0.0) for i in idx]
    a_ak = [jnp.where(strict, mk[i][:c], 0.0).astype(BF16) for i in idx]
    a_rb = [jnp.where(incl, mb[i][c:], 0.0).astype(BF16) for i in idx]
    a_rk = [jnp.where(incl, mk[i][c:], 0.0).astype(BF16) for i in idx]
    pw = a_ab
    inv = [eye + a_ab[i] for i in idx]
    for _ in range(int(math.log2(c)) - 1):
        pw = [_bdot(pw[i], pw[i]) for i in idx]
        inv = [inv[i] + _bdot(inv[i], pw[i]) for i in idx]
    h0 = each(lambda s, h: h_scr[s, h])
    vb = each(lambda s, h: v_ref[s, :, sl(h)].astype(BF16))
    ar_h = [d(ar[i], h0[i].astype(BF16)) for i in idx]
    rhs = [ar_h[i][:c] + d(a_ak[i], vb[i]) for i in idx]
    ub = [_bdot(inv[i], rhs[i]).astype(BF16) for i in idx]
    y = [ar_h[i][c:] + d(a_rb[i], ub[i]) + d(a_rk[i], vb[i]) for i in idx]
    h_new = [jnp.exp(cum_end[i]) * h0[i] + d(b_d[i], ub[i]) + d(k_d[i], vb[i]) for i in idx]
    for i, (s, h) in enumerate(units):
        h_scr[s, h] = h_new[i]
    for s in range(nseq):
        y_ref[s] = jnp.concatenate([y[s * RW_HEADS + h] for h in range(RW_HEADS)], axis=1)

    for i, (s, h) in enumerate(units):
        hf_ref[s, h] = h_new[i]


def _rw_scan(r, ah, lw, v, b, k2, cols, s0, bsz, seq):
    w, c = RW_WIDTH, RW_CHUNK
    pad = (-seq) % c
    sp = seq + pad

    def rows(a):
        a = a.reshape(bsz, seq, w)
        return jnp.pad(a, ((0, 0), (0, pad), (0, 0))) if pad else a

    r3, ah3, lw3, v3 = (rows(a) for a in (r, ah, lw, v))
    n_chunks = sp // c
    if cols is None:
        tr = lambda a: jnp.swapaxes(rows(a).reshape(bsz, n_chunks, c, w), 2, 3)
        cols = (tr(b), tr(k2), tr(lw))
    h0 = jnp.swapaxes(s0, 2, 3)
    ns = RW_SEQS_PER_STEP
    assert bsz % ns == 0
    rspec = pl.BlockSpec((ns, c, w), lambda i, j: (i, j, 0))
    cspec = pl.BlockSpec((ns, 1, w, c), lambda i, j: (i, j, 0, 0))
    sspec = pl.BlockSpec((ns, RW_HEADS, HEAD_DIM, HEAD_DIM), lambda i, j: (i, 0, 0, 0))
    y, hf = pl.pallas_call(
        _rw_scan_kernel,
        grid=(bsz // ns, n_chunks),
        in_specs=[rspec, rspec, rspec, rspec, cspec, cspec, cspec, sspec],
        out_specs=[rspec, sspec],
        out_shape=[jax.ShapeDtypeStruct((bsz, sp, w), F32),
                   jax.ShapeDtypeStruct((bsz, RW_HEADS, HEAD_DIM, HEAD_DIM), F32)],
        scratch_shapes=[pltpu.VMEM((ns, RW_HEADS, HEAD_DIM, HEAD_DIM), F32)],
        compiler_params=_cparams("parallel", "arbitrary"),
        name="rw_scan",
    )(r3, ah3, lw3, v3, *cols, h0)
    return y[:, :seq].reshape(bsz * seq, w), jnp.swapaxes(hf, 2, 3)


def _rope_kernel(qkv_ref, cos_ref, sin_lo_ref, sin_hi_ref, q_ref, k_ref, v_ref):
    w = ATT_WIDTH
    half = ROPE_DIM // 2
    x = qkv_ref[...]
    rep = lambda t: jnp.concatenate([t] * (w // 128), axis=1)
    cos, s_lo, s_hi = rep(cos_ref[...]), rep(sin_lo_ref[...]), rep(sin_hi_ref[...])

    def rot(t):
        up = pltpu.roll(t, w - half, axis=1)
        dn = pltpu.roll(t, half, axis=1)
        return t * cos + up * s_lo + dn * s_hi

    q_ref[...] = rot(x[:, :w]) * ATT_SCALE
    k_ref[...] = rot(x[:, w:2 * w])
    v_ref[...] = x[:, 2 * w:]


def _rope_tables(pos):
    half = ROPE_DIM // 2
    inv_freq = jnp.exp(-math.log(ROPE_THETA) * jnp.arange(half, dtype=jnp.float32) * (2.0 / ROPE_DIM))
    ang = pos.astype(jnp.float32)[:, None] * inv_freq[None, :]
    cos, sin = jnp.cos(ang), jnp.sin(ang)
    n = pos.shape[0]
    one = jnp.ones((n, HEAD_DIM - ROPE_DIM), F32)
    zero = jnp.zeros((n, HEAD_DIM - ROPE_DIM), F32)
    z8 = jnp.zeros((n, half), F32)
    cos_h = jnp.concatenate([cos, cos, one], axis=1)
    lo_h = jnp.concatenate([-sin, z8, zero], axis=1)
    hi_h = jnp.concatenate([z8, sin, zero], axis=1)
    two = lambda t: jnp.concatenate([t, t], axis=1)
    return two(cos_h), two(lo_h), two(hi_h)


def _rope(qkv, tables):
    t = qkv.shape[0]
    period = tables[0].shape[0]
    tm = _row_tile(period, 512)
    nb = period // tm
    w = ATT_WIDTH
    tspec = pl.BlockSpec((tm, 128), lambda i: (i % nb, 0))
    ospec = pl.BlockSpec((tm, w), lambda i: (i, 0))
    return pl.pallas_call(
        _rope_kernel,
        grid=(t // tm,),
        in_specs=[pl.BlockSpec((tm, 3 * w), lambda i: (i, 0)), tspec, tspec, tspec],
        out_specs=[ospec] * 3,
        out_shape=[jax.ShapeDtypeStruct((t, w), F32)] * 3,
        compiler_params=_cparams("parallel"),
        name="rope",
    )(qkv, *tables)


def _multiplicity(dist):
    m = np.zeros(dist.shape, np.float64)
    for window, dil in DILATED_PATTERNS:
        m += ((dist >= 0) & (dist <= window) & (dist % dil == 0))
    return m


def _dist_bias(dist):
    m = _multiplicity(dist)
    return np.where(m > 0, np.log(np.maximum(m, 1.0)), NEG_INF).astype(np.float32)


def _attn_kernel(q_ref, kc_ref, kp_ref, vc_ref, vp_ref, o_ref, kwin, vwin, acc_scr, m_scr, l_scr):
    tile, sub = ATT_TILE, ATT_SUB
    first = pl.program_id(2) == 0
    kwin[0:tile] = kp_ref[0]
    kwin[tile:] = kc_ref[0]
    vwin[0:tile] = vp_ref[0]
    vwin[tile:] = vc_ref[0]
    row = lax.broadcasted_iota(jnp.int32, (sub, 2 * sub), 0)
    col = lax.broadcasted_iota(jnp.int32, (sub, 2 * sub), 1)
    bias = jnp.where(col < sub, jnp.where(col >= row, 0.0, NEG_INF), jnp.where(col - sub <= row, 0.0, NEG_INF))
    bias_first = jnp.where(col < sub, NEG_INF, bias)
    head0 = lax.broadcasted_iota(jnp.int32, (sub, 2 * HEAD_DIM), 1) < HEAD_DIM
    nt = (((1,), (1,)), ((), ()))
    for p, (window, dil) in enumerate(DILATED_PATTERNS):
        assert window == dil * sub
        nsub = tile // dil // sub
        for rho in range(dil):
            for a in range(nsub):
                q_rows = pl.ds(rho + dil * sub * a, sub, stride=dil) if dil > 1 else pl.ds(sub * a, sub)
                k_start = tile + rho + dil * sub * (a - 1)
                k_rows = pl.ds(k_start, 2 * sub, stride=dil) if dil > 1 else pl.ds(k_start, 2 * sub)
                q = q_ref[0, q_rows, :]
                k = kwin[k_rows, :].astype(BF16)
                v = vwin[k_rows, :].astype(BF16)
                b = jnp.where(first, bias_first, bias) if a == 0 else bias
                stats = []
                for hh in range(2):
                    qh = jnp.where(head0 if hh == 0 else jnp.logical_not(head0), q, 0.0).astype(BF16)
                    s = lax.dot_general(qh, k, nt, preferred_element_type=F32) + b
                    m = jnp.max(s, axis=1, keepdims=True)
                    e = jnp.exp(s - m)
                    l = jnp.sum(e, axis=1, keepdims=True)
                    acc = jnp.dot(e.astype(BF16), v, preferred_element_type=F32)
                    stats.append((m, l, acc))
                pick = lambda i: jnp.where(head0, stats[0][i], stats[1][i])
                m_scr[p, q_rows, :] = pick(0)
                l_scr[p, q_rows, :] = pick(1)
                acc_scr[p, q_rows, :] = pick(2)
    ms = [m_scr[p] for p in range(len(DILATED_PATTERNS))]
    m = functools.reduce(jnp.maximum, ms)
    ws = [jnp.exp(mp - m) for mp in ms]
    den = sum(w * l_scr[p] for p, w in enumerate(ws))
    num = sum(w * acc_scr[p] for p, w in enumerate(ws))
    o_ref[0] = num / den


def _attn_prompt(q, k, v, bsz, seq):
    tile = ATT_TILE
    assert seq % tile == 0 and tile == WIN_MAX
    lanes = 2 * HEAD_DIM
    npair = ATT_WIDTH // lanes
    npat = len(DILATED_PATTERNS)
    r3 = lambda a: a.reshape(bsz, seq, ATT_WIDTH)
    cur = pl.BlockSpec((1, tile, lanes), lambda b, h, i: (b, i, h))
    prev = pl.BlockSpec((1, tile, lanes), lambda b, h, i: (b, jnp.maximum(i - 1, 0), h))
    o = pl.pallas_call(
        _attn_kernel,
        grid=(bsz, npair, seq // tile),
        in_specs=[cur, cur, prev, cur, prev],
        out_specs=cur,
        out_shape=jax.ShapeDtypeStruct((bsz, seq, ATT_WIDTH), F32),
        scratch_shapes=[pltpu.VMEM((2 * tile, lanes), F32), pltpu.VMEM((2 * tile, lanes), F32),
                        pltpu.VMEM((npat, tile, lanes), F32), pltpu.VMEM((npat, tile, lanes), F32),
                        pltpu.VMEM((npat, tile, lanes), F32)],
        compiler_params=_cparams("parallel", "parallel", "arbitrary"),
        name="attn_prompt",
    )(r3(q), r3(k), r3(k), r3(v), r3(v))
    return o.reshape(bsz * seq, ATT_WIDTH)


def _attn_step_kernel(q_ref, kn_ref, vn_ref, kc_ref, vc_ref, bias_ref, o_ref):
    w = ATT_WIDTH
    q = q_ref[0]
    kn, vn = kn_ref[0], vn_ref[0]
    n_buf = kc_ref.shape[-1]
    kc = kc_ref[0, 0].reshape(w, n_buf).astype(BF16)
    vc = vc_ref[0, 0].reshape(w, n_buf).astype(BF16)
    hrow = lax.broadcasted_iota(jnp.int32, (8, w), 0)
    hcol = lax.broadcasted_iota(jnp.int32, (8, w), 1) // HEAD_DIM
    own = hrow == hcol
    qh = jnp.where(own, q, 0.0).astype(BF16)
    s_c = jnp.dot(qh, kc, preferred_element_type=F32) + bias_ref[...]
    s_n = jnp.sum(qh.astype(F32) * kn.astype(BF16).astype(F32), axis=1, keepdims=True) + math.log(
        len(DILATED_PATTERNS))
    m = jnp.maximum(jnp.max(s_c, axis=1, keepdims=True), s_n)
    p_c = jnp.exp(s_c - m)
    p_n = jnp.exp(s_n - m)
    den = jnp.sum(p_c, axis=1, keepdims=True) + p_n
    o_all = lax.dot_general(p_c.astype(BF16), vc, (((1,), (1,)), ((), ())), preferred_element_type=F32)
    o_all = (o_all + p_n * vn) / den
    o_ref[0] = jnp.sum(jnp.where(own, o_all, 0.0), axis=0, keepdims=True)


def _attn_step(q, k_new, v_new, k_cache, v_cache, layer):
    n, n_buf = k_cache.shape[1], k_cache.shape[-1]
    w = ATT_WIDTH
    dist = n_buf - np.arange(n_buf)
    bias = _dist_bias(dist)[None, :]
    vec = pl.BlockSpec((1, 1, w), lambda i: (i, 0, 0))
    buf = pl.BlockSpec((1, 1, ATT_HEADS, HEAD_DIM, n_buf), lambda i: (layer, i, 0, 0, 0))
    o = pl.pallas_call(
        _attn_step_kernel,
        grid=(n,),
        in_specs=[vec, vec, vec, buf, buf, pl.BlockSpec((1, n_buf), lambda i: (0, 0))],
        out_specs=vec,
        out_shape=jax.ShapeDtypeStruct((n, 1, w), F32),
        compiler_params=_cparams("parallel"),
        name="attn_step",
    )(q.reshape(n, 1, w), k_new.reshape(n, 1, w), v_new.reshape(n, 1, w), k_cache, v_cache, jnp.asarray(bias))
    return o.reshape(n, w)


def _gelu_tanh(x):
    return 0.5 * x * (1.0 + jnp.tanh(math.sqrt(2.0 / math.pi) * (x + 0.044715 * (x * x * x))))


def _mix_kernel(x_ref, ys_ref, yr_ref, bonus_ref, g_ref, ya_ref, wglu_ref, bglu_ref, gng_ref, gnb_ref,
                havg_ref, wout_ref, lng_ref, lnb_ref, o_ref, *, alpha):
    ys = _gelu_tanh(ys_ref[...])
    ya = ys * _sigmoid(_bdot(ys, wglu_ref[...]) + bglu_ref[...])
    yr = yr_ref[...]
    havg = havg_ref[...]
    mean = _dot2_exact_rhs(yr, havg) * (1.0 / HEAD_DIM)
    yc = yr - mean
    var = _dot2_exact_rhs(yc * yc, havg) * (1.0 / HEAD_DIM)
    yb = (yc * lax.rsqrt(var + RW_GN_EPS) * gng_ref[...] + gnb_ref[...] + bonus_ref[...]) * g_ref[...]
    wout = wout_ref[...]
    o1, o2 = S5_WIDTH, S5_WIDTH + RW_WIDTH
    mix = _bdot(ya, wout[:o1]) + _bdot(yb, wout[o1:o2]) + _bdot(ya_ref[...], wout[o2:])
    o_ref[...] = _layer_norm(alpha * x_ref[...] + mix, lng_ref[...], lnb_ref[...])


def _mix(x, y_s5, y_rw, bonus, gate, y_att, lp, alpha):
    t = x.shape[0]
    tm = _row_tile(t, 512)
    tile = lambda n: pl.BlockSpec((tm, n), lambda i: (i, 0))
    full = lambda a: pl.BlockSpec(a.shape, lambda i: (0,) * a.ndim)
    params = [lp['s5_w_glu'].astype(BF16), lp['s5_b_glu'].reshape(1, -1), lp['rw_gn_g'].reshape(1, -1),
              lp['rw_gn_b'].reshape(1, -1), _head_sum_matrix(RW_WIDTH), lp['w_out'].astype(BF16),
              lp['ln_g'][0].reshape(1, -1), lp['ln_b'][0].reshape(1, -1)]
    return pl.pallas_call(
        functools.partial(_mix_kernel, alpha=alpha),
        grid=(t // tm,),
        in_specs=[tile(D_MODEL), tile(S5_WIDTH), tile(RW_WIDTH), tile(RW_WIDTH), tile(RW_WIDTH),
                  tile(ATT_WIDTH)] + [full(a) for a in params],
        out_specs=tile(D_MODEL),
        out_shape=jax.ShapeDtypeStruct((t, D_MODEL), F32),
        compiler_params=_cparams("parallel"),
        name="mix",
    )(x, y_s5, y_rw, bonus, gate, y_att, *params)


def _rank_select(vals, n_rows, keep):
    ridx = lax.broadcasted_iota(jnp.int32, vals.shape, 0)
    cnt = jnp.zeros(vals.shape, jnp.int32)
    for j in range(n_rows):
        vj = vals[j:j + 1, :]
        beats = jnp.where(vj > vals, 1, jnp.where(vj == vals, jnp.where(ridx > j, 1, 0), 0))
        cnt = cnt + beats
    return cnt < keep


def _router_kernel(x_ref, wt_ref, bias_ref, gates_ref, gscore_scr, ekeep_scr):
    e, ng = N_EXPERTS, N_EXPERT_GROUPS
    per = e // ng
    logits = lax.dot_general(wt_ref[...].astype(BF16), x_ref[...].astype(BF16), (((1,), (1,)), ((), ())),
                             preferred_element_type=F32)
    scores = _sigmoid(logits)
    sel = scores + bias_ref[:, 0:1]
    t = sel.shape[1]
    pos = lax.broadcasted_iota(jnp.int32, (per, t), 0)
    for gi in range(ng):
        grp = sel[gi * per:(gi + 1) * per]
        m1 = jnp.max(grp, axis=0, keepdims=True)
        first = jnp.min(jnp.where(grp == m1, pos, per), axis=0, keepdims=True)
        m2 = jnp.max(jnp.where(pos == first, NEG_INF, grp), axis=0, keepdims=True)
        gscore_scr[gi:gi + 1, :] = m1 + m2
    gkeep = jnp.where(_rank_select(gscore_scr[...], ng, TOPK_GROUPS), 1.0, 0.0)
    for gi in range(ng):
        ekeep_scr[gi * per:(gi + 1) * per, :] = jnp.broadcast_to(gkeep[gi:gi + 1], (per, t))
    masked = jnp.where(ekeep_scr[...] > 0.0, sel, NEG_INF)
    chosen = _rank_select(masked, e, TOP_K)
    w = jnp.where(chosen, scores, 0.0)
    gates_ref[...] = w / jnp.sum(w, axis=0, keepdims=True) * ROUTED_SCALE


def _router(x, w_router, router_bias):
    t = x.shape[0]
    tm = _row_tile(t, 512)
    e = N_EXPERTS
    gates_t = pl.pallas_call(
        _router_kernel,
        grid=(t // tm,),
        in_specs=[pl.BlockSpec((tm, D_MODEL), lambda i: (i, 0)),
                  pl.BlockSpec((e, D_MODEL), lambda i: (0, 0)),
                  pl.BlockSpec((e, 128), lambda i: (0, 0))],
        out_specs=pl.BlockSpec((e, tm), lambda i: (0, i)),
        out_shape=jax.ShapeDtypeStruct((e, t), F32),
        scratch_shapes=[pltpu.VMEM((N_EXPERT_GROUPS, tm), F32), pltpu.VMEM((e, tm), F32)],
        compiler_params=_cparams("parallel"),
        name="router",
    )(x, w_router.T, jnp.broadcast_to(router_bias.reshape(e, 1), (e, 128)))
    return gates_t.T


def _moe_kernel(x_ref, gates_ref, wg_ref, wu_ref, wd_ref, sg_ref, su_ref, sd_ref, lng_ref, lnb_ref,
                o_ref, xb_scr, gsplit_scr, acc_scr, *, alpha):
    e = pl.program_id(1)
    ne = pl.num_programs(1)

    @pl.when(e == 0)
    def _():
        xb = x_ref[...].astype(BF16)
        xb_scr[...] = xb
        hi, lo = _split(gates_ref[...])
        gsplit_scr[...] = jnp.concatenate([hi, lo], axis=1)
        h = _silu(jnp.dot(xb, sg_ref[0].astype(BF16), preferred_element_type=F32)) * jnp.dot(
            xb, su_ref[0].astype(BF16), preferred_element_type=F32)
        acc_scr[...] = jnp.dot(h.astype(BF16), sd_ref[0].astype(BF16), preferred_element_type=F32)

    xb = xb_scr[...]
    h = _silu(jnp.dot(xb, wg_ref[0, 0].astype(BF16), preferred_element_type=F32)) * jnp.dot(
        xb, wu_ref[0, 0].astype(BF16), preferred_element_type=F32)
    pick = lax.broadcasted_iota(jnp.int32, (2 * N_EXPERTS, EXPERT_FF), 0) % N_EXPERTS == e
    gate = jnp.dot(gsplit_scr[...], jnp.where(pick, 1.0, 0.0).astype(BF16), preferred_element_type=F32)
    gate = jnp.concatenate([gate] * (D_MODEL // EXPERT_FF), axis=1)
    acc_scr[...] += jnp.dot(h.astype(BF16), wd_ref[0, 0].astype(BF16), preferred_element_type=F32) * gate

    @pl.when(e == ne - 1)
    def _():
        o_ref[...] = _layer_norm(alpha * x_ref[...] + acc_scr[...], lng_ref[...], lnb_ref[...])


def _moe(x, gates, w, layer, alpha):
    t = x.shape[0]
    tm = _row_tile(t, 1024)
    d, f, e = D_MODEL, EXPERT_FF, N_EXPERTS
    full = lambda a: pl.BlockSpec(a.shape, lambda i, j: (0,) * a.ndim)
    lng, lnb = w['ln_g'][layer, 1].reshape(1, -1), w['ln_b'][layer, 1].reshape(1, -1)
    shared = lambda r, c: pl.BlockSpec((1, r, c), lambda i, j: (layer, 0, 0))
    return pl.pallas_call(
        functools.partial(_moe_kernel, alpha=alpha),
        grid=(t // tm, e),
        in_specs=[pl.BlockSpec((tm, d), lambda i, j: (i, 0)),
                  pl.BlockSpec((tm, e), lambda i, j: (i, 0)),
                  pl.BlockSpec((1, 1, d, f), lambda i, j: (layer, j, 0, 0)),
                  pl.BlockSpec((1, 1, d, f), lambda i, j: (layer, j, 0, 0)),
                  pl.BlockSpec((1, 1, f, d), lambda i, j: (layer, j, 0, 0)),
                  shared(d, f), shared(d, f), shared(f, d), full(lng), full(lnb)],
        out_specs=pl.BlockSpec((tm, d), lambda i, j: (i, 0)),
        out_shape=jax.ShapeDtypeStruct((t, d), F32),
        scratch_shapes=[pltpu.VMEM((tm, d), BF16), pltpu.VMEM((tm, 2 * e), BF16), pltpu.VMEM((tm, d), F32)],
        compiler_params=_cparams("parallel", "arbitrary"),
        name="moe",
    )(x, gates, w['expert_w_gate'], w['expert_w_up'], w['expert_w_down'],
      w['shared_w_gate'], w['shared_w_up'], w['shared_w_down'], lng, lnb)


def _cache_shift_kernel(c_ref, n_ref, o_ref):
    hh, dd, n = c_ref.shape[2:]
    c = c_ref[0, 0].reshape(hh * dd, n)
    lane = lax.broadcasted_iota(jnp.int32, c.shape, 1)
    shifted = pltpu.roll(c, n - 1, axis=1)
    o_ref[0, 0] = jnp.where(lane == n - 1, n_ref[0, 0], shifted).reshape(hh, dd, n)


def _cache_shift(cache, new_rows):
    depth, n, hh, dd, buf = cache.shape
    blk = pl.BlockSpec((1, 1, hh, dd, buf), lambda l, b: (l, b, 0, 0, 0))
    col = pl.BlockSpec((1, 1, hh * dd, 1), lambda l, b: (l, b, 0, 0))
    return pl.pallas_call(
        _cache_shift_kernel,
        grid=(depth, n),
        in_specs=[blk, col],
        out_specs=blk,
        out_shape=jax.ShapeDtypeStruct(cache.shape, cache.dtype),
        compiler_params=_cparams("parallel", "parallel"),
        name="cache_shift",
    )(cache, new_rows.reshape(depth, n, hh * dd, 1))


def _trunk_layer(x, bsz, seq, pos0, h0_re, h0_im, rw_s0, rw_prev, k_buf, v_buf, lp, s5p, rope_tables, alpha):
    t = bsz * seq
    u, u_bf16, p_rw, qkv = _proj(x, lp['w_in_bf16'])
    if seq == 1:
        ab_re, ab_im, bb_re, bb_im = s5p[7:]
        y_s5, h_re, h_im = _s5_step(u, h0_re, h0_im, ab_re, ab_im, bb_re, bb_im,
                                    lp['s5_c_re'], lp['s5_c_im'], lp['s5_d'])
    else:
        y_s5, h_re, h_im = _s5_scan(u, u_bf16, lp['s5_emb'], h0_re, h0_im, bsz, seq)
    (r, lw, k2, v, ah, b, gate, bonus), cols = _rw_prep(p_rw, rw_prev, lp, bsz, seq)
    y_rw, rw_s = _rw_scan(r, ah, lw, v, b, k2, cols, rw_s0, bsz, seq)
    rw_row = p_rw.reshape(bsz, seq, 4 * RW_WIDTH)[:, -1]
    q, k, vv = _rope(qkv, rope_tables)
    if seq == 1:
        y_att = _attn_step(q, k, vv, k_buf, v_buf, lp['layer'])
    else:
        assert k_buf is None
        y_att = _attn_prompt(q, k, vv, bsz, seq)
    keep = min(WIN_MAX, seq)
    k_new = k.reshape(bsz, seq, ATT_HEADS, HEAD_DIM)[:, seq - keep:]
    v_new = vv.reshape(bsz, seq, ATT_HEADS, HEAD_DIM)[:, seq - keep:]
    x1 = _mix(x, y_s5, y_rw, bonus, gate, y_att, lp, alpha)
    gates = _router(x1, lp['w_router'], lp['router_bias'])
    x2 = _moe(x1, gates, lp['all'], lp['layer'], alpha)
    return x2, (h_re, h_im, rw_s, rw_row, k_new, v_new)


def kernel(x_prompt, x_sample, state_s5_re, state_s5_im, state_rwkv, state_rwkv_shift, cache_attn_k, cache_attn_v, w_in, s5_lambda_re, s5_lambda_im, s5_b_re, s5_b_im, s5_c_re, s5_c_im, s5_d, s5_log_step, s5_w_glu, s5_b_glu, rw_mu, rw_w0, rw_w1, rw_w2, rw_a0, rw_a1, rw_a2, rw_g1, rw_g2, rw_k_k, rw_k_a, rw_r_k, rw_gn_g, rw_gn_b, w_out, ln_g, ln_b, w_router, router_bias, expert_w_gate, expert_w_up, expert_w_down, shared_w_gate, shared_w_up, shared_w_down):
    depth = w_in.shape[0]
    alpha = (2 * depth) ** 0.25
    bsz, seq, d = x_prompt.shape
    dbsz, dseq, _ = x_sample.shape
    past = cache_attn_k.shape[2]
    names = dict(s5_c_re=s5_c_re, s5_c_im=s5_c_im, s5_d=s5_d, s5_w_glu=s5_w_glu, s5_b_glu=s5_b_glu,
                 rw_mu=rw_mu, rw_w0=rw_w0, rw_w1=rw_w1, rw_w2=rw_w2, rw_a0=rw_a0, rw_a1=rw_a1, rw_a2=rw_a2,
                 rw_g1=rw_g1, rw_g2=rw_g2, rw_k_k=rw_k_k, rw_k_a=rw_k_a, rw_r_k=rw_r_k, rw_gn_g=rw_gn_g,
                 rw_gn_b=rw_gn_b, w_out=w_out, ln_g=ln_g, ln_b=ln_b, w_router=w_router,
                 router_bias=router_bias, expert_w_gate=expert_w_gate, expert_w_up=expert_w_up,
                 expert_w_down=expert_w_down, shared_w_gate=shared_w_gate, shared_w_up=shared_w_up,
                 shared_w_down=shared_w_down)
    assert dseq == 1
    tables_p = _rope_tables(jnp.arange(seq))
    tables_s = _rope_tables(jnp.full((dbsz,), PAST_LEN))
    yp = x_prompt.reshape(bsz * seq, d)
    ys = x_sample.reshape(dbsz * dseq, d)
    k_cache = jnp.transpose(cache_attn_k, (0, 1, 3, 4, 2))
    v_cache = jnp.transpose(cache_attn_v, (0, 1, 3, 4, 2))
    new_p, new_s = [], []
    for l in range(depth):
        lp = {k: v[l] for k, v in names.items() if not k.startswith(('expert_w', 'shared_w'))}
        lp['all'], lp['layer'] = names, l
        lp['w_in_bf16'] = w_in[l].astype(BF16)
        s5p = _s5_prep(s5_lambda_re[l], s5_lambda_im[l], s5_log_step[l], s5_b_re[l], s5_b_im[l],
                       s5_c_re[l], s5_c_im[l])
        lp['s5_emb'] = _s5_embed(s5p[:7], s5_d[l])
        zs = jnp.zeros((bsz, S5_GROUPS, S5_STATE), F32)
        z_s = jnp.zeros((bsz, RW_HEADS, HEAD_DIM, HEAD_DIM), F32)
        zrow = jnp.zeros((bsz, 4 * RW_WIDTH), F32)
        yp, st_p = _trunk_layer(yp, bsz, seq, 0, zs, zs, z_s, zrow, None, None, lp, s5p, tables_p, alpha)
        ys, st_s = _trunk_layer(ys, dbsz, dseq, past, state_s5_re[l], state_s5_im[l], state_rwkv[l],
                                state_rwkv_shift[l], k_cache, v_cache, lp, s5p, tables_s, alpha)
        new_p.append(st_p)
        new_s.append(st_s)
    stack = lambda sts: [jnp.stack([s[i] for s in sts], 0) for i in range(6)]
    out_p, out_s = stack(new_p), stack(new_s)
    assert past == WIN_MAX
    to_rows = lambda c: jnp.transpose(c, (0, 1, 4, 2, 3))
    out_s[4] = to_rows(_cache_shift(k_cache, out_s[4].reshape(depth, dbsz, ATT_WIDTH)))
    out_s[5] = to_rows(_cache_shift(v_cache, out_s[5].reshape(depth, dbsz, ATT_WIDTH)))
    return (yp.reshape(bsz, seq, d), ys.reshape(dbsz, dseq, d), *out_p, *out_s)
```

```python
import functools
import math

import numpy as np
import jax
import jax.numpy as jnp
from jax import lax
from jax.experimental import pallas as pl
from jax.experimental.pallas import tpu as pltpu

F32 = jnp.float32
BF16 = jnp.bfloat16

D_MODEL = 1024
PAST_LEN = 8192
HEAD_DIM = 64
S5_WIDTH = 256
S5_GROUP = 16
S5_GROUPS = 16
S5_STATE = 64
RW_WIDTH = 384
RW_HEADS = 6
ATT_WIDTH = 384
ATT_HEADS = 6
N_IN = S5_WIDTH + 4 * RW_WIDTH + 3 * ATT_WIDTH
RW_GN_EPS = 64e-5
DILATED_PATTERNS = ((128, 1), (512, 4), (2048, 16))
WIN_MAX = 2048
ATT_SCALE = HEAD_DIM ** -0.5
ROPE_THETA = 500000.0
ROPE_DIM = HEAD_DIM // 4
NEG_INF = -1e30
N_EXPERTS = 64
N_EXPERT_GROUPS = 8
TOPK_GROUPS = 4
TOP_K = 8
EXPERT_FF = 256
ROUTED_SCALE = 2.5
LN_EPS = 1e-5

S5_CHUNK = 16
S5_STATE_BLOCK = 256
S5_OUT_BLOCK = 512
RW_CHUNK = 64
RW_SEQS_PER_STEP = 2
MOE_EXPERTS_PER_STEP = 2
ATT_TILE = 2048
ATT_SUB = 128
VMEM_LIMIT = 56 * 1024 * 1024


def _cparams(*sem):
    return pltpu.CompilerParams(dimension_semantics=sem, vmem_limit_bytes=VMEM_LIMIT)


def _bdot(a, b):
    return jnp.dot(a.astype(BF16), b.astype(BF16), preferred_element_type=F32)


def _split(x):
    hi = x.astype(BF16)
    lo = (x - hi.astype(F32)).astype(BF16)
    return hi, lo


def _dot3(a, b):
    ah, al = _split(a)
    bh, bl = _split(b)
    d = functools.partial(jnp.dot, preferred_element_type=F32)
    return d(ah, bh) + (d(ah, bl) + d(al, bh))


def _dot2_exact_rhs(a, b_bf16):
    ah, al = _split(a)
    d = functools.partial(jnp.dot, preferred_element_type=F32)
    return d(ah, b_bf16) + d(al, b_bf16)


def _sigmoid(x):
    return 1.0 / (1.0 + jnp.exp(-x))


def _silu(x):
    return x * _sigmoid(x)


def _layer_norm(x, g, b):
    mu = jnp.mean(x, axis=-1, keepdims=True)
    xc = x - mu
    var = jnp.mean(xc * xc, axis=-1, keepdims=True)
    return xc * lax.rsqrt(var + LN_EPS) * g + b


def _row_tile(t, pref):
    tm = min(t, pref)
    assert t % tm == 0
    return tm


def _proj_kernel(x_ref, w_ref, u_ref, ub_ref, rw_ref, qkv_ref):
    p = jnp.dot(x_ref[...].astype(BF16), w_ref[...], preferred_element_type=F32)
    o1 = S5_WIDTH
    o2 = o1 + 4 * RW_WIDTH
    u_ref[...] = p[:, :o1]
    ub_ref[...] = p[:, :o1].astype(BF16)
    rw_ref[...] = p[:, o1:o2]
    qkv_ref[...] = p[:, o2:]


def _proj(x, w_bf16):
    t = x.shape[0]
    tm = _row_tile(t, 512)
    return pl.pallas_call(
        _proj_kernel,
        grid=(t // tm,),
        in_specs=[pl.BlockSpec((tm, D_MODEL), lambda i: (i, 0)),
                  pl.BlockSpec((D_MODEL, N_IN), lambda i: (0, 0))],
        out_specs=[pl.BlockSpec((tm, S5_WIDTH), lambda i: (i, 0)),
                   pl.BlockSpec((tm, S5_WIDTH), lambda i: (i, 0)),
                   pl.BlockSpec((tm, 4 * RW_WIDTH), lambda i: (i, 0)),
                   pl.BlockSpec((tm, 3 * ATT_WIDTH), lambda i: (i, 0))],
        out_shape=[jax.ShapeDtypeStruct((t, S5_WIDTH), F32),
                   jax.ShapeDtypeStruct((t, S5_WIDTH), BF16),
                   jax.ShapeDtypeStruct((t, 4 * RW_WIDTH), F32),
                   jax.ShapeDtypeStruct((t, 3 * ATT_WIDTH), F32)],
        compiler_params=_cparams("parallel"),
        name="proj",
    )(x, w_bf16)


def _s5_discretize(lr, li, log_step):
    dt = jnp.exp(log_step)
    mag = jnp.exp(lr * dt)
    ab_re = mag * jnp.cos(li * dt)
    ab_im = mag * jnp.sin(li * dt)
    den = lr * lr + li * li
    cf_re = ((ab_re - 1.0) * lr + ab_im * li) / den
    cf_im = (ab_im * lr - (ab_re - 1.0) * li) / den
    return dt, ab_re, ab_im, cf_re, cf_im


def _s5_power(lr, li, dt, n):
    mag = jnp.exp(n * (lr * dt))
    ang = n * (li * dt)
    return mag * jnp.cos(ang), mag * jnp.sin(ang)


def _s5_prep_kernel(lr_ref, li_ref, ls_ref, lrc_ref, lic_ref, lsc_ref, bt_re_ref, bt_im_ref,
                    ct_re_ref, ct_im_ref,
                    kmat_ref, bout_re_ref, bout_im_ref, win_re_ref, win_im_ref,
                    aq_re_ref, aq_im_ref, ab_re_ref, ab_im_ref, bb_re_ref, bb_im_ref):
    q = S5_CHUNK
    n_rows = q * S5_GROUP
    lr, li, ls = lr_ref[0], li_ref[0], ls_ref[0]
    dt, ab_re, ab_im, cf_re, cf_im = _s5_discretize(lr, li, ls)
    ab_re_ref[0] = ab_re
    ab_im_ref[0] = ab_im
    aq_re, aq_im = _s5_power(lr, li, dt, jnp.float32(q))
    aq_re_ref[0] = aq_re
    aq_im_ref[0] = aq_im
    bt_re, bt_im = bt_re_ref[0], bt_im_ref[0]
    bb_re = cf_re * bt_re - cf_im * bt_im
    bb_im = cf_re * bt_im + cf_im * bt_re
    bb_re_ref[0] = bb_re[:S5_GROUP]
    bb_im_ref[0] = bb_im[:S5_GROUP]
    step = lax.broadcasted_iota(jnp.int32, (n_rows, S5_STATE), 0) // S5_GROUP
    pw_re, pw_im = _s5_power(lr, li, dt, (q - 1 - step).astype(F32))
    bout_re_ref[0] = (bb_re * pw_re - bb_im * pw_im).astype(BF16)
    bout_im_ref[0] = (bb_re * pw_im + bb_im * pw_re).astype(BF16)
    lrc, lic, lsc = lrc_ref[0], lic_ref[0], lsc_ref[0]
    dtc, abc_re, abc_im, _, _ = _s5_discretize(lrc, lic, lsc)
    lag = (lax.broadcasted_iota(jnp.int32, (S5_STATE, n_rows), 1) // S5_GROUP).astype(F32)
    p0_re, p0_im = _s5_power(lrc, lic, dtc, lag)
    ct_re, ct_im = ct_re_ref[0], ct_im_ref[0]
    e0_re = ct_re * p0_re - ct_im * p0_im
    e0_im = ct_re * p0_im + ct_im * p0_re
    e1_re = e0_re * abc_re - e0_im * abc_im
    e1_im = e0_re * abc_im + e0_im * abc_re
    win_re_ref[0] = e1_re.astype(BF16)
    win_im_ref[0] = (-e1_im).astype(BF16)
    g = _dot3(bb_re[:S5_GROUP], e0_re) - _dot3(bb_im[:S5_GROUP], e0_im)
    lane = lax.broadcasted_iota(jnp.int32, (S5_GROUP, n_rows), 1)
    blocks = []
    for i in range(q):
        if i == 0:
            blocks.append(g)
        else:
            shifted = pltpu.roll(g, S5_GROUP * i, axis=1)
            blocks.append(jnp.where(lane >= S5_GROUP * i, shifted, 0.0))
    kmat_ref[0] = jnp.concatenate(blocks, axis=0).astype(BF16)


def _s5_prep(lam_re, lam_im, log_step, b_re, b_im, c_re, c_im):
    g, p, c, q = S5_GROUPS, S5_STATE, S5_GROUP, S5_CHUNK
    n = q * c
    row = lambda a: a.reshape(g, 1, p)
    col = lambda a: jnp.broadcast_to(a.reshape(g, p, 1), (g, p, n))
    ls_row = jnp.broadcast_to(log_step.reshape(g, 1, 1), (g, 1, p))
    ls_col = jnp.broadcast_to(log_step.reshape(g, 1, 1), (g, p, n))
    bt = lambda a: jnp.tile(jnp.swapaxes(a, 1, 2), (1, q, 1))
    ct = lambda a: jnp.tile(jnp.swapaxes(a, 1, 2), (1, 1, q))
    spec = lambda *s: pl.BlockSpec((1,) + s, lambda i: (i, 0, 0))
    sds = lambda s, d: jax.ShapeDtypeStruct((g,) + s, d)
    return pl.pallas_call(
        _s5_prep_kernel,
        grid=(g,),
        in_specs=[spec(1, p), spec(1, p), spec(1, p), spec(p, n), spec(p, n), spec(p, n),
                  spec(n, p), spec(n, p), spec(p, n), spec(p, n)],
        out_specs=[spec(n, n), spec(n, p), spec(n, p), spec(p, n), spec(p, n),
                   spec(1, p), spec(1, p), spec(1, p), spec(1, p), spec(c, p), spec(c, p)],
        out_shape=[sds((n, n), BF16), sds((n, p), BF16), sds((n, p), BF16), sds((p, n), BF16),
                   sds((p, n), BF16), sds((1, p), F32), sds((1, p), F32), sds((1, p), F32),
                   sds((1, p), F32), sds((c, p), F32), sds((c, p), F32)],
        compiler_params=_cparams("parallel"),
        name="s5_prep",
    )(row(lam_re), row(lam_im), ls_row, col(lam_re), col(lam_im), ls_col,
      bt(b_re), bt(b_im), ct(c_re), ct(c_im))


def _s5_embed(mats, d_skip):
    g, p, c, q = S5_GROUPS, S5_STATE, S5_GROUP, S5_CHUNK
    kmat, bout_re, bout_im, win_re, win_im, aq_re, aq_im = mats
    n = q * g * c
    by_step = lambda m: jnp.swapaxes(m.reshape(g, q, c, m.shape[-1]), 0, 1).reshape(n, m.shape[-1])
    src = jnp.arange(q * c)
    dst = jnp.arange(n)
    spread = ((src[:, None] // c == dst[None, :] // (g * c)) & (src[:, None] % c == dst[None, :] % c)).astype(BF16)
    over_groups = lambda m: jnp.dot(m, spread, preferred_element_type=BF16)
    row_group = lambda rows, per: (jnp.arange(rows) // per) % g
    col_group_out = (dst // c) % g
    col_group_state = jnp.arange(g * p) // p
    rows_in = row_group(n, c)
    rows_state = jnp.arange(g * p) // p
    keep = lambda m, rg, cg: jnp.where(rg[:, None] == cg[None, :], m, jnp.zeros((), m.dtype))
    kf = keep(over_groups(by_step(kmat)), rows_in, col_group_out)
    bo = lambda m: keep(jnp.tile(by_step(m), (1, g)), rows_in, col_group_state)
    wi = lambda m: keep(over_groups(m.reshape(g * p, q * c)), rows_state, col_group_out)
    d_full = jnp.tile(d_skip.reshape(1, g * c), (1, q))
    return (kf, bo(bout_re), bo(bout_im), wi(win_re), wi(win_im),
            aq_re.reshape(1, g * p), aq_im.reshape(1, g * p), d_full)


def _s5_state_kernel(xb_ref, bout_re_ref, bout_im_ref, aq_re_ref, aq_im_ref, h0_re_ref, h0_im_ref,
                     hin_re_ref, hin_im_ref, hf_re_ref, hf_im_ref, s_re, s_im, e_re, e_im, *, bsz, n_chunks):
    xb = xb_ref[...]
    s_re[...] = jnp.dot(xb, bout_re_ref[...], preferred_element_type=F32)
    s_im[...] = jnp.dot(xb, bout_im_ref[...], preferred_element_type=F32)
    ar, ai = aq_re_ref[...], aq_im_ref[...]

    def body(k, carry):
        new = []
        for b in range(bsz):
            hr, hi = carry[2 * b], carry[2 * b + 1]
            r = b * n_chunks + k
            e_re[pl.ds(r, 1), :] = hr
            e_im[pl.ds(r, 1), :] = hi
            new.append(ar * hr - ai * hi + s_re[pl.ds(r, 1), :])
            new.append(ar * hi + ai * hr + s_im[pl.ds(r, 1), :])
        return tuple(new)

    init = []
    for b in range(bsz):
        init.append(h0_re_ref[b:b + 1, :])
        init.append(h0_im_ref[b:b + 1, :])
    fin = lax.fori_loop(0, n_chunks, body, tuple(init))
    for b in range(bsz):
        hf_re_ref[b:b + 1, :] = fin[2 * b]
        hf_im_ref[b:b + 1, :] = fin[2 * b + 1]
    hin_re_ref[...] = e_re[...].astype(BF16)
    hin_im_ref[...] = e_im[...].astype(BF16)


def _s5_out_kernel(xb_ref, x_ref, kf_ref, hin_re_ref, hin_im_ref, win_re_ref, win_im_ref, d_ref, y_ref):
    y = jnp.dot(xb_ref[...], kf_ref[...], preferred_element_type=F32)
    y = y + jnp.dot(hin_re_ref[...], win_re_ref[...], preferred_element_type=F32)
    y = y + jnp.dot(hin_im_ref[...], win_im_ref[...], preferred_element_type=F32)
    y_ref[...] = y + x_ref[...] * d_ref[...]


def _s5_scan(u, u_bf16, emb, h0_re, h0_im, bsz, seq):
    g, p, c, q = S5_GROUPS, S5_STATE, S5_GROUP, S5_CHUNK
    kf, bout_re, bout_im, win_re, win_im, aq_re, aq_im, d_full = emb
    n, gp = q * g * c, g * p
    n_chunks = seq // q
    rows = bsz * n_chunks
    x = u.reshape(rows, n)
    xb = u_bf16.reshape(rows, n)
    sb = S5_STATE_BLOCK
    col = lambda r, w: pl.BlockSpec((r, w), lambda i: (0, i))
    fix = lambda r, w: pl.BlockSpec((r, w), lambda i: (0, 0))
    hin_re, hin_im, hf_re, hf_im = pl.pallas_call(
        functools.partial(_s5_state_kernel, bsz=bsz, n_chunks=n_chunks),
        grid=(gp // sb,),
        in_specs=[fix(rows, n), col(n, sb), col(n, sb), col(1, sb), col(1, sb), col(bsz, sb), col(bsz, sb)],
        out_specs=[col(rows, sb), col(rows, sb), col(bsz, sb), col(bsz, sb)],
        out_shape=[jax.ShapeDtypeStruct((rows, gp), BF16), jax.ShapeDtypeStruct((rows, gp), BF16),
                   jax.ShapeDtypeStruct((bsz, gp), F32), jax.ShapeDtypeStruct((bsz, gp), F32)],
        scratch_shapes=[pltpu.VMEM((rows, sb), F32)] * 4,
        compiler_params=_cparams("parallel"),
        name="s5_state",
    )(xb, bout_re, bout_im, aq_re, aq_im, h0_re.reshape(bsz, gp), h0_im.reshape(bsz, gp))
    ob = S5_OUT_BLOCK
    y = pl.pallas_call(
        _s5_out_kernel,
        grid=(n // ob,),
        in_specs=[fix(rows, n), col(rows, ob), col(n, ob), fix(rows, gp), fix(rows, gp), col(gp, ob),
                  col(gp, ob), col(1, ob)],
        out_specs=col(rows, ob),
        out_shape=jax.ShapeDtypeStruct((rows, n), F32),
        compiler_params=_cparams("parallel"),
        name="s5_out",
    )(xb, x, kf, hin_re, hin_im, win_re, win_im, d_full)
    return (y.reshape(bsz * seq, g * c), hf_re.reshape(bsz, g, p), hf_im.reshape(bsz, g, p))


def _s5_step_kernel(u_ref, h0_re_ref, h0_im_ref, a_re_ref, a_im_ref, bb_re_ref, bb_im_ref,
                    cc_re_ref, cc_im_ref, d_ref, y_ref, h_re_ref, h_im_ref):
    u = u_ref[...]
    ub = u.astype(BF16)
    a_re, a_im = a_re_ref[...], a_im_ref[...]
    h0r, h0i = h0_re_ref[...], h0_im_ref[...]
    hr = a_re * h0r - a_im * h0i + jnp.dot(ub, bb_re_ref[...], preferred_element_type=F32)
    hi = a_re * h0i + a_im * h0r + jnp.dot(ub, bb_im_ref[...], preferred_element_type=F32)
    h_re_ref[...] = hr
    h_im_ref[...] = hi
    y = (jnp.dot(hr.astype(BF16), cc_re_ref[...], preferred_element_type=F32)
         - jnp.dot(hi.astype(BF16), cc_im_ref[...], preferred_element_type=F32))
    y_ref[...] = y + u * d_ref[...]


def _block_diag(blocks):
    g, r, c = blocks.shape
    eye = jnp.eye(g, dtype=blocks.dtype)
    return (blocks[:, :, None, :] * eye[:, None, :, None]).reshape(g * r, g * c)


def _s5_step(u, h0_re, h0_im, ab_re, ab_im, bb_re, bb_im, c_re, c_im, d_skip):
    n = u.shape[0]
    gp = S5_GROUPS * S5_STATE
    bbd_re = _block_diag(bb_re).astype(BF16)
    bbd_im = _block_diag(bb_im).astype(BF16)
    ccd_re = _block_diag(jnp.swapaxes(c_re, 1, 2)).astype(BF16)
    ccd_im = _block_diag(jnp.swapaxes(c_im, 1, 2)).astype(BF16)
    y, h_re, h_im = pl.pallas_call(
        _s5_step_kernel,
        out_shape=[jax.ShapeDtypeStruct((n, S5_WIDTH), F32),
                   jax.ShapeDtypeStruct((n, gp), F32),
                   jax.ShapeDtypeStruct((n, gp), F32)],
        compiler_params=pltpu.CompilerParams(vmem_limit_bytes=VMEM_LIMIT),
        name="s5_step",
    )(u, h0_re.reshape(n, gp), h0_im.reshape(n, gp), ab_re.reshape(1, gp), ab_im.reshape(1, gp),
      bbd_re, bbd_im, ccd_re, ccd_im, d_skip.reshape(1, S5_WIDTH))
    return y, h_re.reshape(n, S5_GROUPS, S5_STATE), h_im.reshape(n, S5_GROUPS, S5_STATE)


def _rw_prep_kernel(p_ref, tail_ref, prev_ref, mu_ref, w0_ref, w1_ref, w2_ref, a0_ref, a1_ref, a2_ref,
                    g1_ref, g2_ref, kk_ref, ka_ref, rk_ref, hsum_ref,
                    r_out, lw_out, k_out, v_out, ah_out, b_out, g_out, bonus_out, *col_outs, seq_tiles):
    w = RW_WIDTH
    p = p_ref[...]
    if seq_tiles:
        first = pl.program_id(0) % seq_tiles == 0
        before = jnp.where(first, prev_ref[0], tail_ref[7:8, :])
        rowid = lax.broadcasted_iota(jnp.int32, p.shape, 0)
        p_prev = jnp.where(rowid == 0, before, pltpu.roll(p, 1, axis=0))
    else:
        p_prev = prev_ref[...]
    dp = p_prev - p
    r_in, k_in, v_in, z = p[:, :w], p[:, w:2 * w], p[:, 2 * w:3 * w], p[:, 3 * w:]
    dr, dk, dv, dz = dp[:, :w], dp[:, w:2 * w], dp[:, 2 * w:3 * w], dp[:, 3 * w:]
    mu = mu_ref[...]
    r = r_in + dr * mu[0:1]
    xw = z + dz * mu[1:2]
    k = k_in + dk * mu[2:3]
    v = v_in + dv * mu[3:4]
    xa = z + dz * mu[4:5]
    xg = z + dz * mu[5:6]
    t = w0_ref[...] + _bdot(jnp.tanh(_bdot(xw, w1_ref[...])), w2_ref[...])
    nt = -t
    softplus = jnp.maximum(nt, 0.0) + jnp.log(1.0 + jnp.exp(-jnp.abs(nt)))
    w_log = -softplus - 0.5
    lw = -jnp.exp(w_log)
    lw_out[...] = lw
    a = _sigmoid(a0_ref[...] + _bdot(_bdot(xa, a1_ref[...]), a2_ref[...]))
    g_out[...] = _bdot(_sigmoid(_bdot(xg, g1_ref[...])), g2_ref[...])
    kk = k * kk_ref[...]
    ss = _dot2_exact_rhs(kk * kk, hsum_ref[...])
    kk = kk * lax.rsqrt(jnp.maximum(ss, 1e-24))
    k2 = k * (1.0 + (a - 1.0) * ka_ref[...])
    r_out[...] = r
    k_out[...] = k2
    v_out[...] = v
    ah_out[...] = -kk
    b = kk * a
    b_out[...] = b
    bonus_out[...] = _dot2_exact_rhs(r * k2 * rk_ref[...], hsum_ref[...]) * v
    for arr, out in zip((b, k2, lw), col_outs):
        arr_t = arr.T
        for j in range(arr.shape[0] // RW_CHUNK):
            out[0, j] = arr_t[:, j * RW_CHUNK:(j + 1) * RW_CHUNK]


def _head_sum_matrix(width):
    idx = np.arange(width) // HEAD_DIM
    return jnp.asarray((idx[:, None] == idx[None, :]).astype(np.float32), dtype=BF16)


def _rw_prep(p, rw_prev, lp, bsz, seq):
    t = p.shape[0]
    w = RW_WIDTH
    if seq == 1:
        tm, seq_tiles = t, 0
        tail_spec = pl.BlockSpec((t, 4 * w), lambda i: (0, 0))
        prev_spec = pl.BlockSpec((t, 4 * w), lambda i: (0, 0))
        prev_arg = rw_prev
    else:
        tm = _row_tile(seq, 512)
        seq_tiles = seq // tm
        tail_spec = pl.BlockSpec((8, 4 * w), lambda i: (jnp.maximum(i * (tm // 8) - 1, 0), 0))
        prev_spec = pl.BlockSpec((1, 1, 4 * w), lambda i: (i // seq_tiles, 0, 0))
        prev_arg = rw_prev.reshape(bsz, 1, 4 * w)
    row = lambda a: a.reshape(1, w)
    full = lambda a: pl.BlockSpec(a.shape, lambda i: (0,) * a.ndim)
    params = [lp['rw_mu'], row(lp['rw_w0']), lp['rw_w1'].astype(BF16), lp['rw_w2'].astype(BF16),
              row(lp['rw_a0']), lp['rw_a1'].astype(BF16), lp['rw_a2'].astype(BF16),
              lp['rw_g1'].astype(BF16), lp['rw_g2'].astype(BF16), row(lp['rw_k_k']),
              row(lp['rw_k_a']), row(lp['rw_r_k']), _head_sum_matrix(w)]
    tile = lambda n: pl.BlockSpec((tm, n), lambda i: (i, 0))
    out_specs = [tile(w)] * 8
    out_shape = [jax.ShapeDtypeStruct((t, w), F32)] * 8
    if seq_tiles:
        assert tm % RW_CHUNK == 0
        cpt = tm // RW_CHUNK
        out_specs = out_specs + [pl.BlockSpec((1, cpt, w, RW_CHUNK),
                                              lambda i: (i // seq_tiles, i % seq_tiles, 0, 0))] * 3
        out_shape = out_shape + [jax.ShapeDtypeStruct((bsz, seq // RW_CHUNK, w, RW_CHUNK), F32)] * 3
    outs = pl.pallas_call(
        functools.partial(_rw_prep_kernel, seq_tiles=seq_tiles),
        grid=(t // tm,),
        in_specs=[tile(4 * w), tail_spec, prev_spec] + [full(a) for a in params],
        out_specs=out_specs,
        out_shape=out_shape,
        compiler_params=_cparams("parallel"),
        name="rw_prep",
    )(p, p, prev_arg, *params)
    return outs[:8], (tuple(outs[8:]) if seq_tiles else None)


def _rw_scan_kernel(r_ref, ah_ref, lw_ref, v_ref, bt_ref, kt_ref, lwt_ref, h0_ref,
                    y_ref, hf_ref, h_scr):
    c = RW_CHUNK
    nseq = r_ref.shape[0]
    ci = pl.program_id(1)

    @pl.when(ci == 0)
    def _():
        h_scr[...] = h0_ref[...]

    row = lax.broadcasted_iota(jnp.int32, (c, c), 0)
    col = lax.broadcasted_iota(jnp.int32, (c, c), 1)
    incl = row >= col
    strict = row > col
    tri_lo = jnp.where(incl, 1.0, 0.0).astype(BF16)
    tri_up = jnp.where(row <= col, 1.0, 0.0).astype(BF16)
    eye = jnp.where(row == col, 1.0, 0.0)
    d = functools.partial(jnp.dot, preferred_element_type=F32)

    def split3(x):
        hi, mid = _split(x)
        return hi, mid, (x - hi.astype(F32) - mid.astype(F32)).astype(BF16)

    cums, cums_t = [], []
    for s in range(nseq):
        p3 = split3(lw_ref[s])
        cums.append(d(tri_lo, p3[0]) + (d(tri_lo, p3[1]) + d(tri_lo, p3[2])))
        q3 = split3(lwt_ref[s, 0])
        cums_t.append(d(q3[0], tri_up) + (d(q3[1], tri_up) + d(q3[2], tri_up)))

    units = [(s, h) for s in range(nseq) for h in range(RW_HEADS)]
    sl = lambda h: slice(h * HEAD_DIM, (h + 1) * HEAD_DIM)
    each = lambda f: [f(s, h) for s, h in units]
    idx = range(len(units))
    cum = each(lambda s, h: cums[s][:, sl(h)])
    cum_t = each(lambda s, h: cums_t[s][sl(h), :])
    ar = each(lambda s, h: None)
    for i, (s, h) in enumerate(units):
        a_t = ah_ref[s, :, sl(h)] * jnp.exp(cum[i] - lw_ref[s, :, sl(h)])
        r_t = r_ref[s, :, sl(h)] * jnp.exp(cum[i])
        ar[i] = jnp.concatenate([a_t, r_t], axis=0).astype(BF16)
    w_inv_t = [jnp.exp(-cum_t[i]) for i in idx]
    cum_end = [cum_t[i][:, c - 1:c] for i in idx]
    dec_t = [jnp.exp(cum_end[i] - cum_t[i]) for i in idx]
    bt = each(lambda s, h: bt_ref[s, 0, sl(h), :])
    kt = each(lambda s, h: kt_ref[s, 0, sl(h), :])
    b_t = [(bt[i] * w_inv_t[i]).astype(BF16) for i in idx]
    k_t = [(kt[i] * w_inv_t[i]).astype(BF16) for i in idx]
    b_d = [(bt[i] * dec_t[i]).astype(BF16) for i in idx]
    k_d = [(kt[i] * dec_t[i]).astype(BF16) for i in idx]
    mb = [d(ar[i], b_t[i]) for i in idx]
    mk = [d(ar[i], k_t[i]) for i in idx]
    a_ab = [jnp.where(strict, mb[i][:c], 0.0) for i in idx]
    a_ak = [jnp.where(strict, mk[i][:c], 0.0).astype(BF16) for i in idx]
    a_rb = [jnp.where(incl, mb[i][c:], 0.0).astype(BF16) for i in idx]
    a_rk = [jnp.where(incl, mk[i][c:], 0.0).astype(BF16) for i in idx]
    pw = a_ab
    inv = [eye + a_ab[i] for i in idx]
    for _ in range(int(math.log2(c)) - 1):
        pw = [_bdot(pw[i], pw[i]) for i in idx]
        inv = [inv[i] + _bdot(inv[i], pw[i]) for i in idx]
    h0 = each(lambda s, h: h_scr[s, h])
    vb = each(lambda s, h: v_ref[s, :, sl(h)].astype(BF16))
    ar_h = [d(ar[i], h0[i].astype(BF16)) for i in idx]
    rhs = [ar_h[i][:c] + d(a_ak[i], vb[i]) for i in idx]
    ub = [_bdot(inv[i], rhs[i]).astype(BF16) for i in idx]
    y = [ar_h[i][c:] + d(a_rb[i], ub[i]) + d(a_rk[i], vb[i]) for i in idx]
    h_new = [jnp.exp(cum_end[i]) * h0[i] + d(b_d[i], ub[i]) + d(k_d[i], vb[i]) for i in idx]
    for i, (s, h) in enumerate(units):
        h_scr[s, h] = h_new[i]
    for s in range(nseq):
        y_ref[s] = jnp.concatenate([y[s * RW_HEADS + h] for h in range(RW_HEADS)], axis=1)

    for i, (s, h) in enumerate(units):
        hf_ref[s, h] = h_new[i]


def _rw_scan(r, ah, lw, v, b, k2, cols, s0, bsz, seq):
    w, c = RW_WIDTH, RW_CHUNK
    pad = (-seq) % c
    sp = seq + pad

    def rows(a):
        a = a.reshape(bsz, seq, w)
        return jnp.pad(a, ((0, 0), (0, pad), (0, 0))) if pad else a

    r3, ah3, lw3, v3 = (rows(a) for a in (r, ah, lw, v))
    n_chunks = sp // c
    if cols is None:
        tr = lambda a: jnp.swapaxes(rows(a).reshape(bsz, n_chunks, c, w), 2, 3)
        cols = (tr(b), tr(k2), tr(lw))
    h0 = jnp.swapaxes(s0, 2, 3)
    ns = RW_SEQS_PER_STEP
    assert bsz % ns == 0
    rspec = pl.BlockSpec((ns, c, w), lambda i, j: (i, j, 0))
    cspec = pl.BlockSpec((ns, 1, w, c), lambda i, j: (i, j, 0, 0))
    sspec = pl.BlockSpec((ns, RW_HEADS, HEAD_DIM, HEAD_DIM), lambda i, j: (i, 0, 0, 0))
    y, hf = pl.pallas_call(
        _rw_scan_kernel,
        grid=(bsz // ns, n_chunks),
        in_specs=[rspec, rspec, rspec, rspec, cspec, cspec, cspec, sspec],
        out_specs=[rspec, sspec],
        out_shape=[jax.ShapeDtypeStruct((bsz, sp, w), F32),
                   jax.ShapeDtypeStruct((bsz, RW_HEADS, HEAD_DIM, HEAD_DIM), F32)],
        scratch_shapes=[pltpu.VMEM((ns, RW_HEADS, HEAD_DIM, HEAD_DIM), F32)],
        compiler_params=_cparams("parallel", "arbitrary"),
        name="rw_scan",
    )(r3, ah3, lw3, v3, *cols, h0)
    return y[:, :seq].reshape(bsz * seq, w), jnp.swapaxes(hf, 2, 3)


def _rope_kernel(qkv_ref, cos_ref, sin_lo_ref, sin_hi_ref, q_ref, k_ref, v_ref):
    w = ATT_WIDTH
    half = ROPE_DIM // 2
    x = qkv_ref[...]
    rep = lambda t: jnp.concatenate([t] * (w // 128), axis=1)
    cos, s_lo, s_hi = rep(cos_ref[...]), rep(sin_lo_ref[...]), rep(sin_hi_ref[...])

    def rot(t):
        up = pltpu.roll(t, w - half, axis=1)
        dn = pltpu.roll(t, half, axis=1)
        return t * cos + up * s_lo + dn * s_hi

    q_ref[...] = rot(x[:, :w]) * ATT_SCALE
    k_ref[...] = rot(x[:, w:2 * w])
    v_ref[...] = x[:, 2 * w:]


def _rope_tables(pos):
    half = ROPE_DIM // 2
    inv_freq = jnp.exp(-math.log(ROPE_THETA) * jnp.arange(half, dtype=jnp.float32) * (2.0 / ROPE_DIM))
    ang = pos.astype(jnp.float32)[:, None] * inv_freq[None, :]
    cos, sin = jnp.cos(ang), jnp.sin(ang)
    n = pos.shape[0]
    one = jnp.ones((n, HEAD_DIM - ROPE_DIM), F32)
    zero = jnp.zeros((n, HEAD_DIM - ROPE_DIM), F32)
    z8 = jnp.zeros((n, half), F32)
    cos_h = jnp.concatenate([cos, cos, one], axis=1)
    lo_h = jnp.concatenate([-sin, z8, zero], axis=1)
    hi_h = jnp.concatenate([z8, sin, zero], axis=1)
    two = lambda t: jnp.concatenate([t, t], axis=1)
    return two(cos_h), two(lo_h), two(hi_h)


def _rope(qkv, tables):
    t = qkv.shape[0]
    period = tables[0].shape[0]
    tm = _row_tile(period, 512)
    nb = period // tm
    w = ATT_WIDTH
    tspec = pl.BlockSpec((tm, 128), lambda i: (i % nb, 0))
    ospec = pl.BlockSpec((tm, w), lambda i: (i, 0))
    return pl.pallas_call(
        _rope_kernel,
        grid=(t // tm,),
        in_specs=[pl.BlockSpec((tm, 3 * w), lambda i: (i, 0)), tspec, tspec, tspec],
        out_specs=[ospec] * 3,
        out_shape=[jax.ShapeDtypeStruct((t, w), F32)] * 3,
        compiler_params=_cparams("parallel"),
        name="rope",
    )(qkv, *tables)


def _multiplicity(dist):
    m = np.zeros(dist.shape, np.float64)
    for window, dil in DILATED_PATTERNS:
        m += ((dist >= 0) & (dist <= window) & (dist % dil == 0))
    return m


def _dist_bias(dist):
    m = _multiplicity(dist)
    return np.where(m > 0, np.log(np.maximum(m, 1.0)), NEG_INF).astype(np.float32)


def _attn_kernel(q_ref, kc_ref, kp_ref, vc_ref, vp_ref, o_ref, kwin, vwin, acc_scr, m_scr, l_scr):
    tile, sub = ATT_TILE, ATT_SUB
    first = pl.program_id(2) == 0
    kwin[0:tile] = kp_ref[0]
    kwin[tile:] = kc_ref[0]
    vwin[0:tile] = vp_ref[0]
    vwin[tile:] = vc_ref[0]
    row = lax.broadcasted_iota(jnp.int32, (sub, 2 * sub), 0)
    col = lax.broadcasted_iota(jnp.int32, (sub, 2 * sub), 1)
    bias = jnp.where(col < sub, jnp.where(col >= row, 0.0, NEG_INF), jnp.where(col - sub <= row, 0.0, NEG_INF))
    bias_first = jnp.where(col < sub, NEG_INF, bias)
    head0 = lax.broadcasted_iota(jnp.int32, (sub, 2 * HEAD_DIM), 1) < HEAD_DIM
    nt = (((1,), (1,)), ((), ()))
    for p, (window, dil) in enumerate(DILATED_PATTERNS):
        assert window == dil * sub
        nsub = tile // dil // sub
        for rho in range(dil):
            for a in range(nsub):
                q_rows = pl.ds(rho + dil * sub * a, sub, stride=dil) if dil > 1 else pl.ds(sub * a, sub)
                k_start = tile + rho + dil * sub * (a - 1)
                k_rows = pl.ds(k_start, 2 * sub, stride=dil) if dil > 1 else pl.ds(k_start, 2 * sub)
                q = q_ref[0, q_rows, :]
                k = kwin[k_rows, :].astype(BF16)
                v = vwin[k_rows, :].astype(BF16)
                b = jnp.where(first, bias_first, bias) if a == 0 else bias
                stats = []
                for hh in range(2):
                    qh = jnp.where(head0 if hh == 0 else jnp.logical_not(head0), q, 0.0).astype(BF16)
                    s = lax.dot_general(qh, k, nt, preferred_element_type=F32) + b
                    m = jnp.max(s, axis=1, keepdims=True)
                    e = jnp.exp(s - m)
                    l = jnp.sum(e, axis=1, keepdims=True)
                    acc = jnp.dot(e.astype(BF16), v, preferred_element_type=F32)
                    stats.append((m, l, acc))
                pick = lambda i: jnp.where(head0, stats[0][i], stats[1][i])
                m_scr[p, q_rows, :] = pick(0)
                l_scr[p, q_rows, :] = pick(1)
                acc_scr[p, q_rows, :] = pick(2)
    ms = [m_scr[p] for p in range(len(DILATED_PATTERNS))]
    m = functools.reduce(jnp.maximum, ms)
    ws = [jnp.exp(mp - m) for mp in ms]
    den = sum(w * l_scr[p] for p, w in enumerate(ws))
    num = sum(w * acc_scr[p] for p, w in enumerate(ws))
    o_ref[0] = num / den


def _attn_prompt(q, k, v, bsz, seq):
    tile = ATT_TILE
    assert seq % tile == 0 and tile == WIN_MAX
    lanes = 2 * HEAD_DIM
    npair = ATT_WIDTH // lanes
    npat = len(DILATED_PATTERNS)
    r3 = lambda a: a.reshape(bsz, seq, ATT_WIDTH)
    cur = pl.BlockSpec((1, tile, lanes), lambda b, h, i: (b, i, h))
    prev = pl.BlockSpec((1, tile, lanes), lambda b, h, i: (b, jnp.maximum(i - 1, 0), h))
    o = pl.pallas_call(
        _attn_kernel,
        grid=(bsz, npair, seq // tile),
        in_specs=[cur, cur, prev, cur, prev],
        out_specs=cur,
        out_shape=jax.ShapeDtypeStruct((bsz, seq, ATT_WIDTH), F32),
        scratch_shapes=[pltpu.VMEM((2 * tile, lanes), F32), pltpu.VMEM((2 * tile, lanes), F32),
                        pltpu.VMEM((npat, tile, lanes), F32), pltpu.VMEM((npat, tile, lanes), F32),
                        pltpu.VMEM((npat, tile, lanes), F32)],
        compiler_params=_cparams("parallel", "parallel", "arbitrary"),
        name="attn_prompt",
    )(r3(q), r3(k), r3(k), r3(v), r3(v))
    return o.reshape(bsz * seq, ATT_WIDTH)


def _attn_step_kernel(q_ref, kn_ref, vn_ref, kc_ref, vc_ref, bias_ref, o_ref):
    w = ATT_WIDTH
    q = q_ref[0]
    kn, vn = kn_ref[0], vn_ref[0]
    n_buf = kc_ref.shape[-1]
    kc = kc_ref[0, 0].reshape(w, n_buf).astype(BF16)
    vc = vc_ref[0, 0].reshape(w, n_buf).astype(BF16)
    hrow = lax.broadcasted_iota(jnp.int32, (8, w), 0)
    hcol = lax.broadcasted_iota(jnp.int32, (8, w), 1) // HEAD_DIM
    own = hrow == hcol
    qh = jnp.where(own, q, 0.0).astype(BF16)
    s_c = jnp.dot(qh, kc, preferred_element_type=F32) + bias_ref[...]
    s_n = jnp.sum(qh.astype(F32) * kn.astype(BF16).astype(F32), axis=1, keepdims=True) + math.log(
        len(DILATED_PATTERNS))
    m = jnp.maximum(jnp.max(s_c, axis=1, keepdims=True), s_n)
    p_c = jnp.exp(s_c - m)
    p_n = jnp.exp(s_n - m)
    den = jnp.sum(p_c, axis=1, keepdims=True) + p_n
    o_all = lax.dot_general(p_c.astype(BF16), vc, (((1,), (1,)), ((), ())), preferred_element_type=F32)
    o_all = (o_all + p_n * vn) / den
    o_ref[0] = jnp.sum(jnp.where(own, o_all, 0.0), axis=0, keepdims=True)


def _attn_step(q, k_new, v_new, k_cache, v_cache, layer):
    n, n_buf = k_cache.shape[1], k_cache.shape[-1]
    w = ATT_WIDTH
    dist = n_buf - np.arange(n_buf)
    bias = _dist_bias(dist)[None, :]
    vec = pl.BlockSpec((1, 1, w), lambda i: (i, 0, 0))
    buf = pl.BlockSpec((1, 1, ATT_HEADS, HEAD_DIM, n_buf), lambda i: (layer, i, 0, 0, 0))
    o = pl.pallas_call(
        _attn_step_kernel,
        grid=(n,),
        in_specs=[vec, vec, vec, buf, buf, pl.BlockSpec((1, n_buf), lambda i: (0, 0))],
        out_specs=vec,
        out_shape=jax.ShapeDtypeStruct((n, 1, w), F32),
        compiler_params=_cparams("parallel"),
        name="attn_step",
    )(q.reshape(n, 1, w), k_new.reshape(n, 1, w), v_new.reshape(n, 1, w), k_cache, v_cache, jnp.asarray(bias))
    return o.reshape(n, w)


def _gelu_tanh(x):
    return 0.5 * x * (1.0 + jnp.tanh(math.sqrt(2.0 / math.pi) * (x + 0.044715 * (x * x * x))))


def _mix_kernel(x_ref, ys_ref, yr_ref, bonus_ref, g_ref, ya_ref, wglu_ref, bglu_ref, gng_ref, gnb_ref,
                havg_ref, wout_ref, lng_ref, lnb_ref, o_ref, *, alpha):
    ys = _gelu_tanh(ys_ref[...])
    ya = ys * _sigmoid(_bdot(ys, wglu_ref[...]) + bglu_ref[...])
    yr = yr_ref[...]
    havg = havg_ref[...]
    mean = _dot2_exact_rhs(yr, havg) * (1.0 / HEAD_DIM)
    yc = yr - mean
    var = _dot2_exact_rhs(yc * yc, havg) * (1.0 / HEAD_DIM)
    yb = (yc * lax.rsqrt(var + RW_GN_EPS) * gng_ref[...] + gnb_ref[...] + bonus_ref[...]) * g_ref[...]
    wout = wout_ref[...]
    o1, o2 = S5_WIDTH, S5_WIDTH + RW_WIDTH
    mix = _bdot(ya, wout[:o1]) + _bdot(yb, wout[o1:o2]) + _bdot(ya_ref[...], wout[o2:])
    o_ref[...] = _layer_norm(alpha * x_ref[...] + mix, lng_ref[...], lnb_ref[...])


def _mix(x, y_s5, y_rw, bonus, gate, y_att, lp, alpha):
    t = x.shape[0]
    tm = _row_tile(t, 512)
    tile = lambda n: pl.BlockSpec((tm, n), lambda i: (i, 0))
    full = lambda a: pl.BlockSpec(a.shape, lambda i: (0,) * a.ndim)
    params = [lp['s5_w_glu'].astype(BF16), lp['s5_b_glu'].reshape(1, -1), lp['rw_gn_g'].reshape(1, -1),
              lp['rw_gn_b'].reshape(1, -1), _head_sum_matrix(RW_WIDTH), lp['w_out'].astype(BF16),
              lp['ln_g'][0].reshape(1, -1), lp['ln_b'][0].reshape(1, -1)]
    return pl.pallas_call(
        functools.partial(_mix_kernel, alpha=alpha),
        grid=(t // tm,),
        in_specs=[tile(D_MODEL), tile(S5_WIDTH), tile(RW_WIDTH), tile(RW_WIDTH), tile(RW_WIDTH),
                  tile(ATT_WIDTH)] + [full(a) for a in params],
        out_specs=tile(D_MODEL),
        out_shape=jax.ShapeDtypeStruct((t, D_MODEL), F32),
        compiler_params=_cparams("parallel"),
        name="mix",
    )(x, y_s5, y_rw, bonus, gate, y_att, *params)


def _rank_select(vals, n_rows, keep):
    ridx = lax.broadcasted_iota(jnp.int32, vals.shape, 0)
    cnt = jnp.zeros(vals.shape, jnp.int32)
    for j in range(n_rows):
        vj = vals[j:j + 1, :]
        beats = jnp.where(vj > vals, 1, jnp.where(vj == vals, jnp.where(ridx > j, 1, 0), 0))
        cnt = cnt + beats
    return cnt < keep


def _router_kernel(x_ref, wt_ref, bias_ref, gates_ref, gscore_scr, ekeep_scr):
    e, ng = N_EXPERTS, N_EXPERT_GROUPS
    per = e // ng
    logits = lax.dot_general(wt_ref[...].astype(BF16), x_ref[...].astype(BF16), (((1,), (1,)), ((), ())),
                             preferred_element_type=F32)
    scores = _sigmoid(logits)
    sel = scores + bias_ref[:, 0:1]
    t = sel.shape[1]
    pos = lax.broadcasted_iota(jnp.int32, (per, t), 0)
    for gi in range(ng):
        grp = sel[gi * per:(gi + 1) * per]
        m1 = jnp.max(grp, axis=0, keepdims=True)
        first = jnp.min(jnp.where(grp == m1, pos, per), axis=0, keepdims=True)
        m2 = jnp.max(jnp.where(pos == first, NEG_INF, grp), axis=0, keepdims=True)
        gscore_scr[gi:gi + 1, :] = m1 + m2
    gkeep = jnp.where(_rank_select(gscore_scr[...], ng, TOPK_GROUPS), 1.0, 0.0)
    for gi in range(ng):
        ekeep_scr[gi * per:(gi + 1) * per, :] = jnp.broadcast_to(gkeep[gi:gi + 1], (per, t))
    masked = jnp.where(ekeep_scr[...] > 0.0, sel, NEG_INF)
    chosen = _rank_select(masked, e, TOP_K)
    w = jnp.where(chosen, scores, 0.0)
    gates_ref[...] = w / jnp.sum(w, axis=0, keepdims=True) * ROUTED_SCALE


def _router(x, w_router, router_bias):
    t = x.shape[0]
    tm = _row_tile(t, 512)
    e = N_EXPERTS
    gates_t = pl.pallas_call(
        _router_kernel,
        grid=(t // tm,),
        in_specs=[pl.BlockSpec((tm, D_MODEL), lambda i: (i, 0)),
                  pl.BlockSpec((e, D_MODEL), lambda i: (0, 0)),
                  pl.BlockSpec((e, 128), lambda i: (0, 0))],
        out_specs=pl.BlockSpec((e, tm), lambda i: (0, i)),
        out_shape=jax.ShapeDtypeStruct((e, t), F32),
        scratch_shapes=[pltpu.VMEM((N_EXPERT_GROUPS, tm), F32), pltpu.VMEM((e, tm), F32)],
        compiler_params=_cparams("parallel"),
        name="router",
    )(x, w_router.T, jnp.broadcast_to(router_bias.reshape(e, 1), (e, 128)))
    return gates_t.T


def _moe_kernel(x_ref, gates_ref, wg_ref, wu_ref, wd_ref, sg_ref, su_ref, sd_ref, lng_ref, lnb_ref,
                o_ref, xb_scr, gsplit_scr, acc_scr, *, alpha):
    step = pl.program_id(1)
    per = wg_ref.shape[1]

    @pl.when(step == 0)
    def _():
        xb = x_ref[...].astype(BF16)
        xb_scr[...] = xb
        hi, lo = _split(gates_ref[...])
        gsplit_scr[...] = jnp.concatenate([hi, lo], axis=1)
        h = _silu(jnp.dot(xb, sg_ref[0].astype(BF16), preferred_element_type=F32)) * jnp.dot(
            xb, su_ref[0].astype(BF16), preferred_element_type=F32)
        acc_scr[...] = jnp.dot(h.astype(BF16), sd_ref[0].astype(BF16), preferred_element_type=F32)

    xb = xb_scr[...]
    row = lax.broadcasted_iota(jnp.int32, (2 * N_EXPERTS, EXPERT_FF), 0) % N_EXPERTS
    ys, gates = [], []
    for j in range(per):
        h = _silu(jnp.dot(xb, wg_ref[0, j].astype(BF16), preferred_element_type=F32)) * jnp.dot(
            xb, wu_ref[0, j].astype(BF16), preferred_element_type=F32)
        ys.append(jnp.dot(h.astype(BF16), wd_ref[0, j].astype(BF16), preferred_element_type=F32))
        pick = jnp.where(row == step * per + j, 1.0, 0.0).astype(BF16)
        gates.append(jnp.dot(gsplit_scr[...], pick, preferred_element_type=F32))
    for c in range(D_MODEL // EXPERT_FF):
        cols = slice(c * EXPERT_FF, (c + 1) * EXPERT_FF)
        acc_scr[:, cols] += sum(ys[j][:, cols] * gates[j] for j in range(per))

    @pl.when(step == pl.num_programs(1) - 1)
    def _():
        o_ref[...] = _layer_norm(alpha * x_ref[...] + acc_scr[...], lng_ref[...], lnb_ref[...])


def _moe(x, gates, w, layer, alpha):
    t = x.shape[0]
    tm = _row_tile(t, 1024)
    d, f, e = D_MODEL, EXPERT_FF, N_EXPERTS
    full = lambda a: pl.BlockSpec(a.shape, lambda i, j: (0,) * a.ndim)
    lng, lnb = w['ln_g'][layer, 1].reshape(1, -1), w['ln_b'][layer, 1].reshape(1, -1)
    shared = lambda r, c: pl.BlockSpec((1, r, c), lambda i, j: (layer, 0, 0))
    per = MOE_EXPERTS_PER_STEP
    return pl.pallas_call(
        functools.partial(_moe_kernel, alpha=alpha),
        grid=(t // tm, e // per),
        in_specs=[pl.BlockSpec((tm, d), lambda i, j: (i, 0)),
                  pl.BlockSpec((tm, e), lambda i, j: (i, 0)),
                  pl.BlockSpec((1, per, d, f), lambda i, j: (layer, j, 0, 0)),
                  pl.BlockSpec((1, per, d, f), lambda i, j: (layer, j, 0, 0)),
                  pl.BlockSpec((1, per, f, d), lambda i, j: (layer, j, 0, 0)),
                  shared(d, f), shared(d, f), shared(f, d), full(lng), full(lnb)],
        out_specs=pl.BlockSpec((tm, d), lambda i, j: (i, 0)),
        out_shape=jax.ShapeDtypeStruct((t, d), F32),
        scratch_shapes=[pltpu.VMEM((tm, d), BF16), pltpu.VMEM((tm, 2 * e), BF16), pltpu.VMEM((tm, d), F32)],
        compiler_params=_cparams("parallel", "arbitrary"),
        name="moe",
    )(x, gates, w['expert_w_gate'], w['expert_w_up'], w['expert_w_down'],
      w['shared_w_gate'], w['shared_w_up'], w['shared_w_down'], lng, lnb)


def _cache_shift_kernel(c_ref, n_ref, o_ref):
    hh, dd, n = c_ref.shape[2:]
    c = c_ref[0, 0].reshape(hh * dd, n)
    lane = lax.broadcasted_iota(jnp.int32, c.shape, 1)
    shifted = pltpu.roll(c, n - 1, axis=1)
    o_ref[0, 0] = jnp.where(lane == n - 1, n_ref[0, 0], shifted).reshape(hh, dd, n)


def _cache_shift(cache, new_rows):
    depth, n, hh, dd, buf = cache.shape
    blk = pl.BlockSpec((1, 1, hh, dd, buf), lambda l, b: (l, b, 0, 0, 0))
    col = pl.BlockSpec((1, 1, hh * dd, 1), lambda l, b: (l, b, 0, 0))
    return pl.pallas_call(
        _cache_shift_kernel,
        grid=(depth, n),
        in_specs=[blk, col],
        out_specs=blk,
        out_shape=jax.ShapeDtypeStruct(cache.shape, cache.dtype),
        compiler_params=_cparams("parallel", "parallel"),
        name="cache_shift",
    )(cache, new_rows.reshape(depth, n, hh * dd, 1))


def _trunk_layer(x, bsz, seq, pos0, h0_re, h0_im, rw_s0, rw_prev, k_buf, v_buf, lp, s5p, rope_tables, alpha):
    t = bsz * seq
    u, u_bf16, p_rw, qkv = _proj(x, lp['w_in_bf16'])
    if seq == 1:
        ab_re, ab_im, bb_re, bb_im = s5p[7:]
        y_s5, h_re, h_im = _s5_step(u, h0_re, h0_im, ab_re, ab_im, bb_re, bb_im,
                                    lp['s5_c_re'], lp['s5_c_im'], lp['s5_d'])
    else:
        y_s5, h_re, h_im = _s5_scan(u, u_bf16, lp['s5_emb'], h0_re, h0_im, bsz, seq)
    (r, lw, k2, v, ah, b, gate, bonus), cols = _rw_prep(p_rw, rw_prev, lp, bsz, seq)
    y_rw, rw_s = _rw_scan(r, ah, lw, v, b, k2, cols, rw_s0, bsz, seq)
    rw_row = p_rw.reshape(bsz, seq, 4 * RW_WIDTH)[:, -1]
    q, k, vv = _rope(qkv, rope_tables)
    if seq == 1:
        y_att = _attn_step(q, k, vv, k_buf, v_buf, lp['layer'])
    else:
        assert k_buf is None
        y_att = _attn_prompt(q, k, vv, bsz, seq)
    keep = min(WIN_MAX, seq)
    k_new = k.reshape(bsz, seq, ATT_HEADS, HEAD_DIM)[:, seq - keep:]
    v_new = vv.reshape(bsz, seq, ATT_HEADS, HEAD_DIM)[:, seq - keep:]
    x1 = _mix(x, y_s5, y_rw, bonus, gate, y_att, lp, alpha)
    gates = _router(x1, lp['w_router'], lp['router_bias'])
    x2 = _moe(x1, gates, lp['all'], lp['layer'], alpha)
    return x2, (h_re, h_im, rw_s, rw_row, k_new, v_new)


def kernel(x_prompt, x_sample, state_s5_re, state_s5_im, state_rwkv, state_rwkv_shift, cache_attn_k, cache_attn_v, w_in, s5_lambda_re, s5_lambda_im, s5_b_re, s5_b_im, s5_c_re, s5_c_im, s5_d, s5_log_step, s5_w_glu, s5_b_glu, rw_mu, rw_w0, rw_w1, rw_w2, rw_a0, rw_a1, rw_a2, rw_g1, rw_g2, rw_k_k, rw_k_a, rw_r_k, rw_gn_g, rw_gn_b, w_out, ln_g, ln_b, w_router, router_bias, expert_w_gate, expert_w_up, expert_w_down, shared_w_gate, shared_w_up, shared_w_down):
    depth = w_in.shape[0]
    alpha = (2 * depth) ** 0.25
    bsz, seq, d = x_prompt.shape
    dbsz, dseq, _ = x_sample.shape
    past = cache_attn_k.shape[2]
    names = dict(s5_c_re=s5_c_re, s5_c_im=s5_c_im, s5_d=s5_d, s5_w_glu=s5_w_glu, s5_b_glu=s5_b_glu,
                 rw_mu=rw_mu, rw_w0=rw_w0, rw_w1=rw_w1, rw_w2=rw_w2, rw_a0=rw_a0, rw_a1=rw_a1, rw_a2=rw_a2,
                 rw_g1=rw_g1, rw_g2=rw_g2, rw_k_k=rw_k_k, rw_k_a=rw_k_a, rw_r_k=rw_r_k, rw_gn_g=rw_gn_g,
                 rw_gn_b=rw_gn_b, w_out=w_out, ln_g=ln_g, ln_b=ln_b, w_router=w_router,
                 router_bias=router_bias, expert_w_gate=expert_w_gate, expert_w_up=expert_w_up,
                 expert_w_down=expert_w_down, shared_w_gate=shared_w_gate, shared_w_up=shared_w_up,
                 shared_w_down=shared_w_down)
    assert dseq == 1
    tables_p = _rope_tables(jnp.arange(seq))
    tables_s = _rope_tables(jnp.full((dbsz,), PAST_LEN))
    yp = x_prompt.reshape(bsz * seq, d)
    ys = x_sample.reshape(dbsz * dseq, d)
    k_cache = jnp.transpose(cache_attn_k, (0, 1, 3, 4, 2))
    v_cache = jnp.transpose(cache_attn_v, (0, 1, 3, 4, 2))
    new_p, new_s = [], []
    for l in range(depth):
        lp = {k: v[l] for k, v in names.items() if not k.startswith(('expert_w', 'shared_w'))}
        lp['all'], lp['layer'] = names, l
        lp['w_in_bf16'] = w_in[l].astype(BF16)
        s5p = _s5_prep(s5_lambda_re[l], s5_lambda_im[l], s5_log_step[l], s5_b_re[l], s5_b_im[l],
                       s5_c_re[l], s5_c_im[l])
        lp['s5_emb'] = _s5_embed(s5p[:7], s5_d[l])
        zs = jnp.zeros((bsz, S5_GROUPS, S5_STATE), F32)
        z_s = jnp.zeros((bsz, RW_HEADS, HEAD_DIM, HEAD_DIM), F32)
        zrow = jnp.zeros((bsz, 4 * RW_WIDTH), F32)
        yp, st_p = _trunk_layer(yp, bsz, seq, 0, zs, zs, z_s, zrow, None, None, lp, s5p, tables_p, alpha)
        ys, st_s = _trunk_layer(ys, dbsz, dseq, past, state_s5_re[l], state_s5_im[l], state_rwkv[l],
                                state_rwkv_shift[l], k_cache, v_cache, lp, s5p, tables_s, alpha)
        new_p.append(st_p)
        new_s.append(st_s)
    stack = lambda sts: [jnp.stack([s[i] for s in sts], 0) for i in range(6)]
    out_p, out_s = stack(new_p), stack(new_s)
    assert past == WIN_MAX
    to_rows = lambda c: jnp.transpose(c, (0, 1, 4, 2, 3))
    out_s[4] = to_rows(_cache_shift(k_cache, out_s[4].reshape(depth, dbsz, ATT_WIDTH)))
    out_s[5] = to_rows(_cache_shift(v_cache, out_s[5].reshape(depth, dbsz, ATT_WIDTH)))
    return (yp.reshape(bsz, seq, d), ys.reshape(dbsz, dseq, d), *out_p, *out_s)
```

```python
import functools
import math

import numpy as np
import jax
import jax.numpy as jnp
from jax import lax
from jax.experimental import pallas as pl
from jax.experimental.pallas import tpu as pltpu

F32 = jnp.float32
BF16 = jnp.bfloat16

D_MODEL = 1024
PAST_LEN = 8192
HEAD_DIM = 64
S5_WIDTH = 256
S5_GROUP = 16
S5_GROUPS = 16
S5_STATE = 64
RW_WIDTH = 384
RW_HEADS = 6
ATT_WIDTH = 384
ATT_HEADS = 6
N_IN = S5_WIDTH + 4 * RW_WIDTH + 3 * ATT_WIDTH
RW_GN_EPS = 64e-5
DILATED_PATTERNS = ((128, 1), (512, 4), (2048, 16))
WIN_MAX = 2048
ATT_SCALE = HEAD_DIM ** -0.5
ROPE_THETA = 500000.0
ROPE_DIM = HEAD_DIM // 4
NEG_INF = -1e30
N_EXPERTS = 64
N_EXPERT_GROUPS = 8
TOPK_GROUPS = 4
TOP_K = 8
EXPERT_FF = 256
ROUTED_SCALE = 2.5
LN_EPS = 1e-5

S5_CHUNK = 16
S5_STATE_BLOCK = 256
S5_OUT_BLOCK = 512
RW_CHUNK = 64
RW_SEQS_PER_STEP = 2
MOE_TOKEN_TILE = 1024
ATT_TILE = 2048
ATT_SUB = 128
VMEM_LIMIT = 56 * 1024 * 1024


def _cparams(*sem):
    return pltpu.CompilerParams(dimension_semantics=sem, vmem_limit_bytes=VMEM_LIMIT)


def _bdot(a, b):
    return jnp.dot(a.astype(BF16), b.astype(BF16), preferred_element_type=F32)


def _split(x):
    hi = x.astype(BF16)
    lo = (x - hi.astype(F32)).astype(BF16)
    return hi, lo


def _dot3(a, b):
    ah, al = _split(a)
    bh, bl = _split(b)
    d = functools.partial(jnp.dot, preferred_element_type=F32)
    return d(ah, bh) + (d(ah, bl) + d(al, bh))


def _dot2_exact_rhs(a, b_bf16):
    ah, al = _split(a)
    d = functools.partial(jnp.dot, preferred_element_type=F32)
    return d(ah, b_bf16) + d(al, b_bf16)


def _sigmoid(x):
    return 1.0 / (1.0 + jnp.exp(-x))


def _silu(x):
    return x * _sigmoid(x)


def _layer_norm(x, g, b):
    mu = jnp.mean(x, axis=-1, keepdims=True)
    xc = x - mu
    var = jnp.mean(xc * xc, axis=-1, keepdims=True)
    return xc * lax.rsqrt(var + LN_EPS) * g + b


def _row_tile(t, pref):
    tm = min(t, pref)
    assert t % tm == 0
    return tm


def _proj_kernel(x_ref, w_ref, u_ref, ub_ref, rw_ref, qkv_ref):
    p = jnp.dot(x_ref[...].astype(BF16), w_ref[...], preferred_element_type=F32)
    o1 = S5_WIDTH
    o2 = o1 + 4 * RW_WIDTH
    u_ref[...] = p[:, :o1]
    ub_ref[...] = p[:, :o1].astype(BF16)
    rw_ref[...] = p[:, o1:o2]
    qkv_ref[...] = p[:, o2:]


def _proj(x, w_bf16):
    t = x.shape[0]
    tm = _row_tile(t, 512)
    return pl.pallas_call(
        _proj_kernel,
        grid=(t // tm,),
        in_specs=[pl.BlockSpec((tm, D_MODEL), lambda i: (i, 0)),
                  pl.BlockSpec((D_MODEL, N_IN), lambda i: (0, 0))],
        out_specs=[pl.BlockSpec((tm, S5_WIDTH), lambda i: (i, 0)),
                   pl.BlockSpec((tm, S5_WIDTH), lambda i: (i, 0)),
                   pl.BlockSpec((tm, 4 * RW_WIDTH), lambda i: (i, 0)),
                   pl.BlockSpec((tm, 3 * ATT_WIDTH), lambda i: (i, 0))],
        out_shape=[jax.ShapeDtypeStruct((t, S5_WIDTH), F32),
                   jax.ShapeDtypeStruct((t, S5_WIDTH), BF16),
                   jax.ShapeDtypeStruct((t, 4 * RW_WIDTH), F32),
                   jax.ShapeDtypeStruct((t, 3 * ATT_WIDTH), F32)],
        compiler_params=_cparams("parallel"),
        name="proj",
    )(x, w_bf16)


def _s5_discretize(lr, li, log_step):
    dt = jnp.exp(log_step)
    mag = jnp.exp(lr * dt)
    ab_re = mag * jnp.cos(li * dt)
    ab_im = mag * jnp.sin(li * dt)
    den = lr * lr + li * li
    cf_re = ((ab_re - 1.0) * lr + ab_im * li) / den
    cf_im = (ab_im * lr - (ab_re - 1.0) * li) / den
    return dt, ab_re, ab_im, cf_re, cf_im


def _s5_power(lr, li, dt, n):
    mag = jnp.exp(n * (lr * dt))
    ang = n * (li * dt)
    return mag * jnp.cos(ang), mag * jnp.sin(ang)


def _s5_prep_kernel(lr_ref, li_ref, ls_ref, lrc_ref, lic_ref, lsc_ref, bt_re_ref, bt_im_ref,
                    ct_re_ref, ct_im_ref,
                    kmat_ref, bout_re_ref, bout_im_ref, win_re_ref, win_im_ref,
                    aq_re_ref, aq_im_ref, ab_re_ref, ab_im_ref, bb_re_ref, bb_im_ref):
    q = S5_CHUNK
    n_rows = q * S5_GROUP
    lr, li, ls = lr_ref[0], li_ref[0], ls_ref[0]
    dt, ab_re, ab_im, cf_re, cf_im = _s5_discretize(lr, li, ls)
    ab_re_ref[0] = ab_re
    ab_im_ref[0] = ab_im
    aq_re, aq_im = _s5_power(lr, li, dt, jnp.float32(q))
    aq_re_ref[0] = aq_re
    aq_im_ref[0] = aq_im
    bt_re, bt_im = bt_re_ref[0], bt_im_ref[0]
    bb_re = cf_re * bt_re - cf_im * bt_im
    bb_im = cf_re * bt_im + cf_im * bt_re
    bb_re_ref[0] = bb_re[:S5_GROUP]
    bb_im_ref[0] = bb_im[:S5_GROUP]
    step = lax.broadcasted_iota(jnp.int32, (n_rows, S5_STATE), 0) // S5_GROUP
    pw_re, pw_im = _s5_power(lr, li, dt, (q - 1 - step).astype(F32))
    bout_re_ref[0] = (bb_re * pw_re - bb_im * pw_im).astype(BF16)
    bout_im_ref[0] = (bb_re * pw_im + bb_im * pw_re).astype(BF16)
    lrc, lic, lsc = lrc_ref[0], lic_ref[0], lsc_ref[0]
    dtc, abc_re, abc_im, _, _ = _s5_discretize(lrc, lic, lsc)
    lag = (lax.broadcasted_iota(jnp.int32, (S5_STATE, n_rows), 1) // S5_GROUP).astype(F32)
    p0_re, p0_im = _s5_power(lrc, lic, dtc, lag)
    ct_re, ct_im = ct_re_ref[0], ct_im_ref[0]
    e0_re = ct_re * p0_re - ct_im * p0_im
    e0_im = ct_re * p0_im + ct_im * p0_re
    e1_re = e0_re * abc_re - e0_im * abc_im
    e1_im = e0_re * abc_im + e0_im * abc_re
    win_re_ref[0] = e1_re.astype(BF16)
    win_im_ref[0] = (-e1_im).astype(BF16)
    g = _dot3(bb_re[:S5_GROUP], e0_re) - _dot3(bb_im[:S5_GROUP], e0_im)
    lane = lax.broadcasted_iota(jnp.int32, (S5_GROUP, n_rows), 1)
    blocks = []
    for i in range(q):
        if i == 0:
            blocks.append(g)
        else:
            shifted = pltpu.roll(g, S5_GROUP * i, axis=1)
            blocks.append(jnp.where(lane >= S5_GROUP * i, shifted, 0.0))
    kmat_ref[0] = jnp.concatenate(blocks, axis=0).astype(BF16)


def _s5_prep(lam_re, lam_im, log_step, b_re, b_im, c_re, c_im):
    g, p, c, q = S5_GROUPS, S5_STATE, S5_GROUP, S5_CHUNK
    n = q * c
    row = lambda a: a.reshape(g, 1, p)
    col = lambda a: jnp.broadcast_to(a.reshape(g, p, 1), (g, p, n))
    ls_row = jnp.broadcast_to(log_step.reshape(g, 1, 1), (g, 1, p))
    ls_col = jnp.broadcast_to(log_step.reshape(g, 1, 1), (g, p, n))
    bt = lambda a: jnp.tile(jnp.swapaxes(a, 1, 2), (1, q, 1))
    ct = lambda a: jnp.tile(jnp.swapaxes(a, 1, 2), (1, 1, q))
    spec = lambda *s: pl.BlockSpec((1,) + s, lambda i: (i, 0, 0))
    sds = lambda s, d: jax.ShapeDtypeStruct((g,) + s, d)
    return pl.pallas_call(
        _s5_prep_kernel,
        grid=(g,),
        in_specs=[spec(1, p), spec(1, p), spec(1, p), spec(p, n), spec(p, n), spec(p, n),
                  spec(n, p), spec(n, p), spec(p, n), spec(p, n)],
        out_specs=[spec(n, n), spec(n, p), spec(n, p), spec(p, n), spec(p, n),
                   spec(1, p), spec(1, p), spec(1, p), spec(1, p), spec(c, p), spec(c, p)],
        out_shape=[sds((n, n), BF16), sds((n, p), BF16), sds((n, p), BF16), sds((p, n), BF16),
                   sds((p, n), BF16), sds((1, p), F32), sds((1, p), F32), sds((1, p), F32),
                   sds((1, p), F32), sds((c, p), F32), sds((c, p), F32)],
        compiler_params=_cparams("parallel"),
        name="s5_prep",
    )(row(lam_re), row(lam_im), ls_row, col(lam_re), col(lam_im), ls_col,
      bt(b_re), bt(b_im), ct(c_re), ct(c_im))


def _s5_embed(mats, d_skip):
    g, p, c, q = S5_GROUPS, S5_STATE, S5_GROUP, S5_CHUNK
    kmat, bout_re, bout_im, win_re, win_im, aq_re, aq_im = mats
    n = q * g * c
    by_step = lambda m: jnp.swapaxes(m.reshape(g, q, c, m.shape[-1]), 0, 1).reshape(n, m.shape[-1])
    src = jnp.arange(q * c)
    dst = jnp.arange(n)
    spread = ((src[:, None] // c == dst[None, :] // (g * c)) & (src[:, None] % c == dst[None, :] % c)).astype(BF16)
    over_groups = lambda m: jnp.dot(m, spread, preferred_element_type=BF16)
    row_group = lambda rows, per: (jnp.arange(rows) // per) % g
    col_group_out = (dst // c) % g
    col_group_state = jnp.arange(g * p) // p
    rows_in = row_group(n, c)
    rows_state = jnp.arange(g * p) // p
    keep = lambda m, rg, cg: jnp.where(rg[:, None] == cg[None, :], m, jnp.zeros((), m.dtype))
    kf = keep(over_groups(by_step(kmat)), rows_in, col_group_out)
    bo = lambda m: keep(jnp.tile(by_step(m), (1, g)), rows_in, col_group_state)
    wi = lambda m: keep(over_groups(m.reshape(g * p, q * c)), rows_state, col_group_out)
    d_full = jnp.tile(d_skip.reshape(1, g * c), (1, q))
    return (kf, bo(bout_re), bo(bout_im), wi(win_re), wi(win_im),
            aq_re.reshape(1, g * p), aq_im.reshape(1, g * p), d_full)


def _s5_state_kernel(xb_ref, bout_re_ref, bout_im_ref, aq_re_ref, aq_im_ref, h0_re_ref, h0_im_ref,
                     hin_re_ref, hin_im_ref, hf_re_ref, hf_im_ref, s_re, s_im, e_re, e_im, *, bsz, n_chunks):
    xb = xb_ref[...]
    s_re[...] = jnp.dot(xb, bout_re_ref[...], preferred_element_type=F32)
    s_im[...] = jnp.dot(xb, bout_im_ref[...], preferred_element_type=F32)
    ar, ai = aq_re_ref[...], aq_im_ref[...]

    def body(k, carry):
        new = []
        for b in range(bsz):
            hr, hi = carry[2 * b], carry[2 * b + 1]
            r = b * n_chunks + k
            e_re[pl.ds(r, 1), :] = hr
            e_im[pl.ds(r, 1), :] = hi
            new.append(ar * hr - ai * hi + s_re[pl.ds(r, 1), :])
            new.append(ar * hi + ai * hr + s_im[pl.ds(r, 1), :])
        return tuple(new)

    init = []
    for b in range(bsz):
        init.append(h0_re_ref[b:b + 1, :])
        init.append(h0_im_ref[b:b + 1, :])
    fin = lax.fori_loop(0, n_chunks, body, tuple(init))
    for b in range(bsz):
        hf_re_ref[b:b + 1, :] = fin[2 * b]
        hf_im_ref[b:b + 1, :] = fin[2 * b + 1]
    hin_re_ref[...] = e_re[...].astype(BF16)
    hin_im_ref[...] = e_im[...].astype(BF16)


def _s5_out_kernel(xb_ref, x_ref, kf_ref, hin_re_ref, hin_im_ref, win_re_ref, win_im_ref, d_ref, y_ref):
    y = jnp.dot(xb_ref[...], kf_ref[...], preferred_element_type=F32)
    y = y + jnp.dot(hin_re_ref[...], win_re_ref[...], preferred_element_type=F32)
    y = y + jnp.dot(hin_im_ref[...], win_im_ref[...], preferred_element_type=F32)
    y_ref[...] = y + x_ref[...] * d_ref[...]


def _s5_scan(u, u_bf16, emb, h0_re, h0_im, bsz, seq):
    g, p, c, q = S5_GROUPS, S5_STATE, S5_GROUP, S5_CHUNK
    kf, bout_re, bout_im, win_re, win_im, aq_re, aq_im, d_full = emb
    n, gp = q * g * c, g * p
    n_chunks = seq // q
    rows = bsz * n_chunks
    x = u.reshape(rows, n)
    xb = u_bf16.reshape(rows, n)
    sb = S5_STATE_BLOCK
    col = lambda r, w: pl.BlockSpec((r, w), lambda i: (0, i))
    fix = lambda r, w: pl.BlockSpec((r, w), lambda i: (0, 0))
    hin_re, hin_im, hf_re, hf_im = pl.pallas_call(
        functools.partial(_s5_state_kernel, bsz=bsz, n_chunks=n_chunks),
        grid=(gp // sb,),
        in_specs=[fix(rows, n), col(n, sb), col(n, sb), col(1, sb), col(1, sb), col(bsz, sb), col(bsz, sb)],
        out_specs=[col(rows, sb), col(rows, sb), col(bsz, sb), col(bsz, sb)],
        out_shape=[jax.ShapeDtypeStruct((rows, gp), BF16), jax.ShapeDtypeStruct((rows, gp), BF16),
                   jax.ShapeDtypeStruct((bsz, gp), F32), jax.ShapeDtypeStruct((bsz, gp), F32)],
        scratch_shapes=[pltpu.VMEM((rows, sb), F32)] * 4,
        compiler_params=_cparams("parallel"),
        name="s5_state",
    )(xb, bout_re, bout_im, aq_re, aq_im, h0_re.reshape(bsz, gp), h0_im.reshape(bsz, gp))
    ob = S5_OUT_BLOCK
    y = pl.pallas_call(
        _s5_out_kernel,
        grid=(n // ob,),
        in_specs=[fix(rows, n), col(rows, ob), col(n, ob), fix(rows, gp), fix(rows, gp), col(gp, ob),
                  col(gp, ob), col(1, ob)],
        out_specs=col(rows, ob),
        out_shape=jax.ShapeDtypeStruct((rows, n), F32),
        compiler_params=_cparams("parallel"),
        name="s5_out",
    )(xb, x, kf, hin_re, hin_im, win_re, win_im, d_full)
    return (y.reshape(bsz * seq, g * c), hf_re.reshape(bsz, g, p), hf_im.reshape(bsz, g, p))


def _s5_step_kernel(u_ref, h0_re_ref, h0_im_ref, a_re_ref, a_im_ref, bb_re_ref, bb_im_ref,
                    cc_re_ref, cc_im_ref, d_ref, y_ref, h_re_ref, h_im_ref):
    u = u_ref[...]
    ub = u.astype(BF16)
    a_re, a_im = a_re_ref[...], a_im_ref[...]
    h0r, h0i = h0_re_ref[...], h0_im_ref[...]
    hr = a_re * h0r - a_im * h0i + jnp.dot(ub, bb_re_ref[...], preferred_element_type=F32)
    hi = a_re * h0i + a_im * h0r + jnp.dot(ub, bb_im_ref[...], preferred_element_type=F32)
    h_re_ref[...] = hr
    h_im_ref[...] = hi
    y = (jnp.dot(hr.astype(BF16), cc_re_ref[...], preferred_element_type=F32)
         - jnp.dot(hi.astype(BF16), cc_im_ref[...], preferred_element_type=F32))
    y_ref[...] = y + u * d_ref[...]


def _block_diag(blocks):
    g, r, c = blocks.shape
    eye = jnp.eye(g, dtype=blocks.dtype)
    return (blocks[:, :, None, :] * eye[:, None, :, None]).reshape(g * r, g * c)


def _s5_step(u, h0_re, h0_im, ab_re, ab_im, bb_re, bb_im, c_re, c_im, d_skip):
    n = u.shape[0]
    gp = S5_GROUPS * S5_STATE
    bbd_re = _block_diag(bb_re).astype(BF16)
    bbd_im = _block_diag(bb_im).astype(BF16)
    ccd_re = _block_diag(jnp.swapaxes(c_re, 1, 2)).astype(BF16)
    ccd_im = _block_diag(jnp.swapaxes(c_im, 1, 2)).astype(BF16)
    y, h_re, h_im = pl.pallas_call(
        _s5_step_kernel,
        out_shape=[jax.ShapeDtypeStruct((n, S5_WIDTH), F32),
                   jax.ShapeDtypeStruct((n, gp), F32),
                   jax.ShapeDtypeStruct((n, gp), F32)],
        compiler_params=pltpu.CompilerParams(vmem_limit_bytes=VMEM_LIMIT),
        name="s5_step",
    )(u, h0_re.reshape(n, gp), h0_im.reshape(n, gp), ab_re.reshape(1, gp), ab_im.reshape(1, gp),
      bbd_re, bbd_im, ccd_re, ccd_im, d_skip.reshape(1, S5_WIDTH))
    return y, h_re.reshape(n, S5_GROUPS, S5_STATE), h_im.reshape(n, S5_GROUPS, S5_STATE)


def _rw_prep_kernel(p_ref, tail_ref, prev_ref, mu_ref, w0_ref, w1_ref, w2_ref, a0_ref, a1_ref, a2_ref,
                    g1_ref, g2_ref, kk_ref, ka_ref, rk_ref, hsum_ref,
                    r_out, lw_out, k_out, v_out, ah_out, b_out, g_out, bonus_out, *col_outs, seq_tiles):
    w = RW_WIDTH
    p = p_ref[...]
    if seq_tiles:
        first = pl.program_id(0) % seq_tiles == 0
        before = jnp.where(first, prev_ref[0], tail_ref[7:8, :])
        rowid = lax.broadcasted_iota(jnp.int32, p.shape, 0)
        p_prev = jnp.where(rowid == 0, before, pltpu.roll(p, 1, axis=0))
    else:
        p_prev = prev_ref[...]
    dp = p_prev - p
    r_in, k_in, v_in, z = p[:, :w], p[:, w:2 * w], p[:, 2 * w:3 * w], p[:, 3 * w:]
    dr, dk, dv, dz = dp[:, :w], dp[:, w:2 * w], dp[:, 2 * w:3 * w], dp[:, 3 * w:]
    mu = mu_ref[...]
    r = r_in + dr * mu[0:1]
    xw = z + dz * mu[1:2]
    k = k_in + dk * mu[2:3]
    v = v_in + dv * mu[3:4]
    xa = z + dz * mu[4:5]
    xg = z + dz * mu[5:6]
    t = w0_ref[...] + _bdot(jnp.tanh(_bdot(xw, w1_ref[...])), w2_ref[...])
    nt = -t
    softplus = jnp.maximum(nt, 0.0) + jnp.log(1.0 + jnp.exp(-jnp.abs(nt)))
    w_log = -softplus - 0.5
    lw = -jnp.exp(w_log)
    lw_out[...] = lw
    a = _sigmoid(a0_ref[...] + _bdot(_bdot(xa, a1_ref[...]), a2_ref[...]))
    g_out[...] = _bdot(_sigmoid(_bdot(xg, g1_ref[...])), g2_ref[...])
    kk = k * kk_ref[...]
    ss = _dot2_exact_rhs(kk * kk, hsum_ref[...])
    kk = kk * lax.rsqrt(jnp.maximum(ss, 1e-24))
    k2 = k * (1.0 + (a - 1.0) * ka_ref[...])
    r_out[...] = r
    k_out[...] = k2
    v_out[...] = v
    ah_out[...] = -kk
    b = kk * a
    b_out[...] = b
    bonus_out[...] = _dot2_exact_rhs(r * k2 * rk_ref[...], hsum_ref[...]) * v
    for arr, out in zip((b, k2, lw), col_outs):
        arr_t = arr.T
        for j in range(arr.shape[0] // RW_CHUNK):
            out[0, j] = arr_t[:, j * RW_CHUNK:(j + 1) * RW_CHUNK]


def _head_sum_matrix(width):
    idx = np.arange(width) // HEAD_DIM
    return jnp.asarray((idx[:, None] == idx[None, :]).astype(np.float32), dtype=BF16)


def _rw_prep(p, rw_prev, lp, bsz, seq):
    t = p.shape[0]
    w = RW_WIDTH
    if seq == 1:
        tm, seq_tiles = t, 0
        tail_spec = pl.BlockSpec((t, 4 * w), lambda i: (0, 0))
        prev_spec = pl.BlockSpec((t, 4 * w), lambda i: (0, 0))
        prev_arg = rw_prev
    else:
        tm = _row_tile(seq, 512)
        seq_tiles = seq // tm
        tail_spec = pl.BlockSpec((8, 4 * w), lambda i: (jnp.maximum(i * (tm // 8) - 1, 0), 0))
        prev_spec = pl.BlockSpec((1, 1, 4 * w), lambda i: (i // seq_tiles, 0, 0))
        prev_arg = rw_prev.reshape(bsz, 1, 4 * w)
    row = lambda a: a.reshape(1, w)
    full = lambda a: pl.BlockSpec(a.shape, lambda i: (0,) * a.ndim)
    params = [lp['rw_mu'], row(lp['rw_w0']), lp['rw_w1'].astype(BF16), lp['rw_w2'].astype(BF16),
              row(lp['rw_a0']), lp['rw_a1'].astype(BF16), lp['rw_a2'].astype(BF16),
              lp['rw_g1'].astype(BF16), lp['rw_g2'].astype(BF16), row(lp['rw_k_k']),
              row(lp['rw_k_a']), row(lp['rw_r_k']), _head_sum_matrix(w)]
    tile = lambda n: pl.BlockSpec((tm, n), lambda i: (i, 0))
    out_specs = [tile(w)] * 8
    out_shape = [jax.ShapeDtypeStruct((t, w), F32)] * 8
    if seq_tiles:
        assert tm % RW_CHUNK == 0
        cpt = tm // RW_CHUNK
        out_specs = out_specs + [pl.BlockSpec((1, cpt, w, RW_CHUNK),
                                              lambda i: (i // seq_tiles, i % seq_tiles, 0, 0))] * 3
        out_shape = out_shape + [jax.ShapeDtypeStruct((bsz, seq // RW_CHUNK, w, RW_CHUNK), F32)] * 3
    outs = pl.pallas_call(
        functools.partial(_rw_prep_kernel, seq_tiles=seq_tiles),
        grid=(t // tm,),
        in_specs=[tile(4 * w), tail_spec, prev_spec] + [full(a) for a in params],
        out_specs=out_specs,
        out_shape=out_shape,
        compiler_params=_cparams("parallel"),
        name="rw_prep",
    )(p, p, prev_arg, *params)
    return outs[:8], (tuple(outs[8:]) if seq_tiles else None)


def _rw_scan_kernel(r_ref, ah_ref, lw_ref, v_ref, bt_ref, kt_ref, lwt_ref, h0_ref,
                    y_ref, hf_ref, h_scr):
    c = RW_CHUNK
    nseq = r_ref.shape[0]
    ci = pl.program_id(1)

    @pl.when(ci == 0)
    def _():
        h_scr[...] = h0_ref[...]

    row = lax.broadcasted_iota(jnp.int32, (c, c), 0)
    col = lax.broadcasted_iota(jnp.int32, (c, c), 1)
    incl = row >= col
    strict = row > col
    tri_lo = jnp.where(incl, 1.0, 0.0).astype(BF16)
    tri_up = jnp.where(row <= col, 1.0, 0.0).astype(BF16)
    eye = jnp.where(row == col, 1.0, 0.0)
    d = functools.partial(jnp.dot, preferred_element_type=F32)

    def split3(x):
        hi, mid = _split(x)
        return hi, mid, (x - hi.astype(F32) - mid.astype(F32)).astype(BF16)

    cums, cums_t = [], []
    for s in range(nseq):
        p3 = split3(lw_ref[s])
        cums.append(d(tri_lo, p3[0]) + (d(tri_lo, p3[1]) + d(tri_lo, p3[2])))
        q3 = split3(lwt_ref[s, 0])
        cums_t.append(d(q3[0], tri_up) + (d(q3[1], tri_up) + d(q3[2], tri_up)))

    units = [(s, h) for s in range(nseq) for h in range(RW_HEADS)]
    sl = lambda h: slice(h * HEAD_DIM, (h + 1) * HEAD_DIM)
    each = lambda f: [f(s, h) for s, h in units]
    idx = range(len(units))
    cum = each(lambda s, h: cums[s][:, sl(h)])
    cum_t = each(lambda s, h: cums_t[s][sl(h), :])
    ar = each(lambda s, h: None)
    for i, (s, h) in enumerate(units):
        a_t = ah_ref[s, :, sl(h)] * jnp.exp(cum[i] - lw_ref[s, :, sl(h)])
        r_t = r_ref[s, :, sl(h)] * jnp.exp(cum[i])
        ar[i] = jnp.concatenate([a_t, r_t], axis=0).astype(BF16)
    w_inv_t = [jnp.exp(-cum_t[i]) for i in idx]
    cum_end = [cum_t[i][:, c - 1:c] for i in idx]
    dec_t = [jnp.exp(cum_end[i] - cum_t[i]) for i in idx]
    bt = each(lambda s, h: bt_ref[s, 0, sl(h), :])
    kt = each(lambda s, h: kt_ref[s, 0, sl(h), :])
    b_t = [(bt[i] * w_inv_t[i]).astype(BF16) for i in idx]
    k_t = [(kt[i] * w_inv_t[i]).astype(BF16) for i in idx]
    b_d = [(bt[i] * dec_t[i]).astype(BF16) for i in idx]
    k_d = [(kt[i] * dec_t[i]).astype(BF16) for i in idx]
    mb = [d(ar[i], b_t[i]) for i in idx]
    mk = [d(ar[i], k_t[i]) for i in idx]
    a_ab = [jnp.where(strict, mb[i][:c], 0.0) for i in idx]
    a_ak = [jnp.where(strict, mk[i][:c], 0.0).astype(BF16) for i in idx]
    a_rb = [jnp.where(incl, mb[i][c:], 0.0).astype(BF16) for i in idx]
    a_rk = [jnp.where(incl, mk[i][c:], 0.0).astype(BF16) for i in idx]
    pw = a_ab
    inv = [eye + a_ab[i] for i in idx]
    for _ in range(int(math.log2(c)) - 1):
        pw = [_bdot(pw[i], pw[i]) for i in idx]
        inv = [inv[i] + _bdot(inv[i], pw[i]) for i in idx]
    h0 = each(lambda s, h: h_scr[s, h])
    vb = each(lambda s, h: v_ref[s, :, sl(h)].astype(BF16))
    ar_h = [d(ar[i], h0[i].astype(BF16)) for i in idx]
    rhs = [ar_h[i][:c] + d(a_ak[i], vb[i]) for i in idx]
    ub = [_bdot(inv[i], rhs[i]).astype(BF16) for i in idx]
    y = [ar_h[i][c:] + d(a_rb[i], ub[i]) + d(a_rk[i], vb[i]) for i in idx]
    h_new = [jnp.exp(cum_end[i]) * h0[i] + d(b_d[i], ub[i]) + d(k_d[i], vb[i]) for i in idx]
    for i, (s, h) in enumerate(units):
        h_scr[s, h] = h_new[i]
    for s in range(nseq):
        y_ref[s] = jnp.concatenate([y[s * RW_HEADS + h] for h in range(RW_HEADS)], axis=1)

    for i, (s, h) in enumerate(units):
        hf_ref[s, h] = h_new[i]


def _rw_scan(r, ah, lw, v, b, k2, cols, s0, bsz, seq):
    w, c = RW_WIDTH, RW_CHUNK
    pad = (-seq) % c
    sp = seq + pad

    def rows(a):
        a = a.reshape(bsz, seq, w)
        return jnp.pad(a, ((0, 0), (0, pad), (0, 0))) if pad else a

    r3, ah3, lw3, v3 = (rows(a) for a in (r, ah, lw, v))
    n_chunks = sp // c
    if cols is None:
        tr = lambda a: jnp.swapaxes(rows(a).reshape(bsz, n_chunks, c, w), 2, 3)
        cols = (tr(b), tr(k2), tr(lw))
    h0 = jnp.swapaxes(s0, 2, 3)
    ns = RW_SEQS_PER_STEP
    assert bsz % ns == 0
    rspec = pl.BlockSpec((ns, c, w), lambda i, j: (i, j, 0))
    cspec = pl.BlockSpec((ns, 1, w, c), lambda i, j: (i, j, 0, 0))
    sspec = pl.BlockSpec((ns, RW_HEADS, HEAD_DIM, HEAD_DIM), lambda i, j: (i, 0, 0, 0))
    y, hf = pl.pallas_call(
        _rw_scan_kernel,
        grid=(bsz // ns, n_chunks),
        in_specs=[rspec, rspec, rspec, rspec, cspec, cspec, cspec, sspec],
        out_specs=[rspec, sspec],
        out_shape=[jax.ShapeDtypeStruct((bsz, sp, w), F32),
                   jax.ShapeDtypeStruct((bsz, RW_HEADS, HEAD_DIM, HEAD_DIM), F32)],
        scratch_shapes=[pltpu.VMEM((ns, RW_HEADS, HEAD_DIM, HEAD_DIM), F32)],
        compiler_params=_cparams("parallel", "arbitrary"),
        name="rw_scan",
    )(r3, ah3, lw3, v3, *cols, h0)
    return y[:, :seq].reshape(bsz * seq, w), jnp.swapaxes(hf, 2, 3)


def _rope_kernel(qkv_ref, cos_ref, sin_lo_ref, sin_hi_ref, q_ref, k_ref, v_ref):
    w = ATT_WIDTH
    half = ROPE_DIM // 2
    x = qkv_ref[...]
    rep = lambda t: jnp.concatenate([t] * (w // 128), axis=1)
    cos, s_lo, s_hi = rep(cos_ref[...]), rep(sin_lo_ref[...]), rep(sin_hi_ref[...])

    def rot(t):
        up = pltpu.roll(t, w - half, axis=1)
        dn = pltpu.roll(t, half, axis=1)
        return t * cos + up * s_lo + dn * s_hi

    q_ref[...] = rot(x[:, :w]) * ATT_SCALE
    k_ref[...] = rot(x[:, w:2 * w])
    v_ref[...] = x[:, 2 * w:]


def _rope_tables(pos):
    half = ROPE_DIM // 2
    inv_freq = jnp.exp(-math.log(ROPE_THETA) * jnp.arange(half, dtype=jnp.float32) * (2.0 / ROPE_DIM))
    ang = pos.astype(jnp.float32)[:, None] * inv_freq[None, :]
    cos, sin = jnp.cos(ang), jnp.sin(ang)
    n = pos.shape[0]
    one = jnp.ones((n, HEAD_DIM - ROPE_DIM), F32)
    zero = jnp.zeros((n, HEAD_DIM - ROPE_DIM), F32)
    z8 = jnp.zeros((n, half), F32)
    cos_h = jnp.concatenate([cos, cos, one], axis=1)
    lo_h = jnp.concatenate([-sin, z8, zero], axis=1)
    hi_h = jnp.concatenate([z8, sin, zero], axis=1)
    two = lambda t: jnp.concatenate([t, t], axis=1)
    return two(cos_h), two(lo_h), two(hi_h)


def _rope(qkv, tables):
    t = qkv.shape[0]
    period = tables[0].shape[0]
    tm = _row_tile(period, 512)
    nb = period // tm
    w = ATT_WIDTH
    tspec = pl.BlockSpec((tm, 128), lambda i: (i % nb, 0))
    ospec = pl.BlockSpec((tm, w), lambda i: (i, 0))
    return pl.pallas_call(
        _rope_kernel,
        grid=(t // tm,),
        in_specs=[pl.BlockSpec((tm, 3 * w), lambda i: (i, 0)), tspec, tspec, tspec],
        out_specs=[ospec] * 3,
        out_shape=[jax.ShapeDtypeStruct((t, w), F32)] * 3,
        compiler_params=_cparams("parallel"),
        name="rope",
    )(qkv, *tables)


def _multiplicity(dist):
    m = np.zeros(dist.shape, np.float64)
    for window, dil in DILATED_PATTERNS:
        m += ((dist >= 0) & (dist <= window) & (dist % dil == 0))
    return m


def _dist_bias(dist):
    m = _multiplicity(dist)
    return np.where(m > 0, np.log(np.maximum(m, 1.0)), NEG_INF).astype(np.float32)


def _attn_kernel(q_ref, kc_ref, kp_ref, vc_ref, vp_ref, o_ref, kwin, vwin, acc_scr, m_scr, l_scr):
    tile, sub = ATT_TILE, ATT_SUB
    first = pl.program_id(2) == 0
    kwin[0:tile] = kp_ref[0]
    kwin[tile:] = kc_ref[0]
    vwin[0:tile] = vp_ref[0]
    vwin[tile:] = vc_ref[0]
    row = lax.broadcasted_iota(jnp.int32, (sub, 2 * sub), 0)
    col = lax.broadcasted_iota(jnp.int32, (sub, 2 * sub), 1)
    bias = jnp.where(col < sub, jnp.where(col >= row, 0.0, NEG_INF), jnp.where(col - sub <= row, 0.0, NEG_INF))
    bias_first = jnp.where(col < sub, NEG_INF, bias)
    head0 = lax.broadcasted_iota(jnp.int32, (sub, 2 * HEAD_DIM), 1) < HEAD_DIM
    nt = (((1,), (1,)), ((), ()))
    for p, (window, dil) in enumerate(DILATED_PATTERNS):
        assert window == dil * sub
        nsub = tile // dil // sub
        for rho in range(dil):
            for a in range(nsub):
                q_rows = pl.ds(rho + dil * sub * a, sub, stride=dil) if dil > 1 else pl.ds(sub * a, sub)
                k_start = tile + rho + dil * sub * (a - 1)
                k_rows = pl.ds(k_start, 2 * sub, stride=dil) if dil > 1 else pl.ds(k_start, 2 * sub)
                q = q_ref[0, q_rows, :]
                k = kwin[k_rows, :].astype(BF16)
                v = vwin[k_rows, :].astype(BF16)
                b = jnp.where(first, bias_first, bias) if a == 0 else bias
                stats = []
                for hh in range(2):
                    qh = jnp.where(head0 if hh == 0 else jnp.logical_not(head0), q, 0.0).astype(BF16)
                    s = lax.dot_general(qh, k, nt, preferred_element_type=F32) + b
                    m = jnp.max(s, axis=1, keepdims=True)
                    e = jnp.exp(s - m)
                    l = jnp.sum(e, axis=1, keepdims=True)
                    acc = jnp.dot(e.astype(BF16), v, preferred_element_type=F32)
                    stats.append((m, l, acc))
                pick = lambda i: jnp.where(head0, stats[0][i], stats[1][i])
                m_scr[p, q_rows, :] = pick(0)
                l_scr[p, q_rows, :] = pick(1)
                acc_scr[p, q_rows, :] = pick(2)
    ms = [m_scr[p] for p in range(len(DILATED_PATTERNS))]
    m = functools.reduce(jnp.maximum, ms)
    ws = [jnp.exp(mp - m) for mp in ms]
    den = sum(w * l_scr[p] for p, w in enumerate(ws))
    num = sum(w * acc_scr[p] for p, w in enumerate(ws))
    o_ref[0] = num / den


def _attn_prompt(q, k, v, bsz, seq):
    tile = ATT_TILE
    assert seq % tile == 0 and tile == WIN_MAX
    lanes = 2 * HEAD_DIM
    npair = ATT_WIDTH // lanes
    npat = len(DILATED_PATTERNS)
    r3 = lambda a: a.reshape(bsz, seq, ATT_WIDTH)
    cur = pl.BlockSpec((1, tile, lanes), lambda b, h, i: (b, i, h))
    prev = pl.BlockSpec((1, tile, lanes), lambda b, h, i: (b, jnp.maximum(i - 1, 0), h))
    o = pl.pallas_call(
        _attn_kernel,
        grid=(bsz, npair, seq // tile),
        in_specs=[cur, cur, prev, cur, prev],
        out_specs=cur,
        out_shape=jax.ShapeDtypeStruct((bsz, seq, ATT_WIDTH), F32),
        scratch_shapes=[pltpu.VMEM((2 * tile, lanes), F32), pltpu.VMEM((2 * tile, lanes), F32),
                        pltpu.VMEM((npat, tile, lanes), F32), pltpu.VMEM((npat, tile, lanes), F32),
                        pltpu.VMEM((npat, tile, lanes), F32)],
        compiler_params=_cparams("parallel", "parallel", "arbitrary"),
        name="attn_prompt",
    )(r3(q), r3(k), r3(k), r3(v), r3(v))
    return o.reshape(bsz * seq, ATT_WIDTH)


def _attn_step_kernel(q_ref, kn_ref, vn_ref, kc_ref, vc_ref, bias_ref, o_ref):
    w = ATT_WIDTH
    q = q_ref[0]
    kn, vn = kn_ref[0], vn_ref[0]
    n_buf = kc_ref.shape[-1]
    kc = kc_ref[0, 0].reshape(w, n_buf).astype(BF16)
    vc = vc_ref[0, 0].reshape(w, n_buf).astype(BF16)
    hrow = lax.broadcasted_iota(jnp.int32, (8, w), 0)
    hcol = lax.broadcasted_iota(jnp.int32, (8, w), 1) // HEAD_DIM
    own = hrow == hcol
    qh = jnp.where(own, q, 0.0).astype(BF16)
    s_c = jnp.dot(qh, kc, preferred_element_type=F32) + bias_ref[...]
    s_n = jnp.sum(qh.astype(F32) * kn.astype(BF16).astype(F32), axis=1, keepdims=True) + math.log(
        len(DILATED_PATTERNS))
    m = jnp.maximum(jnp.max(s_c, axis=1, keepdims=True), s_n)
    p_c = jnp.exp(s_c - m)
    p_n = jnp.exp(s_n - m)
    den = jnp.sum(p_c, axis=1, keepdims=True) + p_n
    o_all = lax.dot_general(p_c.astype(BF16), vc, (((1,), (1,)), ((), ())), preferred_element_type=F32)
    o_all = (o_all + p_n * vn) / den
    o_ref[0] = jnp.sum(jnp.where(own, o_all, 0.0), axis=0, keepdims=True)


def _attn_step(q, k_new, v_new, k_cache, v_cache, layer):
    n, n_buf = k_cache.shape[1], k_cache.shape[-1]
    w = ATT_WIDTH
    dist = n_buf - np.arange(n_buf)
    bias = _dist_bias(dist)[None, :]
    vec = pl.BlockSpec((1, 1, w), lambda i: (i, 0, 0))
    buf = pl.BlockSpec((1, 1, ATT_HEADS, HEAD_DIM, n_buf), lambda i: (layer, i, 0, 0, 0))
    o = pl.pallas_call(
        _attn_step_kernel,
        grid=(n,),
        in_specs=[vec, vec, vec, buf, buf, pl.BlockSpec((1, n_buf), lambda i: (0, 0))],
        out_specs=vec,
        out_shape=jax.ShapeDtypeStruct((n, 1, w), F32),
        compiler_params=_cparams("parallel"),
        name="attn_step",
    )(q.reshape(n, 1, w), k_new.reshape(n, 1, w), v_new.reshape(n, 1, w), k_cache, v_cache, jnp.asarray(bias))
    return o.reshape(n, w)


def _gelu_tanh(x):
    return 0.5 * x * (1.0 + jnp.tanh(math.sqrt(2.0 / math.pi) * (x + 0.044715 * (x * x * x))))


def _mix_kernel(x_ref, ys_ref, yr_ref, bonus_ref, g_ref, ya_ref, wglu_ref, bglu_ref, gng_ref, gnb_ref,
                havg_ref, wout_ref, lng_ref, lnb_ref, o_ref, *, alpha):
    ys = _gelu_tanh(ys_ref[...])
    ya = ys * _sigmoid(_bdot(ys, wglu_ref[...]) + bglu_ref[...])
    yr = yr_ref[...]
    havg = havg_ref[...]
    mean = _dot2_exact_rhs(yr, havg) * (1.0 / HEAD_DIM)
    yc = yr - mean
    var = _dot2_exact_rhs(yc * yc, havg) * (1.0 / HEAD_DIM)
    yb = (yc * lax.rsqrt(var + RW_GN_EPS) * gng_ref[...] + gnb_ref[...] + bonus_ref[...]) * g_ref[...]
    wout = wout_ref[...]
    o1, o2 = S5_WIDTH, S5_WIDTH + RW_WIDTH
    mix = _bdot(ya, wout[:o1]) + _bdot(yb, wout[o1:o2]) + _bdot(ya_ref[...], wout[o2:])
    o_ref[...] = _layer_norm(alpha * x_ref[...] + mix, lng_ref[...], lnb_ref[...])


def _mix(x, y_s5, y_rw, bonus, gate, y_att, lp, alpha):
    t = x.shape[0]
    tm = _row_tile(t, 512)
    tile = lambda n: pl.BlockSpec((tm, n), lambda i: (i, 0))
    full = lambda a: pl.BlockSpec(a.shape, lambda i: (0,) * a.ndim)
    params = [lp['s5_w_glu'].astype(BF16), lp['s5_b_glu'].reshape(1, -1), lp['rw_gn_g'].reshape(1, -1),
              lp['rw_gn_b'].reshape(1, -1), _head_sum_matrix(RW_WIDTH), lp['w_out'].astype(BF16),
              lp['ln_g'][0].reshape(1, -1), lp['ln_b'][0].reshape(1, -1)]
    return pl.pallas_call(
        functools.partial(_mix_kernel, alpha=alpha),
        grid=(t // tm,),
        in_specs=[tile(D_MODEL), tile(S5_WIDTH), tile(RW_WIDTH), tile(RW_WIDTH), tile(RW_WIDTH),
                  tile(ATT_WIDTH)] + [full(a) for a in params],
        out_specs=tile(D_MODEL),
        out_shape=jax.ShapeDtypeStruct((t, D_MODEL), F32),
        compiler_params=_cparams("parallel"),
        name="mix",
    )(x, y_s5, y_rw, bonus, gate, y_att, *params)


def _rank_select(vals, n_rows, keep):
    ridx = lax.broadcasted_iota(jnp.int32, vals.shape, 0)
    cnt = jnp.zeros(vals.shape, jnp.int32)
    for j in range(n_rows):
        vj = vals[j:j + 1, :]
        beats = jnp.where(vj > vals, 1, jnp.where(vj == vals, jnp.where(ridx > j, 1, 0), 0))
        cnt = cnt + beats
    return cnt < keep


def _router_kernel(x_ref, wt_ref, bias_ref, gates_ref, gscore_scr, ekeep_scr):
    e, ng = N_EXPERTS, N_EXPERT_GROUPS
    per = e // ng
    logits = lax.dot_general(wt_ref[...].astype(BF16), x_ref[...].astype(BF16), (((1,), (1,)), ((), ())),
                             preferred_element_type=F32)
    scores = _sigmoid(logits)
    sel = scores + bias_ref[:, 0:1]
    t = sel.shape[1]
    pos = lax.broadcasted_iota(jnp.int32, (per, t), 0)
    for gi in range(ng):
        grp = sel[gi * per:(gi + 1) * per]
        m1 = jnp.max(grp, axis=0, keepdims=True)
        first = jnp.min(jnp.where(grp == m1, pos, per), axis=0, keepdims=True)
        m2 = jnp.max(jnp.where(pos == first, NEG_INF, grp), axis=0, keepdims=True)
        gscore_scr[gi:gi + 1, :] = m1 + m2
    gkeep = jnp.where(_rank_select(gscore_scr[...], ng, TOPK_GROUPS), 1.0, 0.0)
    for gi in range(ng):
        ekeep_scr[gi * per:(gi + 1) * per, :] = jnp.broadcast_to(gkeep[gi:gi + 1], (per, t))
    masked = jnp.where(ekeep_scr[...] > 0.0, sel, NEG_INF)
    chosen = _rank_select(masked, e, TOP_K)
    w = jnp.where(chosen, scores, 0.0)
    gates_ref[...] = w / jnp.sum(w, axis=0, keepdims=True) * ROUTED_SCALE


def _router(x, w_router, router_bias):
    t = x.shape[0]
    tm = _row_tile(t, 512)
    e = N_EXPERTS
    gates_t = pl.pallas_call(
        _router_kernel,
        grid=(t // tm,),
        in_specs=[pl.BlockSpec((tm, D_MODEL), lambda i: (i, 0)),
                  pl.BlockSpec((e, D_MODEL), lambda i: (0, 0)),
                  pl.BlockSpec((e, 128), lambda i: (0, 0))],
        out_specs=pl.BlockSpec((e, tm), lambda i: (0, i)),
        out_shape=jax.ShapeDtypeStruct((e, t), F32),
        scratch_shapes=[pltpu.VMEM((N_EXPERT_GROUPS, tm), F32), pltpu.VMEM((e, tm), F32)],
        compiler_params=_cparams("parallel"),
        name="router",
    )(x, w_router.T, jnp.broadcast_to(router_bias.reshape(e, 1), (e, 128)))
    return gates_t.T


def _moe_kernel(x_ref, gates_ref, wg_ref, wu_ref, wd_ref, sg_ref, su_ref, sd_ref, lng_ref, lnb_ref,
                o_ref, xb_scr, gsplit_scr, acc_scr, *, alpha):
    e = pl.program_id(1)
    ne = pl.num_programs(1)

    @pl.when(e == 0)
    def _():
        xb = x_ref[...].astype(BF16)
        xb_scr[...] = xb
        hi, lo = _split(gates_ref[...])
        gsplit_scr[...] = jnp.concatenate([hi, lo], axis=1)
        h = _silu(jnp.dot(xb, sg_ref[0].astype(BF16), preferred_element_type=F32)) * jnp.dot(
            xb, su_ref[0].astype(BF16), preferred_element_type=F32)
        acc_scr[...] = jnp.dot(h.astype(BF16), sd_ref[0].astype(BF16), preferred_element_type=F32)

    xb = xb_scr[...]
    h = _silu(jnp.dot(xb, wg_ref[0, 0].astype(BF16), preferred_element_type=F32)) * jnp.dot(
        xb, wu_ref[0, 0].astype(BF16), preferred_element_type=F32)
    pick = lax.broadcasted_iota(jnp.int32, (2 * N_EXPERTS, EXPERT_FF), 0) % N_EXPERTS == e
    gate = jnp.dot(gsplit_scr[...], jnp.where(pick, 1.0, 0.0).astype(BF16), preferred_element_type=F32)
    gate = jnp.concatenate([gate] * (D_MODEL // EXPERT_FF), axis=1)
    acc_scr[...] += jnp.dot(h.astype(BF16), wd_ref[0, 0].astype(BF16), preferred_element_type=F32) * gate

    @pl.when(e == ne - 1)
    def _():
        o_ref[...] = _layer_norm(alpha * x_ref[...] + acc_scr[...], lng_ref[...], lnb_ref[...])


def _moe(x, gates, w, layer, alpha):
    t = x.shape[0]
    tm = _row_tile(t, MOE_TOKEN_TILE)
    d, f, e = D_MODEL, EXPERT_FF, N_EXPERTS
    full = lambda a: pl.BlockSpec(a.shape, lambda i, j: (0,) * a.ndim)
    lng, lnb = w['ln_g'][layer, 1].reshape(1, -1), w['ln_b'][layer, 1].reshape(1, -1)
    shared = lambda r, c: pl.BlockSpec((1, r, c), lambda i, j: (layer, 0, 0))
    return pl.pallas_call(
        functools.partial(_moe_kernel, alpha=alpha),
        grid=(t // tm, e),
        in_specs=[pl.BlockSpec((tm, d), lambda i, j: (i, 0)),
                  pl.BlockSpec((tm, e), lambda i, j: (i, 0)),
                  pl.BlockSpec((1, 1, d, f), lambda i, j: (layer, j, 0, 0)),
                  pl.BlockSpec((1, 1, d, f), lambda i, j: (layer, j, 0, 0)),
                  pl.BlockSpec((1, 1, f, d), lambda i, j: (layer, j, 0, 0)),
                  shared(d, f), shared(d, f), shared(f, d), full(lng), full(lnb)],
        out_specs=pl.BlockSpec((tm, d), lambda i, j: (i, 0)),
        out_shape=jax.ShapeDtypeStruct((t, d), F32),
        scratch_shapes=[pltpu.VMEM((tm, d), BF16), pltpu.VMEM((tm, 2 * e), BF16), pltpu.VMEM((tm, d), F32)],
        compiler_params=_cparams("parallel", "arbitrary"),
        name="moe",
    )(x, gates, w['expert_w_gate'], w['expert_w_up'], w['expert_w_down'],
      w['shared_w_gate'], w['shared_w_up'], w['shared_w_down'], lng, lnb)


def _cache_shift_kernel(c_ref, n_ref, o_ref):
    hh, dd, n = c_ref.shape[2:]
    c = c_ref[0, 0].reshape(hh * dd, n)
    lane = lax.broadcasted_iota(jnp.int32, c.shape, 1)
    shifted = pltpu.roll(c, n - 1, axis=1)
    o_ref[0, 0] = jnp.where(lane == n - 1, n_ref[0, 0], shifted).reshape(hh, dd, n)


def _cache_shift(cache, new_rows):
    depth, n, hh, dd, buf = cache.shape
    blk = pl.BlockSpec((1, 1, hh, dd, buf), lambda l, b: (l, b, 0, 0, 0))
    col = pl.BlockSpec((1, 1, hh * dd, 1), lambda l, b: (l, b, 0, 0))
    return pl.pallas_call(
        _cache_shift_kernel,
        grid=(depth, n),
        in_specs=[blk, col],
        out_specs=blk,
        out_shape=jax.ShapeDtypeStruct(cache.shape, cache.dtype),
        compiler_params=_cparams("parallel", "parallel"),
        name="cache_shift",
    )(cache, new_rows.reshape(depth, n, hh * dd, 1))


def _trunk_layer(x, bsz, seq, pos0, h0_re, h0_im, rw_s0, rw_prev, k_buf, v_buf, lp, s5p, rope_tables, alpha):
    t = bsz * seq
    u, u_bf16, p_rw, qkv = _proj(x, lp['w_in_bf16'])
    if seq == 1:
        ab_re, ab_im, bb_re, bb_im = s5p[7:]
        y_s5, h_re, h_im = _s5_step(u, h0_re, h0_im, ab_re, ab_im, bb_re, bb_im,
                                    lp['s5_c_re'], lp['s5_c_im'], lp['s5_d'])
    else:
        y_s5, h_re, h_im = _s5_scan(u, u_bf16, lp['s5_emb'], h0_re, h0_im, bsz, seq)
    (r, lw, k2, v, ah, b, gate, bonus), cols = _rw_prep(p_rw, rw_prev, lp, bsz, seq)
    y_rw, rw_s = _rw_scan(r, ah, lw, v, b, k2, cols, rw_s0, bsz, seq)
    rw_row = p_rw.reshape(bsz, seq, 4 * RW_WIDTH)[:, -1]
    q, k, vv = _rope(qkv, rope_tables)
    if seq == 1:
        y_att = _attn_step(q, k, vv, k_buf, v_buf, lp['layer'])
    else:
        assert k_buf is None
        y_att = _attn_prompt(q, k, vv, bsz, seq)
    keep = min(WIN_MAX, seq)
    last = lambda a: a.reshape(bsz, seq, ATT_WIDTH)[:, seq - keep:].reshape(bsz, keep, ATT_HEADS, HEAD_DIM)
    k_new, v_new = last(k), last(vv)
    x1 = _mix(x, y_s5, y_rw, bonus, gate, y_att, lp, alpha)
    gates = _router(x1, lp['w_router'], lp['router_bias'])
    x2 = _moe(x1, gates, lp['all'], lp['layer'], alpha)
    return x2, (h_re, h_im, rw_s, rw_row, k_new, v_new)


def kernel(x_prompt, x_sample, state_s5_re, state_s5_im, state_rwkv, state_rwkv_shift, cache_attn_k, cache_attn_v, w_in, s5_lambda_re, s5_lambda_im, s5_b_re, s5_b_im, s5_c_re, s5_c_im, s5_d, s5_log_step, s5_w_glu, s5_b_glu, rw_mu, rw_w0, rw_w1, rw_w2, rw_a0, rw_a1, rw_a2, rw_g1, rw_g2, rw_k_k, rw_k_a, rw_r_k, rw_gn_g, rw_gn_b, w_out, ln_g, ln_b, w_router, router_bias, expert_w_gate, expert_w_up, expert_w_down, shared_w_gate, shared_w_up, shared_w_down):
    depth = w_in.shape[0]
    alpha = (2 * depth) ** 0.25
    bsz, seq, d = x_prompt.shape
    dbsz, dseq, _ = x_sample.shape
    past = cache_attn_k.shape[2]
    names = dict(s5_c_re=s5_c_re, s5_c_im=s5_c_im, s5_d=s5_d, s5_w_glu=s5_w_glu, s5_b_glu=s5_b_glu,
                 rw_mu=rw_mu, rw_w0=rw_w0, rw_w1=rw_w1, rw_w2=rw_w2, rw_a0=rw_a0, rw_a1=rw_a1, rw_a2=rw_a2,
                 rw_g1=rw_g1, rw_g2=rw_g2, rw_k_k=rw_k_k, rw_k_a=rw_k_a, rw_r_k=rw_r_k, rw_gn_g=rw_gn_g,
                 rw_gn_b=rw_gn_b, w_out=w_out, ln_g=ln_g, ln_b=ln_b, w_router=w_router,
                 router_bias=router_bias, expert_w_gate=expert_w_gate, expert_w_up=expert_w_up,
                 expert_w_down=expert_w_down, shared_w_gate=shared_w_gate, shared_w_up=shared_w_up,
                 shared_w_down=shared_w_down)
    assert dseq == 1
    tables_p = _rope_tables(jnp.arange(seq))
    tables_s = _rope_tables(jnp.full((dbsz,), PAST_LEN))
    yp = x_prompt.reshape(bsz * seq, d)
    ys = x_sample.reshape(dbsz * dseq, d)
    k_cache = jnp.transpose(cache_attn_k, (0, 1, 3, 4, 2))
    v_cache = jnp.transpose(cache_attn_v, (0, 1, 3, 4, 2))
    new_p, new_s = [], []
    for l in range(depth):
        lp = {k: v[l] for k, v in names.items() if not k.startswith(('expert_w', 'shared_w'))}
        lp['all'], lp['layer'] = names, l
        lp['w_in_bf16'] = w_in[l].astype(BF16)
        s5p = _s5_prep(s5_lambda_re[l], s5_lambda_im[l], s5_log_step[l], s5_b_re[l], s5_b_im[l],
                       s5_c_re[l], s5_c_im[l])
        lp['s5_emb'] = _s5_embed(s5p[:7], s5_d[l])
        zs = jnp.zeros((bsz, S5_GROUPS, S5_STATE), F32)
        z_s = jnp.zeros((bsz, RW_HEADS, HEAD_DIM, HEAD_DIM), F32)
        zrow = jnp.zeros((bsz, 4 * RW_WIDTH), F32)
        yp, st_p = _trunk_layer(yp, bsz, seq, 0, zs, zs, z_s, zrow, None, None, lp, s5p, tables_p, alpha)
        ys, st_s = _trunk_layer(ys, dbsz, dseq, past, state_s5_re[l], state_s5_im[l], state_rwkv[l],
                                state_rwkv_shift[l], k_cache, v_cache, lp, s5p, tables_s, alpha)
        new_p.append(st_p)
        new_s.append(st_s)
    stack = lambda sts: [jnp.stack([s[i] for s in sts], 0) for i in range(6)]
    out_p, out_s = stack(new_p), stack(new_s)
    assert past == WIN_MAX
    to_rows = lambda c: jnp.transpose(c, (0, 1, 4, 2, 3))
    out_s[4] = to_rows(_cache_shift(k_cache, out_s[4].reshape(depth, dbsz, ATT_WIDTH)))
    out_s[5] = to_rows(_cache_shift(v_cache, out_s[5].reshape(depth, dbsz, ATT_WIDTH)))
    return (yp.reshape(bsz, seq, d), ys.reshape(dbsz, dseq, d), *out_p, *out_s)
```

```python
import functools
import math

import numpy as np
import jax
import jax.numpy as jnp
from jax import lax
from jax.experimental import pallas as pl
from jax.experimental.pallas import tpu as pltpu

F32 = jnp.float32
BF16 = jnp.bfloat16

D_MODEL = 1024
PAST_LEN = 8192
HEAD_DIM = 64
S5_WIDTH = 256
S5_GROUP = 16
S5_GROUPS = 16
S5_STATE = 64
RW_WIDTH = 384
RW_HEADS = 6
ATT_WIDTH = 384
ATT_HEADS = 6
N_IN = S5_WIDTH + 4 * RW_WIDTH + 3 * ATT_WIDTH
RW_GN_EPS = 64e-5
DILATED_PATTERNS = ((128, 1), (512, 4), (2048, 16))
WIN_MAX = 2048
ATT_SCALE = HEAD_DIM ** -0.5
ROPE_THETA = 500000.0
ROPE_DIM = HEAD_DIM // 4
NEG_INF = -1e30
N_EXPERTS = 64
N_EXPERT_GROUPS = 8
TOPK_GROUPS = 4
TOP_K = 8
EXPERT_FF = 256
ROUTED_SCALE = 2.5
LN_EPS = 1e-5

S5_CHUNK = 16
S5_STATE_BLOCK = 256
S5_OUT_BLOCK = 512
RW_CHUNK = 64
RW_SEQS_PER_STEP = 2
MOE_TOKEN_TILE = 1024
ATT_TILE = 2048
ATT_SUB = 128
VMEM_LIMIT = 56 * 1024 * 1024


def _cparams(*sem):
    return pltpu.CompilerParams(dimension_semantics=sem, vmem_limit_bytes=VMEM_LIMIT)


def _bdot(a, b):
    return jnp.dot(a.astype(BF16), b.astype(BF16), preferred_element_type=F32)


def _split(x):
    hi = x.astype(BF16)
    lo = (x - hi.astype(F32)).astype(BF16)
    return hi, lo


def _dot3(a, b):
    ah, al = _split(a)
    bh, bl = _split(b)
    d = functools.partial(jnp.dot, preferred_element_type=F32)
    return d(ah, bh) + (d(ah, bl) + d(al, bh))


def _dot2_exact_rhs(a, b_bf16):
    ah, al = _split(a)
    d = functools.partial(jnp.dot, preferred_element_type=F32)
    return d(ah, b_bf16) + d(al, b_bf16)


def _sigmoid(x):
    return 1.0 / (1.0 + jnp.exp(-x))


def _silu(x):
    return x * _sigmoid(x)


def _layer_norm(x, g, b):
    mu = jnp.mean(x, axis=-1, keepdims=True)
    xc = x - mu
    var = jnp.mean(xc * xc, axis=-1, keepdims=True)
    return xc * lax.rsqrt(var + LN_EPS) * g + b


def _row_tile(t, pref):
    tm = min(t, pref)
    assert t % tm == 0
    return tm


def _proj_kernel(x_ref, w_ref, u_ref, ub_ref, rw_ref, qkv_ref):
    p = jnp.dot(x_ref[...].astype(BF16), w_ref[...], preferred_element_type=F32)
    o1 = S5_WIDTH
    o2 = o1 + 4 * RW_WIDTH
    u_ref[...] = p[:, :o1]
    ub_ref[...] = p[:, :o1].astype(BF16)
    rw_ref[...] = p[:, o1:o2]
    qkv_ref[...] = p[:, o2:]


def _proj(x, w_bf16):
    t = x.shape[0]
    tm = _row_tile(t, 512)
    return pl.pallas_call(
        _proj_kernel,
        grid=(t // tm,),
        in_specs=[pl.BlockSpec((tm, D_MODEL), lambda i: (i, 0)),
                  pl.BlockSpec((D_MODEL, N_IN), lambda i: (0, 0))],
        out_specs=[pl.BlockSpec((tm, S5_WIDTH), lambda i: (i, 0)),
                   pl.BlockSpec((tm, S5_WIDTH), lambda i: (i, 0)),
                   pl.BlockSpec((tm, 4 * RW_WIDTH), lambda i: (i, 0)),
                   pl.BlockSpec((tm, 3 * ATT_WIDTH), lambda i: (i, 0))],
        out_shape=[jax.ShapeDtypeStruct((t, S5_WIDTH), F32),
                   jax.ShapeDtypeStruct((t, S5_WIDTH), BF16),
                   jax.ShapeDtypeStruct((t, 4 * RW_WIDTH), F32),
                   jax.ShapeDtypeStruct((t, 3 * ATT_WIDTH), F32)],
        compiler_params=_cparams("parallel"),
        name="proj",
    )(x, w_bf16)


def _s5_discretize(lr, li, log_step):
    dt = jnp.exp(log_step)
    mag = jnp.exp(lr * dt)
    ab_re = mag * jnp.cos(li * dt)
    ab_im = mag * jnp.sin(li * dt)
    den = lr * lr + li * li
    cf_re = ((ab_re - 1.0) * lr + ab_im * li) / den
    cf_im = (ab_im * lr - (ab_re - 1.0) * li) / den
    return dt, ab_re, ab_im, cf_re, cf_im


def _s5_power(lr, li, dt, n):
    mag = jnp.exp(n * (lr * dt))
    ang = n * (li * dt)
    return mag * jnp.cos(ang), mag * jnp.sin(ang)


def _s5_prep_kernel(lr_ref, li_ref, ls_ref, lrc_ref, lic_ref, lsc_ref, bt_re_ref, bt_im_ref,
                    ct_re_ref, ct_im_ref,
                    kmat_ref, bout_re_ref, bout_im_ref, win_re_ref, win_im_ref,
                    aq_re_ref, aq_im_ref, ab_re_ref, ab_im_ref, bb_re_ref, bb_im_ref):
    q = S5_CHUNK
    n_rows = q * S5_GROUP
    lr, li, ls = lr_ref[0], li_ref[0], ls_ref[0]
    dt, ab_re, ab_im, cf_re, cf_im = _s5_discretize(lr, li, ls)
    ab_re_ref[0] = ab_re
    ab_im_ref[0] = ab_im
    aq_re, aq_im = _s5_power(lr, li, dt, jnp.float32(q))
    aq_re_ref[0] = aq_re
    aq_im_ref[0] = aq_im
    bt_re, bt_im = bt_re_ref[0], bt_im_ref[0]
    bb_re = cf_re * bt_re - cf_im * bt_im
    bb_im = cf_re * bt_im + cf_im * bt_re
    bb_re_ref[0] = bb_re[:S5_GROUP]
    bb_im_ref[0] = bb_im[:S5_GROUP]
    step = lax.broadcasted_iota(jnp.int32, (n_rows, S5_STATE), 0) // S5_GROUP
    pw_re, pw_im = _s5_power(lr, li, dt, (q - 1 - step).astype(F32))
    bout_re_ref[0] = (bb_re * pw_re - bb_im * pw_im).astype(BF16)
    bout_im_ref[0] = (bb_re * pw_im + bb_im * pw_re).astype(BF16)
    lrc, lic, lsc = lrc_ref[0], lic_ref[0], lsc_ref[0]
    dtc, abc_re, abc_im, _, _ = _s5_discretize(lrc, lic, lsc)
    lag = (lax.broadcasted_iota(jnp.int32, (S5_STATE, n_rows), 1) // S5_GROUP).astype(F32)
    p0_re, p0_im = _s5_power(lrc, lic, dtc, lag)
    ct_re, ct_im = ct_re_ref[0], ct_im_ref[0]
    e0_re = ct_re * p0_re - ct_im * p0_im
    e0_im = ct_re * p0_im + ct_im * p0_re
    e1_re = e0_re * abc_re - e0_im * abc_im
    e1_im = e0_re * abc_im + e0_im * abc_re
    win_re_ref[0] = e1_re.astype(BF16)
    win_im_ref[0] = (-e1_im).astype(BF16)
    g = _dot3(bb_re[:S5_GROUP], e0_re) - _dot3(bb_im[:S5_GROUP], e0_im)
    lane = lax.broadcasted_iota(jnp.int32, (S5_GROUP, n_rows), 1)
    blocks = []
    for i in range(q):
        if i == 0:
            blocks.append(g)
        else:
            shifted = pltpu.roll(g, S5_GROUP * i, axis=1)
            blocks.append(jnp.where(lane >= S5_GROUP * i, shifted, 0.0))
    kmat_ref[0] = jnp.concatenate(blocks, axis=0).astype(BF16)


def _s5_prep(lam_re, lam_im, log_step, b_re, b_im, c_re, c_im):
    g, p, c, q = S5_GROUPS, S5_STATE, S5_GROUP, S5_CHUNK
    n = q * c
    row = lambda a: a.reshape(g, 1, p)
    col = lambda a: jnp.broadcast_to(a.reshape(g, p, 1), (g, p, n))
    ls_row = jnp.broadcast_to(log_step.reshape(g, 1, 1), (g, 1, p))
    ls_col = jnp.broadcast_to(log_step.reshape(g, 1, 1), (g, p, n))
    bt = lambda a: jnp.tile(jnp.swapaxes(a, 1, 2), (1, q, 1))
    ct = lambda a: jnp.tile(jnp.swapaxes(a, 1, 2), (1, 1, q))
    spec = lambda *s: pl.BlockSpec((1,) + s, lambda i: (i, 0, 0))
    sds = lambda s, d: jax.ShapeDtypeStruct((g,) + s, d)
    return pl.pallas_call(
        _s5_prep_kernel,
        grid=(g,),
        in_specs=[spec(1, p), spec(1, p), spec(1, p), spec(p, n), spec(p, n), spec(p, n),
                  spec(n, p), spec(n, p), spec(p, n), spec(p, n)],
        out_specs=[spec(n, n), spec(n, p), spec(n, p), spec(p, n), spec(p, n),
                   spec(1, p), spec(1, p), spec(1, p), spec(1, p), spec(c, p), spec(c, p)],
        out_shape=[sds((n, n), BF16), sds((n, p), BF16), sds((n, p), BF16), sds((p, n), BF16),
                   sds((p, n), BF16), sds((1, p), F32), sds((1, p), F32), sds((1, p), F32),
                   sds((1, p), F32), sds((c, p), F32), sds((c, p), F32)],
        compiler_params=_cparams("parallel"),
        name="s5_prep",
    )(row(lam_re), row(lam_im), ls_row, col(lam_re), col(lam_im), ls_col,
      bt(b_re), bt(b_im), ct(c_re), ct(c_im))


def _s5_embed(mats, d_skip):
    g, p, c, q = S5_GROUPS, S5_STATE, S5_GROUP, S5_CHUNK
    kmat, bout_re, bout_im, win_re, win_im, aq_re, aq_im = mats
    n = q * g * c
    by_step = lambda m: jnp.swapaxes(m.reshape(g, q, c, m.shape[-1]), 0, 1).reshape(n, m.shape[-1])
    src = jnp.arange(q * c)
    dst = jnp.arange(n)
    spread = ((src[:, None] // c == dst[None, :] // (g * c)) & (src[:, None] % c == dst[None, :] % c)).astype(BF16)
    over_groups = lambda m: jnp.dot(m, spread, preferred_element_type=BF16)
    row_group = lambda rows, per: (jnp.arange(rows) // per) % g
    col_group_out = (dst // c) % g
    col_group_state = jnp.arange(g * p) // p
    rows_in = row_group(n, c)
    rows_state = jnp.arange(g * p) // p
    keep = lambda m, rg, cg: jnp.where(rg[:, None] == cg[None, :], m, jnp.zeros((), m.dtype))
    kf = keep(over_groups(by_step(kmat)), rows_in, col_group_out)
    bo = lambda m: keep(jnp.tile(by_step(m), (1, g)), rows_in, col_group_state)
    wi = lambda m: keep(over_groups(m.reshape(g * p, q * c)), rows_state, col_group_out)
    d_full = jnp.tile(d_skip.reshape(1, g * c), (1, q))
    return (kf, bo(bout_re), bo(bout_im), wi(win_re), wi(win_im),
            aq_re.reshape(1, g * p), aq_im.reshape(1, g * p), d_full)


def _s5_state_kernel(xb_ref, bout_re_ref, bout_im_ref, aq_re_ref, aq_im_ref, h0_re_ref, h0_im_ref,
                     hin_re_ref, hin_im_ref, hf_re_ref, hf_im_ref, s_re, s_im, e_re, e_im, *, bsz, n_chunks):
    xb = xb_ref[...]
    s_re[...] = jnp.dot(xb, bout_re_ref[...], preferred_element_type=F32)
    s_im[...] = jnp.dot(xb, bout_im_ref[...], preferred_element_type=F32)
    ar, ai = aq_re_ref[...], aq_im_ref[...]

    def body(k, carry):
        new = []
        for b in range(bsz):
            hr, hi = carry[2 * b], carry[2 * b + 1]
            r = b * n_chunks + k
            e_re[pl.ds(r, 1), :] = hr
            e_im[pl.ds(r, 1), :] = hi
            new.append(ar * hr - ai * hi + s_re[pl.ds(r, 1), :])
            new.append(ar * hi + ai * hr + s_im[pl.ds(r, 1), :])
        return tuple(new)

    init = []
    for b in range(bsz):
        init.append(h0_re_ref[b:b + 1, :])
        init.append(h0_im_ref[b:b + 1, :])
    fin = lax.fori_loop(0, n_chunks, body, tuple(init))
    for b in range(bsz):
        hf_re_ref[b:b + 1, :] = fin[2 * b]
        hf_im_ref[b:b + 1, :] = fin[2 * b + 1]
    hin_re_ref[...] = e_re[...].astype(BF16)
    hin_im_ref[...] = e_im[...].astype(BF16)


def _s5_out_kernel(xb_ref, x_ref, kf_ref, hin_re_ref, hin_im_ref, win_re_ref, win_im_ref, d_ref, y_ref):
    y = jnp.dot(xb_ref[...], kf_ref[...], preferred_element_type=F32)
    y = y + jnp.dot(hin_re_ref[...], win_re_ref[...], preferred_element_type=F32)
    y = y + jnp.dot(hin_im_ref[...], win_im_ref[...], preferred_element_type=F32)
    y_ref[...] = y + x_ref[...] * d_ref[...]


def _s5_scan(u, u_bf16, emb, h0_re, h0_im, bsz, seq):
    g, p, c, q = S5_GROUPS, S5_STATE, S5_GROUP, S5_CHUNK
    kf, bout_re, bout_im, win_re, win_im, aq_re, aq_im, d_full = emb
    n, gp = q * g * c, g * p
    n_chunks = seq // q
    rows = bsz * n_chunks
    x = u.reshape(rows, n)
    xb = u_bf16.reshape(rows, n)
    sb = S5_STATE_BLOCK
    col = lambda r, w: pl.BlockSpec((r, w), lambda i: (0, i))
    fix = lambda r, w: pl.BlockSpec((r, w), lambda i: (0, 0))
    hin_re, hin_im, hf_re, hf_im = pl.pallas_call(
        functools.partial(_s5_state_kernel, bsz=bsz, n_chunks=n_chunks),
        grid=(gp // sb,),
        in_specs=[fix(rows, n), col(n, sb), col(n, sb), col(1, sb), col(1, sb), col(bsz, sb), col(bsz, sb)],
        out_specs=[col(rows, sb), col(rows, sb), col(bsz, sb), col(bsz, sb)],
        out_shape=[jax.ShapeDtypeStruct((rows, gp), BF16), jax.ShapeDtypeStruct((rows, gp), BF16),
                   jax.ShapeDtypeStruct((bsz, gp), F32), jax.ShapeDtypeStruct((bsz, gp), F32)],
        scratch_shapes=[pltpu.VMEM((rows, sb), F32)] * 4,
        compiler_params=_cparams("parallel"),
        name="s5_state",
    )(xb, bout_re, bout_im, aq_re, aq_im, h0_re.reshape(bsz, gp), h0_im.reshape(bsz, gp))
    ob = S5_OUT_BLOCK
    y = pl.pallas_call(
        _s5_out_kernel,
        grid=(n // ob,),
        in_specs=[fix(rows, n), col(rows, ob), col(n, ob), fix(rows, gp), fix(rows, gp), col(gp, ob),
                  col(gp, ob), col(1, ob)],
        out_specs=col(rows, ob),
        out_shape=jax.ShapeDtypeStruct((rows, n), F32),
        compiler_params=_cparams("parallel"),
        name="s5_out",
    )(xb, x, kf, hin_re, hin_im, win_re, win_im, d_full)
    return (y.reshape(bsz * seq, g * c), hf_re.reshape(bsz, g, p), hf_im.reshape(bsz, g, p))


def _s5_step_kernel(u_ref, h0_re_ref, h0_im_ref, a_re_ref, a_im_ref, bb_re_ref, bb_im_ref,
                    cc_re_ref, cc_im_ref, d_ref, y_ref, h_re_ref, h_im_ref):
    u = u_ref[...]
    ub = u.astype(BF16)
    a_re, a_im = a_re_ref[...], a_im_ref[...]
    h0r, h0i = h0_re_ref[...], h0_im_ref[...]
    hr = a_re * h0r - a_im * h0i + jnp.dot(ub, bb_re_ref[...], preferred_element_type=F32)
    hi = a_re * h0i + a_im * h0r + jnp.dot(ub, bb_im_ref[...], preferred_element_type=F32)
    h_re_ref[...] = hr
    h_im_ref[...] = hi
    y = (jnp.dot(hr.astype(BF16), cc_re_ref[...], preferred_element_type=F32)
         - jnp.dot(hi.astype(BF16), cc_im_ref[...], preferred_element_type=F32))
    y_ref[...] = y + u * d_ref[...]


def _block_diag(blocks):
    g, r, c = blocks.shape
    eye = jnp.eye(g, dtype=blocks.dtype)
    return (blocks[:, :, None, :] * eye[:, None, :, None]).reshape(g * r, g * c)


def _s5_step(u, h0_re, h0_im, ab_re, ab_im, bb_re, bb_im, c_re, c_im, d_skip):
    n = u.shape[0]
    gp = S5_GROUPS * S5_STATE
    bbd_re = _block_diag(bb_re).astype(BF16)
    bbd_im = _block_diag(bb_im).astype(BF16)
    ccd_re = _block_diag(jnp.swapaxes(c_re, 1, 2)).astype(BF16)
    ccd_im = _block_diag(jnp.swapaxes(c_im, 1, 2)).astype(BF16)
    y, h_re, h_im = pl.pallas_call(
        _s5_step_kernel,
        out_shape=[jax.ShapeDtypeStruct((n, S5_WIDTH), F32),
                   jax.ShapeDtypeStruct((n, gp), F32),
                   jax.ShapeDtypeStruct((n, gp), F32)],
        compiler_params=pltpu.CompilerParams(vmem_limit_bytes=VMEM_LIMIT),
        name="s5_step",
    )(u, h0_re.reshape(n, gp), h0_im.reshape(n, gp), ab_re.reshape(1, gp), ab_im.reshape(1, gp),
      bbd_re, bbd_im, ccd_re, ccd_im, d_skip.reshape(1, S5_WIDTH))
    return y, h_re.reshape(n, S5_GROUPS, S5_STATE), h_im.reshape(n, S5_GROUPS, S5_STATE)


def _rw_prep_kernel(p_ref, tail_ref, prev_ref, mu_ref, w0_ref, w1_ref, w2_ref, a0_ref, a1_ref, a2_ref,
                    g1_ref, g2_ref, kk_ref, ka_ref, rk_ref, hsum_ref,
                    r_out, lw_out, k_out, v_out, ah_out, b_out, g_out, bonus_out, *col_outs, seq_tiles):
    w = RW_WIDTH
    p = p_ref[...]
    if seq_tiles:
        first = pl.program_id(0) % seq_tiles == 0
        before = jnp.where(first, prev_ref[0], tail_ref[7:8, :])
        rowid = lax.broadcasted_iota(jnp.int32, p.shape, 0)
        p_prev = jnp.where(rowid == 0, before, pltpu.roll(p, 1, axis=0))
    else:
        p_prev = prev_ref[...]
    dp = p_prev - p
    r_in, k_in, v_in, z = p[:, :w], p[:, w:2 * w], p[:, 2 * w:3 * w], p[:, 3 * w:]
    dr, dk, dv, dz = dp[:, :w], dp[:, w:2 * w], dp[:, 2 * w:3 * w], dp[:, 3 * w:]
    mu = mu_ref[...]
    r = r_in + dr * mu[0:1]
    xw = z + dz * mu[1:2]
    k = k_in + dk * mu[2:3]
    v = v_in + dv * mu[3:4]
    xa = z + dz * mu[4:5]
    xg = z + dz * mu[5:6]
    t = w0_ref[...] + _bdot(jnp.tanh(_bdot(xw, w1_ref[...])), w2_ref[...])
    nt = -t
    softplus = jnp.maximum(nt, 0.0) + jnp.log(1.0 + jnp.exp(-jnp.abs(nt)))
    w_log = -softplus - 0.5
    lw = -jnp.exp(w_log)
    lw_out[...] = lw
    a = _sigmoid(a0_ref[...] + _bdot(_bdot(xa, a1_ref[...]), a2_ref[...]))
    g_out[...] = _bdot(_sigmoid(_bdot(xg, g1_ref[...])), g2_ref[...])
    kk = k * kk_ref[...]
    ss = _dot2_exact_rhs(kk * kk, hsum_ref[...])
    kk = kk * lax.rsqrt(jnp.maximum(ss, 1e-24))
    k2 = k * (1.0 + (a - 1.0) * ka_ref[...])
    r_out[...] = r
    k_out[...] = k2
    v_out[...] = v
    ah_out[...] = -kk
    b = kk * a
    b_out[...] = b
    bonus_out[...] = _dot2_exact_rhs(r * k2 * rk_ref[...], hsum_ref[...]) * v
    for arr, out in zip((b, k2, lw), col_outs):
        arr_t = arr.T
        for j in range(arr.shape[0] // RW_CHUNK):
            out[0, j] = arr_t[:, j * RW_CHUNK:(j + 1) * RW_CHUNK]


def _head_sum_matrix(width):
    idx = np.arange(width) // HEAD_DIM
    return jnp.asarray((idx[:, None] == idx[None, :]).astype(np.float32), dtype=BF16)


def _rw_prep(p, rw_prev, lp, bsz, seq):
    t = p.shape[0]
    w = RW_WIDTH
    if seq == 1:
        tm, seq_tiles = t, 0
        tail_spec = pl.BlockSpec((t, 4 * w), lambda i: (0, 0))
        prev_spec = pl.BlockSpec((t, 4 * w), lambda i: (0, 0))
        prev_arg = rw_prev
    else:
        tm = _row_tile(seq, 512)
        seq_tiles = seq // tm
        tail_spec = pl.BlockSpec((8, 4 * w), lambda i: (jnp.maximum(i * (tm // 8) - 1, 0), 0))
        prev_spec = pl.BlockSpec((1, 1, 4 * w), lambda i: (i // seq_tiles, 0, 0))
        prev_arg = rw_prev.reshape(bsz, 1, 4 * w)
    row = lambda a: a.reshape(1, w)
    full = lambda a: pl.BlockSpec(a.shape, lambda i: (0,) * a.ndim)
    params = [lp['rw_mu'], row(lp['rw_w0']), lp['rw_w1'].astype(BF16), lp['rw_w2'].astype(BF16),
              row(lp['rw_a0']), lp['rw_a1'].astype(BF16), lp['rw_a2'].astype(BF16),
              lp['rw_g1'].astype(BF16), lp['rw_g2'].astype(BF16), row(lp['rw_k_k']),
              row(lp['rw_k_a']), row(lp['rw_r_k']), _head_sum_matrix(w)]
    tile = lambda n: pl.BlockSpec((tm, n), lambda i: (i, 0))
    out_specs = [tile(w)] * 8
    out_shape = [jax.ShapeDtypeStruct((t, w), F32)] * 8
    if seq_tiles:
        assert tm % RW_CHUNK == 0
        cpt = tm // RW_CHUNK
        out_specs = out_specs + [pl.BlockSpec((1, cpt, w, RW_CHUNK),
                                              lambda i: (i // seq_tiles, i % seq_tiles, 0, 0))] * 3
        out_shape = out_shape + [jax.ShapeDtypeStruct((bsz, seq // RW_CHUNK, w, RW_CHUNK), F32)] * 3
    outs = pl.pallas_call(
        functools.partial(_rw_prep_kernel, seq_tiles=seq_tiles),
        grid=(t // tm,),
        in_specs=[tile(4 * w), tail_spec, prev_spec] + [full(a) for a in params],
        out_specs=out_specs,
        out_shape=out_shape,
        compiler_params=_cparams("parallel"),
        name="rw_prep",
    )(p, p, prev_arg, *params)
    return outs[:8], (tuple(outs[8:]) if seq_tiles else None)


def _rw_scan_kernel(r_ref, ah_ref, lw_ref, v_ref, bt_ref, kt_ref, lwt_ref, h0_ref,
                    y_ref, hf_ref, h_scr):
    c = RW_CHUNK
    nseq = r_ref.shape[0]
    ci = pl.program_id(1)

    @pl.when(ci == 0)
    def _():
        h_scr[...] = h0_ref[...]

    row = lax.broadcasted_iota(jnp.int32, (c, c), 0)
    col = lax.broadcasted_iota(jnp.int32, (c, c), 1)
    incl = row >= col
    strict = row > col
    tri_lo = jnp.where(incl, 1.0, 0.0).astype(BF16)
    tri_up = jnp.where(row <= col, 1.0, 0.0).astype(BF16)
    eye = jnp.where(row == col, 1.0, 0.0)
    d = functools.partial(jnp.dot, preferred_element_type=F32)

    def split3(x):
        hi, mid = _split(x)
        return hi, mid, (x - hi.astype(F32) - mid.astype(F32)).astype(BF16)

    cums, cums_t = [], []
    for s in range(nseq):
        p3 = split3(lw_ref[s])
        cums.append(d(tri_lo, p3[0]) + (d(tri_lo, p3[1]) + d(tri_lo, p3[2])))
        q3 = split3(lwt_ref[s, 0])
        cums_t.append(d(q3[0], tri_up) + (d(q3[1], tri_up) + d(q3[2], tri_up)))

    units = [(s, h) for s in range(nseq) for h in range(RW_HEADS)]
    sl = lambda h: slice(h * HEAD_DIM, (h + 1) * HEAD_DIM)
    each = lambda f: [f(s, h) for s, h in units]
    idx = range(len(units))
    cum = each(lambda s, h: cums[s][:, sl(h)])
    cum_t = each(lambda s, h: cums_t[s][sl(h), :])
    ar = each(lambda s, h: None)
    for i, (s, h) in enumerate(units):
        a_t = ah_ref[s, :, sl(h)] * jnp.exp(cum[i] - lw_ref[s, :, sl(h)])
        r_t = r_ref[s, :, sl(h)] * jnp.exp(cum[i])
        ar[i] = jnp.concatenate([a_t, r_t], axis=0).astype(BF16)
    w_inv_t = [jnp.exp(-cum_t[i]) for i in idx]
    cum_end = [cum_t[i][:, c - 1:c] for i in idx]
    dec_t = [jnp.exp(cum_end[i] - cum_t[i]) for i in idx]
    bt = each(lambda s, h: bt_ref[s, 0, sl(h), :])
    kt = each(lambda s, h: kt_ref[s, 0, sl(h), :])
    b_t = [(bt[i] * w_inv_t[i]).astype(BF16) for i in idx]
    k_t = [(kt[i] * w_inv_t[i]).astype(BF16) for i in idx]
    b_d = [(bt[i] * dec_t[i]).astype(BF16) for i in idx]
    k_d = [(kt[i] * dec_t[i]).astype(BF16) for i in idx]
    mb = [d(ar[i], b_t[i]) for i in idx]
    mk = [d(ar[i], k_t[i]) for i in idx]
    a_ab = [jnp.where(strict, mb[i][:c], 0.0) for i in idx]
    a_ak = [jnp.where(strict, mk[i][:c], 0.0).astype(BF16) for i in idx]
    a_rb = [jnp.where(incl, mb[i][c:], 0.0).astype(BF16) for i in idx]
    a_rk = [jnp.where(incl, mk[i][c:], 0.0).astype(BF16) for i in idx]
    pw = a_ab
    inv = [eye + a_ab[i] for i in idx]
    for _ in range(int(math.log2(c)) - 1):
        pw = [_bdot(pw[i], pw[i]) for i in idx]
        inv = [inv[i] + _bdot(inv[i], pw[i]) for i in idx]
    h0 = each(lambda s, h: h_scr[s, h])
    vb = each(lambda s, h: v_ref[s, :, sl(h)].astype(BF16))
    ar_h = [d(ar[i], h0[i].astype(BF16)) for i in idx]
    rhs = [ar_h[i][:c] + d(a_ak[i], vb[i]) for i in idx]
    ub = [_bdot(inv[i], rhs[i]).astype(BF16) for i in idx]
    y = [ar_h[i][c:] + d(a_rb[i], ub[i]) + d(a_rk[i], vb[i]) for i in idx]
    h_new = [jnp.exp(cum_end[i]) * h0[i] + d(b_d[i], ub[i]) + d(k_d[i], vb[i]) for i in idx]
    for i, (s, h) in enumerate(units):
        h_scr[s, h] = h_new[i]
    for s in range(nseq):
        y_ref[s] = jnp.concatenate([y[s * RW_HEADS + h] for h in range(RW_HEADS)], axis=1)

    for i, (s, h) in enumerate(units):
        hf_ref[s, h] = h_new[i]


def _rw_scan(r, ah, lw, v, b, k2, cols, s0, bsz, seq):
    w, c = RW_WIDTH, RW_CHUNK
    pad = (-seq) % c
    sp = seq + pad

    def rows(a):
        a = a.reshape(bsz, seq, w)
        return jnp.pad(a, ((0, 0), (0, pad), (0, 0))) if pad else a

    r3, ah3, lw3, v3 = (rows(a) for a in (r, ah, lw, v))
    n_chunks = sp // c
    if cols is None:
        tr = lambda a: jnp.swapaxes(rows(a).reshape(bsz, n_chunks, c, w), 2, 3)
        cols = (tr(b), tr(k2), tr(lw))
    h0 = jnp.swapaxes(s0, 2, 3)
    ns = RW_SEQS_PER_STEP
    assert bsz % ns == 0
    rspec = pl.BlockSpec((ns, c, w), lambda i, j: (i, j, 0))
    cspec = pl.BlockSpec((ns, 1, w, c), lambda i, j: (i, j, 0, 0))
    sspec = pl.BlockSpec((ns, RW_HEADS, HEAD_DIM, HEAD_DIM), lambda i, j: (i, 0, 0, 0))
    y, hf = pl.pallas_call(
        _rw_scan_kernel,
        grid=(bsz // ns, n_chunks),
        in_specs=[rspec, rspec, rspec, rspec, cspec, cspec, cspec, sspec],
        out_specs=[rspec, sspec],
        out_shape=[jax.ShapeDtypeStruct((bsz, sp, w), F32),
                   jax.ShapeDtypeStruct((bsz, RW_HEADS, HEAD_DIM, HEAD_DIM), F32)],
        scratch_shapes=[pltpu.VMEM((ns, RW_HEADS, HEAD_DIM, HEAD_DIM), F32)],
        compiler_params=_cparams("parallel", "arbitrary"),
        name="rw_scan",
    )(r3, ah3, lw3, v3, *cols, h0)
    return y[:, :seq].reshape(bsz * seq, w), jnp.swapaxes(hf, 2, 3)


def _rope_kernel(qkv_ref, cos_ref, sin_lo_ref, sin_hi_ref, q_ref, k_ref, v_ref):
    w = ATT_WIDTH
    half = ROPE_DIM // 2
    x = qkv_ref[...]
    rep = lambda t: jnp.concatenate([t] * (w // 128), axis=1)
    cos, s_lo, s_hi = rep(cos_ref[...]), rep(sin_lo_ref[...]), rep(sin_hi_ref[...])

    def rot(t):
        up = pltpu.roll(t, w - half, axis=1)
        dn = pltpu.roll(t, half, axis=1)
        return t * cos + up * s_lo + dn * s_hi

    q_ref[...] = rot(x[:, :w]) * ATT_SCALE
    k_ref[...] = rot(x[:, w:2 * w])
    v_ref[...] = x[:, 2 * w:]


def _rope_tables(pos):
    half = ROPE_DIM // 2
    inv_freq = jnp.exp(-math.log(ROPE_THETA) * jnp.arange(half, dtype=jnp.float32) * (2.0 / ROPE_DIM))
    ang = pos.astype(jnp.float32)[:, None] * inv_freq[None, :]
    cos, sin = jnp.cos(ang), jnp.sin(ang)
    n = pos.shape[0]
    one = jnp.ones((n, HEAD_DIM - ROPE_DIM), F32)
    zero = jnp.zeros((n, HEAD_DIM - ROPE_DIM), F32)
    z8 = jnp.zeros((n, half), F32)
    cos_h = jnp.concatenate([cos, cos, one], axis=1)
    lo_h = jnp.concatenate([-sin, z8, zero], axis=1)
    hi_h = jnp.concatenate([z8, sin, zero], axis=1)
    two = lambda t: jnp.concatenate([t, t], axis=1)
    return two(cos_h), two(lo_h), two(hi_h)


def _rope(qkv, tables):
    t = qkv.shape[0]
    period = tables[0].shape[0]
    tm = _row_tile(period, 512)
    nb = period // tm
    w = ATT_WIDTH
    tspec = pl.BlockSpec((tm, 128), lambda i: (i % nb, 0))
    ospec = pl.BlockSpec((tm, w), lambda i: (i, 0))
    return pl.pallas_call(
        _rope_kernel,
        grid=(t // tm,),
        in_specs=[pl.BlockSpec((tm, 3 * w), lambda i: (i, 0)), tspec, tspec, tspec],
        out_specs=[ospec] * 3,
        out_shape=[jax.ShapeDtypeStruct((t, w), F32)] * 3,
        compiler_params=_cparams("parallel"),
        name="rope",
    )(qkv, *tables)


def _multiplicity(dist):
    m = np.zeros(dist.shape, np.float64)
    for window, dil in DILATED_PATTERNS:
        m += ((dist >= 0) & (dist <= window) & (dist % dil == 0))
    return m


def _dist_bias(dist):
    m = _multiplicity(dist)
    return np.where(m > 0, np.log(np.maximum(m, 1.0)), NEG_INF).astype(np.float32)


def _attn_kernel(q_ref, kc_ref, kp_ref, vc_ref, vp_ref, o_ref, kwin, vwin, acc_scr, m_scr, l_scr):
    tile, sub = ATT_TILE, ATT_SUB
    first = pl.program_id(2) == 0
    kwin[0:tile] = kp_ref[0]
    kwin[tile:] = kc_ref[0]
    vwin[0:tile] = vp_ref[0]
    vwin[tile:] = vc_ref[0]
    row = lax.broadcasted_iota(jnp.int32, (sub, 2 * sub), 0)
    col = lax.broadcasted_iota(jnp.int32, (sub, 2 * sub), 1)
    bias = jnp.where(col < sub, jnp.where(col >= row, 0.0, NEG_INF), jnp.where(col - sub <= row, 0.0, NEG_INF))
    bias_first = jnp.where(col < sub, NEG_INF, bias)
    head0 = lax.broadcasted_iota(jnp.int32, (sub, 2 * HEAD_DIM), 1) < HEAD_DIM
    nt = (((1,), (1,)), ((), ()))
    for p, (window, dil) in enumerate(DILATED_PATTERNS):
        assert window == dil * sub
        nsub = tile // dil // sub
        for rho in range(dil):
            for a in range(nsub):
                q_rows = pl.ds(rho + dil * sub * a, sub, stride=dil) if dil > 1 else pl.ds(sub * a, sub)
                k_start = tile + rho + dil * sub * (a - 1)
                k_rows = pl.ds(k_start, 2 * sub, stride=dil) if dil > 1 else pl.ds(k_start, 2 * sub)
                q = q_ref[0, q_rows, :]
                k = kwin[k_rows, :].astype(BF16)
                v = vwin[k_rows, :].astype(BF16)
                b = jnp.where(first, bias_first, bias) if a == 0 else bias
                stats = []
                for hh in range(2):
                    qh = jnp.where(head0 if hh == 0 else jnp.logical_not(head0), q, 0.0).astype(BF16)
                    s = lax.dot_general(qh, k, nt, preferred_element_type=F32) + b
                    m = jnp.max(s, axis=1, keepdims=True)
                    e = jnp.exp(s - m)
                    l = jnp.sum(e, axis=1, keepdims=True)
                    acc = jnp.dot(e.astype(BF16), v, preferred_element_type=F32)
                    stats.append((m, l, acc))
                pick = lambda i: jnp.where(head0, stats[0][i], stats[1][i])
                m_scr[p, q_rows, :] = pick(0)
                l_scr[p, q_rows, :] = pick(1)
                acc_scr[p, q_rows, :] = pick(2)
    ms = [m_scr[p] for p in range(len(DILATED_PATTERNS))]
    m = functools.reduce(jnp.maximum, ms)
    ws = [jnp.exp(mp - m) for mp in ms]
    den = sum(w * l_scr[p] for p, w in enumerate(ws))
    num = sum(w * acc_scr[p] for p, w in enumerate(ws))
    o_ref[0] = num / den


def _attn_prompt(q, k, v, bsz, seq):
    tile = ATT_TILE
    assert seq % tile == 0 and tile == WIN_MAX
    lanes = 2 * HEAD_DIM
    npair = ATT_WIDTH // lanes
    npat = len(DILATED_PATTERNS)
    r3 = lambda a: a.reshape(bsz, seq, ATT_WIDTH)
    cur = pl.BlockSpec((1, tile, lanes), lambda b, h, i: (b, i, h))
    prev = pl.BlockSpec((1, tile, lanes), lambda b, h, i: (b, jnp.maximum(i - 1, 0), h))
    o = pl.pallas_call(
        _attn_kernel,
        grid=(bsz, npair, seq // tile),
        in_specs=[cur, cur, prev, cur, prev],
        out_specs=cur,
        out_shape=jax.ShapeDtypeStruct((bsz, seq, ATT_WIDTH), F32),
        scratch_shapes=[pltpu.VMEM((2 * tile, lanes), F32), pltpu.VMEM((2 * tile, lanes), F32),
                        pltpu.VMEM((npat, tile, lanes), F32), pltpu.VMEM((npat, tile, lanes), F32),
                        pltpu.VMEM((npat, tile, lanes), F32)],
        compiler_params=_cparams("parallel", "parallel", "arbitrary"),
        name="attn_prompt",
    )(r3(q), r3(k), r3(k), r3(v), r3(v))
    return o.reshape(bsz * seq, ATT_WIDTH)


def _attn_step_kernel(q_ref, kn_ref, vn_ref, kc_ref, vc_ref, bias_ref, o_ref):
    w = ATT_WIDTH
    q = q_ref[0]
    kn, vn = kn_ref[0], vn_ref[0]
    n_buf = kc_ref.shape[-1]
    kc = kc_ref[0, 0].reshape(w, n_buf).astype(BF16)
    vc = vc_ref[0, 0].reshape(w, n_buf).astype(BF16)
    hrow = lax.broadcasted_iota(jnp.int32, (8, w), 0)
    hcol = lax.broadcasted_iota(jnp.int32, (8, w), 1) // HEAD_DIM
    own = hrow == hcol
    qh = jnp.where(own, q, 0.0).astype(BF16)
    s_c = jnp.dot(qh, kc, preferred_element_type=F32) + bias_ref[...]
    s_n = jnp.sum(qh.astype(F32) * kn.astype(BF16).astype(F32), axis=1, keepdims=True) + math.log(
        len(DILATED_PATTERNS))
    m = jnp.maximum(jnp.max(s_c, axis=1, keepdims=True), s_n)
    p_c = jnp.exp(s_c - m)
    p_n = jnp.exp(s_n - m)
    den = jnp.sum(p_c, axis=1, keepdims=True) + p_n
    o_all = lax.dot_general(p_c.astype(BF16), vc, (((1,), (1,)), ((), ())), preferred_element_type=F32)
    o_all = (o_all + p_n * vn) / den
    o_ref[0] = jnp.sum(jnp.where(own, o_all, 0.0), axis=0, keepdims=True)


def _attn_step(q, k_new, v_new, k_cache, v_cache, layer):
    n, n_buf = k_cache.shape[1], k_cache.shape[-1]
    w = ATT_WIDTH
    dist = n_buf - np.arange(n_buf)
    bias = _dist_bias(dist)[None, :]
    vec = pl.BlockSpec((1, 1, w), lambda i: (i, 0, 0))
    buf = pl.BlockSpec((1, 1, ATT_HEADS, HEAD_DIM, n_buf), lambda i: (layer, i, 0, 0, 0))
    o = pl.pallas_call(
        _attn_step_kernel,
        grid=(n,),
        in_specs=[vec, vec, vec, buf, buf, pl.BlockSpec((1, n_buf), lambda i: (0, 0))],
        out_specs=vec,
        out_shape=jax.ShapeDtypeStruct((n, 1, w), F32),
        compiler_params=_cparams("parallel"),
        name="attn_step",
    )(q.reshape(n, 1, w), k_new.reshape(n, 1, w), v_new.reshape(n, 1, w), k_cache, v_cache, jnp.asarray(bias))
    return o.reshape(n, w)


def _gelu_tanh(x):
    return 0.5 * x * (1.0 + jnp.tanh(math.sqrt(2.0 / math.pi) * (x + 0.044715 * (x * x * x))))


def _mix_kernel(x_ref, ys_ref, yr_ref, bonus_ref, g_ref, ya_ref, wglu_ref, bglu_ref, gng_ref, gnb_ref,
                havg_ref, wout_ref, lng_ref, lnb_ref, o_ref, *, alpha):
    ys = _gelu_tanh(ys_ref[...])
    ya = ys * _sigmoid(_bdot(ys, wglu_ref[...]) + bglu_ref[...])
    yr = yr_ref[...]
    havg = havg_ref[...]
    mean = _dot2_exact_rhs(yr, havg) * (1.0 / HEAD_DIM)
    yc = yr - mean
    var = _dot2_exact_rhs(yc * yc, havg) * (1.0 / HEAD_DIM)
    yb = (yc * lax.rsqrt(var + RW_GN_EPS) * gng_ref[...] + gnb_ref[...] + bonus_ref[...]) * g_ref[...]
    wout = wout_ref[...]
    o1, o2 = S5_WIDTH, S5_WIDTH + RW_WIDTH
    mix = _bdot(ya, wout[:o1]) + _bdot(yb, wout[o1:o2]) + _bdot(ya_ref[...], wout[o2:])
    o_ref[...] = _layer_norm(alpha * x_ref[...] + mix, lng_ref[...], lnb_ref[...])


def _mix(x, y_s5, y_rw, bonus, gate, y_att, lp, alpha):
    t = x.shape[0]
    tm = _row_tile(t, 512)
    tile = lambda n: pl.BlockSpec((tm, n), lambda i: (i, 0))
    full = lambda a: pl.BlockSpec(a.shape, lambda i: (0,) * a.ndim)
    params = [lp['s5_w_glu'].astype(BF16), lp['s5_b_glu'].reshape(1, -1), lp['rw_gn_g'].reshape(1, -1),
              lp['rw_gn_b'].reshape(1, -1), _head_sum_matrix(RW_WIDTH), lp['w_out'].astype(BF16),
              lp['ln_g'][0].reshape(1, -1), lp['ln_b'][0].reshape(1, -1)]
    return pl.pallas_call(
        functools.partial(_mix_kernel, alpha=alpha),
        grid=(t // tm,),
        in_specs=[tile(D_MODEL), tile(S5_WIDTH), tile(RW_WIDTH), tile(RW_WIDTH), tile(RW_WIDTH),
                  tile(ATT_WIDTH)] + [full(a) for a in params],
        out_specs=tile(D_MODEL),
        out_shape=jax.ShapeDtypeStruct((t, D_MODEL), F32),
        compiler_params=_cparams("parallel"),
        name="mix",
    )(x, y_s5, y_rw, bonus, gate, y_att, *params)


def _rank_select(vals, n_rows, keep):
    ridx = lax.broadcasted_iota(jnp.int32, vals.shape, 0)
    cnt = jnp.zeros(vals.shape, jnp.int32)
    for j in range(n_rows):
        vj = vals[j:j + 1, :]
        beats = jnp.where(vj > vals, 1, jnp.where(vj == vals, jnp.where(ridx > j, 1, 0), 0))
        cnt = cnt + beats
    return cnt < keep


def _router_kernel(x_ref, wt_ref, bias_ref, gates_ref, gscore_scr, ekeep_scr):
    e, ng = N_EXPERTS, N_EXPERT_GROUPS
    per = e // ng
    logits = lax.dot_general(wt_ref[...].astype(BF16), x_ref[...].astype(BF16), (((1,), (1,)), ((), ())),
                             preferred_element_type=F32)
    scores = _sigmoid(logits)
    sel = scores + bias_ref[:, 0:1]
    t = sel.shape[1]
    pos = lax.broadcasted_iota(jnp.int32, (per, t), 0)
    for gi in range(ng):
        grp = sel[gi * per:(gi + 1) * per]
        m1 = jnp.max(grp, axis=0, keepdims=True)
        first = jnp.min(jnp.where(grp == m1, pos, per), axis=0, keepdims=True)
        m2 = jnp.max(jnp.where(pos == first, NEG_INF, grp), axis=0, keepdims=True)
        gscore_scr[gi:gi + 1, :] = m1 + m2
    gkeep = jnp.where(_rank_select(gscore_scr[...], ng, TOPK_GROUPS), 1.0, 0.0)
    for gi in range(ng):
        ekeep_scr[gi * per:(gi + 1) * per, :] = jnp.broadcast_to(gkeep[gi:gi + 1], (per, t))
    masked = jnp.where(ekeep_scr[...] > 0.0, sel, NEG_INF)
    chosen = _rank_select(masked, e, TOP_K)
    w = jnp.where(chosen, scores, 0.0)
    gates_ref[...] = w / jnp.sum(w, axis=0, keepdims=True) * ROUTED_SCALE


def _router(x, w_router, router_bias):
    t = x.shape[0]
    tm = _row_tile(t, 512)
    e = N_EXPERTS
    gates_t = pl.pallas_call(
        _router_kernel,
        grid=(t // tm,),
        in_specs=[pl.BlockSpec((tm, D_MODEL), lambda i: (i, 0)),
                  pl.BlockSpec((e, D_MODEL), lambda i: (0, 0)),
                  pl.BlockSpec((e, 128), lambda i: (0, 0))],
        out_specs=pl.BlockSpec((e, tm), lambda i: (0, i)),
        out_shape=jax.ShapeDtypeStruct((e, t), F32),
        scratch_shapes=[pltpu.VMEM((N_EXPERT_GROUPS, tm), F32), pltpu.VMEM((e, tm), F32)],
        compiler_params=_cparams("parallel"),
        name="router",
    )(x, w_router.T, jnp.broadcast_to(router_bias.reshape(e, 1), (e, 128)))
    return gates_t.T


def _moe_kernel(x_ref, gates_ref, wg_ref, wu_ref, wd_ref, sg_ref, su_ref, sd_ref, lng_ref, lnb_ref,
                o_ref, xb_scr, gsplit_scr, acc_scr, *, alpha):
    e = pl.program_id(1)
    ne = pl.num_programs(1)

    @pl.when(e == 0)
    def _():
        xb = x_ref[...].astype(BF16)
        xb_scr[...] = xb
        hi, lo = _split(gates_ref[...])
        gsplit_scr[...] = jnp.concatenate([hi, lo], axis=1)
        h = _silu(jnp.dot(xb, sg_ref[0].astype(BF16), preferred_element_type=F32)) * jnp.dot(
            xb, su_ref[0].astype(BF16), preferred_element_type=F32)
        acc_scr[...] = jnp.dot(h.astype(BF16), sd_ref[0].astype(BF16), preferred_element_type=F32)

    xb = xb_scr[...]
    h = _silu(jnp.dot(xb, wg_ref[0, 0].astype(BF16), preferred_element_type=F32)) * jnp.dot(
        xb, wu_ref[0, 0].astype(BF16), preferred_element_type=F32)
    pick = lax.broadcasted_iota(jnp.int32, (2 * N_EXPERTS, EXPERT_FF), 0) % N_EXPERTS == e
    gate = jnp.dot(gsplit_scr[...], jnp.where(pick, 1.0, 0.0).astype(BF16), preferred_element_type=F32)
    gate = jnp.concatenate([gate] * (D_MODEL // EXPERT_FF), axis=1)
    acc_scr[...] += jnp.dot(h.astype(BF16), wd_ref[0, 0].astype(BF16), preferred_element_type=F32) * gate

    @pl.when(e == ne - 1)
    def _():
        o_ref[...] = _layer_norm(alpha * x_ref[...] + acc_scr[...], lng_ref[...], lnb_ref[...])


def _moe(x, gates, w, layer, alpha):
    t = x.shape[0]
    tm = _row_tile(t, MOE_TOKEN_TILE)
    d, f, e = D_MODEL, EXPERT_FF, N_EXPERTS
    full = lambda a: pl.BlockSpec(a.shape, lambda i, j: (0,) * a.ndim)
    lng, lnb = w['ln_g'][layer, 1].reshape(1, -1), w['ln_b'][layer, 1].reshape(1, -1)
    shared = lambda r, c: pl.BlockSpec((1, r, c), lambda i, j: (layer, 0, 0))
    return pl.pallas_call(
        functools.partial(_moe_kernel, alpha=alpha),
        grid=(t // tm, e),
        in_specs=[pl.BlockSpec((tm, d), lambda i, j: (i, 0)),
                  pl.BlockSpec((tm, e), lambda i, j: (i, 0)),
                  pl.BlockSpec((1, 1, d, f), lambda i, j: (layer, j, 0, 0)),
                  pl.BlockSpec((1, 1, d, f), lambda i, j: (layer, j, 0, 0)),
                  pl.BlockSpec((1, 1, f, d), lambda i, j: (layer, j, 0, 0)),
                  shared(d, f), shared(d, f), shared(f, d), full(lng), full(lnb)],
        out_specs=pl.BlockSpec((tm, d), lambda i, j: (i, 0)),
        out_shape=jax.ShapeDtypeStruct((t, d), F32),
        scratch_shapes=[pltpu.VMEM((tm, d), BF16), pltpu.VMEM((tm, 2 * e), BF16), pltpu.VMEM((tm, d), F32)],
        compiler_params=_cparams("parallel", "arbitrary"),
        name="moe",
    )(x, gates, w['expert_w_gate'], w['expert_w_up'], w['expert_w_down'],
      w['shared_w_gate'], w['shared_w_up'], w['shared_w_down'], lng, lnb)


def _cache_shift_kernel(c_ref, n_ref, o_ref):
    hh, dd, n = c_ref.shape[2:]
    c = c_ref[0, 0].reshape(hh * dd, n)
    lane = lax.broadcasted_iota(jnp.int32, c.shape, 1)
    shifted = pltpu.roll(c, n - 1, axis=1)
    o_ref[0, 0] = jnp.where(lane == n - 1, n_ref[0, 0], shifted).reshape(hh, dd, n)


def _cache_shift(cache, new_rows):
    depth, n, hh, dd, buf = cache.shape
    blk = pl.BlockSpec((1, 1, hh, dd, buf), lambda l, b: (l, b, 0, 0, 0))
    col = pl.BlockSpec((1, 1, hh * dd, 1), lambda l, b: (l, b, 0, 0))
    return pl.pallas_call(
        _cache_shift_kernel,
        grid=(depth, n),
        in_specs=[blk, col],
        out_specs=blk,
        out_shape=jax.ShapeDtypeStruct(cache.shape, cache.dtype),
        compiler_params=_cparams("parallel", "parallel"),
        name="cache_shift",
    )(cache, new_rows.reshape(depth, n, hh * dd, 1))


def _trunk_layer(x, bsz, seq, h0_re, h0_im, rw_s0, rw_prev, k_cache, v_cache, lp, s5p, rope_tables, alpha):
    u, u_bf16, p_rw, qkv = _proj(x, lp['w_in_bf16'])
    if seq == 1:
        ab_re, ab_im, bb_re, bb_im = s5p[7:]
        y_s5, h_re, h_im = _s5_step(u, h0_re, h0_im, ab_re, ab_im, bb_re, bb_im,
                                    lp['s5_c_re'], lp['s5_c_im'], lp['s5_d'])
    else:
        y_s5, h_re, h_im = _s5_scan(u, u_bf16, lp['s5_emb'], h0_re, h0_im, bsz, seq)
    (r, lw, k2, v, ah, b, gate, bonus), cols = _rw_prep(p_rw, rw_prev, lp, bsz, seq)
    y_rw, rw_s = _rw_scan(r, ah, lw, v, b, k2, cols, rw_s0, bsz, seq)
    rw_row = p_rw.reshape(bsz, seq, 4 * RW_WIDTH)[:, -1]
    q, k, vv = _rope(qkv, rope_tables)
    if seq == 1:
        y_att = _attn_step(q, k, vv, k_cache, v_cache, lp['layer'])
    else:
        assert k_cache is None
        y_att = _attn_prompt(q, k, vv, bsz, seq)
    keep = min(WIN_MAX, seq)
    last = lambda a: a.reshape(bsz, seq, ATT_WIDTH)[:, seq - keep:].reshape(bsz, keep, ATT_HEADS, HEAD_DIM)
    k_new, v_new = last(k), last(vv)
    x1 = _mix(x, y_s5, y_rw, bonus, gate, y_att, lp, alpha)
    gates = _router(x1, lp['w_router'], lp['router_bias'])
    x2 = _moe(x1, gates, lp['all'], lp['layer'], alpha)
    return x2, (h_re, h_im, rw_s, rw_row, k_new, v_new)


def kernel(x_prompt, x_sample, state_s5_re, state_s5_im, state_rwkv, state_rwkv_shift, cache_attn_k, cache_attn_v, w_in, s5_lambda_re, s5_lambda_im, s5_b_re, s5_b_im, s5_c_re, s5_c_im, s5_d, s5_log_step, s5_w_glu, s5_b_glu, rw_mu, rw_w0, rw_w1, rw_w2, rw_a0, rw_a1, rw_a2, rw_g1, rw_g2, rw_k_k, rw_k_a, rw_r_k, rw_gn_g, rw_gn_b, w_out, ln_g, ln_b, w_router, router_bias, expert_w_gate, expert_w_up, expert_w_down, shared_w_gate, shared_w_up, shared_w_down):
    depth = w_in.shape[0]
    alpha = (2 * depth) ** 0.25
    bsz, seq, d = x_prompt.shape
    dbsz, dseq, _ = x_sample.shape
    past = cache_attn_k.shape[2]
    names = dict(s5_c_re=s5_c_re, s5_c_im=s5_c_im, s5_d=s5_d, s5_w_glu=s5_w_glu, s5_b_glu=s5_b_glu,
                 rw_mu=rw_mu, rw_w0=rw_w0, rw_w1=rw_w1, rw_w2=rw_w2, rw_a0=rw_a0, rw_a1=rw_a1, rw_a2=rw_a2,
                 rw_g1=rw_g1, rw_g2=rw_g2, rw_k_k=rw_k_k, rw_k_a=rw_k_a, rw_r_k=rw_r_k, rw_gn_g=rw_gn_g,
                 rw_gn_b=rw_gn_b, w_out=w_out, ln_g=ln_g, ln_b=ln_b, w_router=w_router,
                 router_bias=router_bias, expert_w_gate=expert_w_gate, expert_w_up=expert_w_up,
                 expert_w_down=expert_w_down, shared_w_gate=shared_w_gate, shared_w_up=shared_w_up,
                 shared_w_down=shared_w_down)
    assert dseq == 1
    tables_p = _rope_tables(jnp.arange(seq))
    tables_s = _rope_tables(jnp.full((dbsz,), PAST_LEN))
    yp = x_prompt.reshape(bsz * seq, d)
    ys = x_sample.reshape(dbsz * dseq, d)
    k_cache = jnp.transpose(cache_attn_k, (0, 1, 3, 4, 2))
    v_cache = jnp.transpose(cache_attn_v, (0, 1, 3, 4, 2))
    new_p, new_s = [], []
    for l in range(depth):
        lp = {k: v[l] for k, v in names.items() if not k.startswith(('expert_w', 'shared_w'))}
        lp['all'], lp['layer'] = names, l
        lp['w_in_bf16'] = w_in[l].astype(BF16)
        s5p = _s5_prep(s5_lambda_re[l], s5_lambda_im[l], s5_log_step[l], s5_b_re[l], s5_b_im[l],
                       s5_c_re[l], s5_c_im[l])
        lp['s5_emb'] = _s5_embed(s5p[:7], s5_d[l])
        zs = jnp.zeros((bsz, S5_GROUPS, S5_STATE), F32)
        z_s = jnp.zeros((bsz, RW_HEADS, HEAD_DIM, HEAD_DIM), F32)
        zrow = jnp.zeros((bsz, 4 * RW_WIDTH), F32)
        yp, st_p = _trunk_layer(yp, bsz, seq, zs, zs, z_s, zrow, None, None, lp, s5p, tables_p, alpha)
        ys, st_s = _trunk_layer(ys, dbsz, dseq, state_s5_re[l], state_s5_im[l], state_rwkv[l],
                                state_rwkv_shift[l], k_cache, v_cache, lp, s5p, tables_s, alpha)
        new_p.append(st_p)
        new_s.append(st_s)
    stack = lambda sts: [jnp.stack([s[i] for s in sts], 0) for i in range(6)]
    out_p, out_s = stack(new_p), stack(new_s)
    assert past == WIN_MAX
    to_rows = lambda c: jnp.transpose(c, (0, 1, 4, 2, 3))
    out_s[4] = to_rows(_cache_shift(k_cache, out_s[4].reshape(depth, dbsz, ATT_WIDTH)))
    out_s[5] = to_rows(_cache_shift(v_cache, out_s[5].reshape(depth, dbsz, ATT_WIDTH)))
    return (yp.reshape(bsz, seq, d), ys.reshape(dbsz, dseq, d), *out_p, *out_s)
```
